```python
import math
import jax, jax.numpy as jnp
from jax import lax
import numpy as np

D_MODEL = 4096
BATCH = 2
SEQ = 4096
DEPTH = 2
DEC_BATCH = 4
DEC_SEQ = 2048
PAST_LEN = 128

EPS = 1e-6
HEAD_DIM = 128
N_BRANCH = 4
SSD_HEADS = 16
SSD_HEAD_DIM = 64
SSD_INNER = SSD_HEADS * SSD_HEAD_DIM
SSD_GROUPS = 2
SSD_STATE = 128
SSD_CONV = 5
SSD_CHUNK = 128
SSD_CONV_DIM = SSD_INNER + 2 * SSD_GROUPS * SSD_STATE
SWA_HEADS = 8
SWA_KV_HEADS = 2
SWA_WIDTH = SWA_HEADS * HEAD_DIM
SWA_KV_WIDTH = SWA_KV_HEADS * HEAD_DIM
SWA_WINDOW = 128
SWA_BLOCK = 128
ROPE_THETA = 10000.0
S5_WIDTH = 1024
S5_GROUP = 16
S5_GROUPS = S5_WIDTH // S5_GROUP
S5_STATE = 64
NA_HEADS = 8
NA_WIDTH = NA_HEADS * HEAD_DIM
GRID_W = 64
NA_KR = 8
NA_KW = 16
D_FF = ((8 * D_MODEL + 3 * 256 - 1) // (3 * 256)) * 256
IN_SIZES = (SSD_INNER, SSD_CONV_DIM, 2 * SSD_HEADS,
            SWA_WIDTH, SWA_KV_WIDTH, SWA_KV_WIDTH,
            S5_WIDTH,
            NA_WIDTH, NA_WIDTH, NA_WIDTH,
            N_BRANCH * D_MODEL)
N_IN = sum(IN_SIZES)

kernel_name = 'hybrid_ssd_swa_s5_natten_encoder'


def rms_norm(x, g):
    xf = x.astype(jnp.float32)
    y = xf * lax.rsqrt(jnp.mean(xf * xf, axis=-1, keepdims=True) + EPS)
    return (y * g.astype(jnp.float32)).astype(x.dtype)


def rope(x):
    l, d = x.shape[1], x.shape[-1]
    half = d // 2
    inv_freq = ROPE_THETA ** (-jnp.arange(half, dtype=jnp.float32) / half)
    ang = jnp.arange(l, dtype=jnp.float32)[:, None] * inv_freq[None, :]
    cos = jnp.cos(ang)[:, None, :]
    sin = jnp.sin(ang)[:, None, :]
    xf = x.astype(jnp.float32)
    x1, x2 = xf[..., :half], xf[..., half:]
    return jnp.concatenate([x1 * cos - x2 * sin, x2 * cos + x1 * sin], axis=-1).astype(x.dtype)


def segsum(a):
    t = a.shape[-1]
    cs = jnp.cumsum(a, axis=-1)
    diff = cs[..., :, None] - cs[..., None, :]
    mask = jnp.tril(jnp.ones((t, t), dtype=bool))
    return jnp.where(mask, diff, -jnp.inf)


def ssd_scan(x, dt, a, bm, cm):
    b, l, h, p = x.shape
    n = bm.shape[-1]
    nc = l // SSD_CHUNK
    xdt = (x * dt[..., None]).reshape(b, nc, SSD_CHUNK, h, p)
    bc = bm.reshape(b, nc, SSD_CHUNK, h, n)
    cc = cm.reshape(b, nc, SSD_CHUNK, h, n)
    da = jnp.transpose((dt * a).reshape(b, nc, SSD_CHUNK, h), (0, 3, 1, 2))
    a_cs = jnp.cumsum(da, axis=-1)
    scores = jnp.einsum('bclhn,bcshn->bhcls', cc, bc) * jnp.exp(segsum(da))
    y_diag = jnp.einsum('bhcls,bcshp->bclhp', scores, xdt)
    decay_to_end = jnp.exp(a_cs[..., -1:] - a_cs)
    states = jnp.einsum('bclhn,bhcl,bclhp->bchpn', bc, decay_to_end, xdt)
    chunk_decay = jnp.exp(segsum(jnp.pad(a_cs[..., -1], ((0, 0), (0, 0), (1, 0)))))
    states = jnp.concatenate([jnp.zeros_like(states[:, :1]), states], axis=1)
    states = jnp.einsum('bhzc,bchpn->bzhpn', chunk_decay, states)[:, :-1]
    y_off = jnp.einsum('bclhn,bchpn,bhcl->bclhp', cc, states, jnp.exp(a_cs))
    return (y_diag + y_off).reshape(b, l, h, p)


def ssd_mixer(z, xbc, dt_raw, conv_w, conv_b, dt_bias, a_log, d_skip, norm_g):
    b, l, _ = xbc.shape
    pad = SSD_CONV // 2
    xbc = lax.conv_general_dilated(xbc, conv_w[:, None, :], window_strides=(1,),
                                   padding=((pad, pad),), dimension_numbers=('NWC', 'WIO', 'NWC'),
                                   feature_group_count=SSD_CONV_DIM) + conv_b
    xbc = jax.nn.silu(xbc).astype(jnp.float32)
    xs = xbc[..., :SSD_INNER].reshape(b, l, SSD_HEADS, SSD_HEAD_DIM)
    rep = SSD_HEADS // SSD_GROUPS
    bm = jnp.repeat(xbc[..., SSD_INNER:SSD_INNER + SSD_GROUPS * SSD_STATE]
                    .reshape(b, l, SSD_GROUPS, SSD_STATE), rep, axis=2)
    cm = jnp.repeat(xbc[..., SSD_INNER + SSD_GROUPS * SSD_STATE:]
                    .reshape(b, l, SSD_GROUPS, SSD_STATE), rep, axis=2)
    dt = jax.nn.softplus(dt_raw.astype(jnp.float32).reshape(b, l, 2, SSD_HEADS)
                         + dt_bias.astype(jnp.float32))
    a = -jnp.exp(a_log.astype(jnp.float32))
    y_fwd = ssd_scan(xs, dt[:, :, 0], a[0], bm, cm)
    flip = lambda t: jnp.flip(t, axis=1)
    y_bwd = flip(ssd_scan(flip(xs), flip(dt[:, :, 1]), a[1], flip(bm), flip(cm)))
    y = y_fwd + y_bwd + d_skip.astype(jnp.float32)[:, None] * xs
    y = y.reshape(b, l, SSD_INNER) * jax.nn.silu(z.astype(jnp.float32))
    return rms_norm(y, norm_g).astype(z.dtype)


def window_attention(q, k, v, q_norm_g, k_norm_g, sink):
    b, l, _ = q.shape
    nb = l // SWA_BLOCK
    grp = SWA_HEADS // SWA_KV_HEADS
    q = rope(rms_norm(q.reshape(b, l, SWA_HEADS, HEAD_DIM), q_norm_g))
    k = rope(rms_norm(k.reshape(b, l, SWA_KV_HEADS, HEAD_DIM), k_norm_g))
    v = v.reshape(b, l, SWA_KV_HEADS, HEAD_DIM)
    qb = q.reshape(b, nb, SWA_BLOCK, SWA_KV_HEADS, grp, HEAD_DIM)
    pad = ((0, 0), (SWA_BLOCK, SWA_BLOCK), (0, 0), (0, 0))

    def band(t):
        tp = jnp.pad(t, pad).reshape(b, nb + 2, SWA_BLOCK, SWA_KV_HEADS, HEAD_DIM)
        return jnp.concatenate([tp[:, :-2], tp[:, 1:-1], tp[:, 2:]], axis=2)

    kw, vw = band(k), band(v)
    s = jnp.einsum('bnqkgd,bnskd->bnkgqs', qb, kw).astype(jnp.float32) * (HEAD_DIM ** -0.5)
    qpos = jnp.arange(nb)[:, None, None] * SWA_BLOCK + jnp.arange(SWA_BLOCK)[None, :, None]
    kpos = jnp.arange(nb)[:, None, None] * SWA_BLOCK - SWA_BLOCK + jnp.arange(3 * SWA_BLOCK)[None, None, :]
    valid = (jnp.abs(kpos - qpos) <= SWA_WINDOW) & (kpos >= 0) & (kpos < l)
    s = jnp.where(valid[None, :, None, None], s, -jnp.inf)
    sk = sink.astype(jnp.float32).reshape(SWA_KV_HEADS, grp)[None, None, :, :, None, None]
    m = jnp.maximum(jnp.max(s, axis=-1, keepdims=True), sk)
    p = jnp.exp(s - m)
    denom = jnp.sum(p, axis=-1, keepdims=True) + jnp.exp(sk - m)
    o = jnp.einsum('bnkgqs,bnskd->bnqkgd', (p / denom).astype(vw.dtype), vw)
    return o.reshape(b, l, SWA_WIDTH)


def _linear_recurrence(e1, e2):
    a1, b1 = e1
    a2, b2 = e2
    return a1 * a2, a2 * b1 + b2


def s5_mixer(u, a_re, a_im, log_step, b_re, b_im, c_re, c_im, d_skip, glu_w, glu_b):
    b, l, _ = u.shape
    f32 = jnp.float32
    uf = u.astype(f32).reshape(b, l, S5_GROUPS, S5_GROUP)
    lam = lax.complex(a_re.astype(f32), a_im.astype(f32))
    step = jnp.exp(log_step.astype(f32))[..., None]
    lam_bar = jnp.exp(lam * step)
    b_c = lax.complex(b_re.astype(f32), b_im.astype(f32))
    b_bar = ((lam_bar - 1.0) / lam)[..., None] * b_c[None]
    c_c = lax.complex(c_re.astype(f32), c_im.astype(f32))

    def run(direction, reverse):
        bu = jnp.einsum('blgi,gpi->blgp', uf, b_bar[direction])
        a = jnp.broadcast_to(lam_bar[direction], bu.shape)
        _, xs = lax.associative_scan(_linear_recurrence, (a, bu), reverse=reverse, axis=1)
        return jnp.real(jnp.einsum('blgp,gip->blgi', xs, c_c[direction]))

    y = run(0, False) + run(1, True) + d_skip.astype(f32).reshape(S5_GROUPS, S5_GROUP) * uf
    g = jax.nn.gelu(y.reshape(b, l, S5_WIDTH))
    out = g * jax.nn.sigmoid(g @ glu_w.astype(f32) + glu_b.astype(f32))
    return out.astype(u.dtype)


def neighborhood_attention(q, k, v, q_norm_g, k_norm_g, rpb):
    b, l, _ = q.shape
    rows = l // GRID_W
    kr = min(NA_KR, rows)
    shp = (b, rows, GRID_W, NA_HEADS, HEAD_DIM)
    q = rms_norm(q.reshape(b, l, NA_HEADS, HEAD_DIM), q_norm_g).reshape(shp)
    k = rms_norm(k.reshape(b, l, NA_HEADS, HEAD_DIM), k_norm_g).reshape(shp)
    v = v.reshape(shp)
    r = jnp.arange(rows)
    row_start = jnp.clip(r - kr // 2, 0, rows - kr)
    row_idx = row_start[:, None] + jnp.arange(kr)[None, :]
    kg = k[:, row_idx].reshape(b, rows, kr * GRID_W, NA_HEADS, HEAD_DIM)
    vg = v[:, row_idx].reshape(b, rows, kr * GRID_W, NA_HEADS, HEAD_DIM)
    s = jnp.einsum('brqhd,brshd->brhqs', q, kg).astype(jnp.float32) * (HEAD_DIM ** -0.5)
    qc = jnp.arange(GRID_W)
    kc = jnp.arange(GRID_W)
    col_start = jnp.clip(qc - NA_KW // 2, 0, GRID_W - NA_KW)
    col_valid = (kc[None, :] >= col_start[:, None]) & (kc[None, :] < col_start[:, None] + NA_KW)
    dr = row_idx - r[:, None] + (NA_KR - 1)
    dc = jnp.clip(kc[None, :] - qc[:, None], -(NA_KW - 1), NA_KW - 1) + (NA_KW - 1)
    bias = rpb.astype(jnp.float32)[:, dr[:, None, :, None], dc[None, :, None, :]]
    bias = jnp.where(col_valid[None, None, :, None, :], bias, -jnp.inf)
    bias = jnp.transpose(bias, (1, 0, 2, 3, 4)).reshape(rows, NA_HEADS, GRID_W, kr * GRID_W)
    p = jax.nn.softmax(s + bias[None], axis=-1)
    o = jnp.einsum('brhqs,brshd->brqhd', p.astype(vg.dtype), vg)
    return o.reshape(b, l, NA_WIDTH)


def block(x, c, ada_w, ada_b, norm1_g, norm2_g, w_in,
          ssd_conv_w, ssd_conv_b, ssd_dt_bias, ssd_a_log, ssd_d, ssd_norm_g,
          swa_q_norm_g, swa_k_norm_g, swa_sink,
          s5_a_re, s5_a_im, s5_log_step, s5_b_re, s5_b_im, s5_c_re, s5_c_im, s5_d, s5_glu_w, s5_glu_b,
          na_q_norm_g, na_k_norm_g, na_rpb,
          w_branch_ssd, w_branch_swa, w_branch_s5, w_branch_na, w_out,
          ffn_w1, ffn_w3, ffn_w2):
    mod = jax.nn.silu(c) @ ada_w + ada_b
    shift1, scale1, gate1, shift2, scale2, gate2 = jnp.split(mod[:, None, :], 6, axis=-1)
    h = rms_norm(x, norm1_g) * (1.0 + scale1) + shift1
    proj = h @ w_in
    splits = np.cumsum(IN_SIZES)[:-1].tolist()
    (z, xbc, dt_raw, q_swa, k_swa, v_swa, u_s5, q_na, k_na, v_na, gate_logits) = jnp.split(proj, splits, axis=-1)
    y_ssd = ssd_mixer(z, xbc, dt_raw, ssd_conv_w, ssd_conv_b, ssd_dt_bias, ssd_a_log, ssd_d, ssd_norm_g) @ w_branch_ssd
    y_swa = window_attention(q_swa, k_swa, v_swa, swa_q_norm_g, swa_k_norm_g, swa_sink) @ w_branch_swa
    y_s5 = s5_mixer(u_s5, s5_a_re, s5_a_im, s5_log_step, s5_b_re, s5_b_im, s5_c_re, s5_c_im,
                    s5_d, s5_glu_w, s5_glu_b) @ w_branch_s5
    y_na = neighborhood_attention(q_na, k_na, v_na, na_q_norm_g, na_k_norm_g, na_rpb) @ w_branch_na
    g_ssd, g_swa, g_s5, g_na = jnp.split(jax.nn.sigmoid(gate_logits), N_BRANCH, axis=-1)
    merged = g_ssd * y_ssd + g_swa * y_swa + g_s5 * y_s5 + g_na * y_na
    x = x + gate1 * (merged @ w_out)
    h2 = rms_norm(x, norm2_g) * (1.0 + scale2) + shift2
    ffn = (jax.nn.silu(h2 @ ffn_w1) * (h2 @ ffn_w3)) @ ffn_w2
    return x + gate2 * ffn


def setup_inputs(seed: int = 0) -> dict:
    key = jax.random.key(seed)
    ks = iter(jax.random.split(key, 48))
    f32 = jnp.float32
    L = DEPTH

    def nrm(shape, scale):
        return jax.random.normal(next(ks), shape, f32) * scale

    def gain(shape):
        return 1.0 + 0.02 * jax.random.normal(next(ks), shape, f32)

    def unif(shape, lo, hi):
        return jax.random.uniform(next(ks), shape, f32, lo, hi)

    x_prompt = nrm((BATCH, SEQ, D_MODEL), 1.0)
    x_sample = nrm((DEC_BATCH, DEC_SEQ, D_MODEL), 1.0)
    c_prompt = nrm((BATCH, D_MODEL), 1.0)
    c_sample = nrm((DEC_BATCH, D_MODEL), 1.0)
    ada_w = nrm((L, D_MODEL, 6 * D_MODEL), 0.2 * D_MODEL ** -0.5)
    ada_b = nrm((L, 6 * D_MODEL), 0.02)
    norm1_g = gain((L, D_MODEL))
    norm2_g = gain((L, D_MODEL))
    w_in = nrm((L, D_MODEL, N_IN), D_MODEL ** -0.5)
    ssd_conv_w = nrm((L, SSD_CONV, SSD_CONV_DIM), SSD_CONV ** -0.5)
    ssd_conv_b = nrm((L, SSD_CONV_DIM), 0.02)
    dt0 = jnp.exp(unif((L, 2, SSD_HEADS), math.log(1e-3), math.log(1e-1)))
    ssd_dt_bias = dt0 + jnp.log(-jnp.expm1(-dt0))
    ssd_a_log = jnp.log(unif((L, 2, SSD_HEADS), 1.0, 16.0))
    ssd_d = gain((L, SSD_HEADS))
    ssd_norm_g = gain((L, SSD_INNER))
    swa_q_norm_g = gain((L, HEAD_DIM))
    swa_k_norm_g = gain((L, HEAD_DIM))
    swa_sink = nrm((L, SWA_HEADS), 0.5)
    s5_a_re = -0.5 + nrm((L, 2, S5_GROUPS, S5_STATE), 0.01)
    s5_a_im = math.pi * jnp.arange(S5_STATE, dtype=f32) + nrm((L, 2, S5_GROUPS, S5_STATE), 0.01)
    s5_log_step = unif((L, 2, S5_GROUPS), math.log(1e-3), math.log(1e-1))
    s5_b_re = nrm((L, S5_GROUPS, S5_STATE, S5_GROUP), (2 * S5_GROUP) ** -0.5)
    s5_b_im = nrm((L, S5_GROUPS, S5_STATE, S5_GROUP), (2 * S5_GROUP) ** -0.5)
    s5_c_re = nrm((L, 2, S5_GROUPS, S5_GROUP, S5_STATE), (2 * S5_STATE) ** -0.5)
    s5_c_im = nrm((L, 2, S5_GROUPS, S5_GROUP, S5_STATE), (2 * S5_STATE) ** -0.5)
    s5_d = nrm((L, S5_WIDTH), 0.5)
    s5_glu_w = nrm((L, S5_WIDTH, S5_WIDTH), S5_WIDTH ** -0.5)
    s5_glu_b = nrm((L, S5_WIDTH), 0.02)
    na_q_norm_g = gain((L, HEAD_DIM))
    na_k_norm_g = gain((L, HEAD_DIM))
    na_rpb = nrm((L, NA_HEADS, 2 * NA_KR - 1, 2 * NA_KW - 1), 0.02)
    w_branch_ssd = nrm((L, SSD_INNER, D_MODEL), SSD_INNER ** -0.5)
    w_branch_swa = nrm((L, SWA_WIDTH, D_MODEL), SWA_WIDTH ** -0.5)
    w_branch_s5 = nrm((L, S5_WIDTH, D_MODEL), S5_WIDTH ** -0.5)
    w_branch_na = nrm((L, NA_WIDTH, D_MODEL), NA_WIDTH ** -0.5)
    w_out = nrm((L, D_MODEL, D_MODEL), D_MODEL ** -0.5)
    ffn_w1 = nrm((L, D_MODEL, D_FF), D_MODEL ** -0.5)
    ffn_w3 = nrm((L, D_MODEL, D_FF), D_MODEL ** -0.5)
    ffn_w2 = nrm((L, D_FF, D_MODEL), D_FF ** -0.5)
    return {'x_prompt': x_prompt, 'x_sample': x_sample, 'c_prompt': c_prompt, 'c_sample': c_sample,
            'ada_w': ada_w, 'ada_b': ada_b, 'norm1_g': norm1_g, 'norm2_g': norm2_g, 'w_in': w_in,
            'ssd_conv_w': ssd_conv_w, 'ssd_conv_b': ssd_conv_b, 'ssd_dt_bias': ssd_dt_bias,
            'ssd_a_log': ssd_a_log, 'ssd_d': ssd_d, 'ssd_norm_g': ssd_norm_g,
            'swa_q_norm_g': swa_q_norm_g, 'swa_k_norm_g': swa_k_norm_g, 'swa_sink': swa_sink,
            's5_a_re': s5_a_re, 's5_a_im': s5_a_im, 's5_log_step': s5_log_step,
            's5_b_re': s5_b_re, 's5_b_im': s5_b_im, 's5_c_re': s5_c_re, 's5_c_im': s5_c_im,
            's5_d': s5_d, 's5_glu_w': s5_glu_w, 's5_glu_b': s5_glu_b,
            'na_q_norm_g': na_q_norm_g, 'na_k_norm_g': na_k_norm_g, 'na_rpb': na_rpb,
            'w_branch_ssd': w_branch_ssd, 'w_branch_swa': w_branch_swa, 'w_branch_s5': w_branch_s5,
            'w_branch_na': w_branch_na, 'w_out': w_out,
            'ffn_w1': ffn_w1, 'ffn_w3': ffn_w3, 'ffn_w2': ffn_w2}


def reference(x_prompt, x_sample, c_prompt, c_sample, ada_w, ada_b, norm1_g, norm2_g, w_in,
              ssd_conv_w, ssd_conv_b, ssd_dt_bias, ssd_a_log, ssd_d, ssd_norm_g,
              swa_q_norm_g, swa_k_norm_g, swa_sink,
              s5_a_re, s5_a_im, s5_log_step, s5_b_re, s5_b_im, s5_c_re, s5_c_im, s5_d, s5_glu_w, s5_glu_b,
              na_q_norm_g, na_k_norm_g, na_rpb,
              w_branch_ssd, w_branch_swa, w_branch_s5, w_branch_na, w_out,
              ffn_w1, ffn_w3, ffn_w2):
    y_prompt = x_prompt
    y_sample = x_sample
    for i in range(DEPTH):
        layer = (ada_w[i], ada_b[i], norm1_g[i], norm2_g[i], w_in[i],
                 ssd_conv_w[i], ssd_conv_b[i], ssd_dt_bias[i], ssd_a_log[i], ssd_d[i], ssd_norm_g[i],
                 swa_q_norm_g[i], swa_k_norm_g[i], swa_sink[i],
                 s5_a_re[i], s5_a_im[i], s5_log_step[i], s5_b_re[i], s5_b_im[i], s5_c_re[i], s5_c_im[i],
                 s5_d[i], s5_glu_w[i], s5_glu_b[i],
                 na_q_norm_g[i], na_k_norm_g[i], na_rpb[i],
                 w_branch_ssd[i], w_branch_swa[i], w_branch_s5[i], w_branch_na[i], w_out[i],
                 ffn_w1[i], ffn_w3[i], ffn_w2[i])
        y_prompt = block(y_prompt, c_prompt, *layer)
        y_sample = block(y_sample, c_sample, *layer)
    return (y_prompt, y_sample)
```

```python
import functools
import math

import jax
import jax.numpy as jnp
import numpy as np
from jax import lax
from jax.experimental import pallas as pl
from jax.experimental.pallas import tpu as pltpu

D_MODEL = 4096
BATCH = 2
SEQ = 4096
DEPTH = 2
DEC_BATCH = 4
DEC_SEQ = 2048
N_TOK = BATCH * SEQ + DEC_BATCH * DEC_SEQ
N_SEQS = BATCH + DEC_BATCH

EPS = 1e-6
HEAD_DIM = 128
N_BRANCH = 4
SSD_HEADS = 16
SSD_HEAD_DIM = 64
SSD_INNER = SSD_HEADS * SSD_HEAD_DIM
SSD_GROUPS = 2
SSD_STATE = 128
SSD_CONV = 5
SSD_CHUNK = 128
SSD_CONV_DIM = SSD_INNER + 2 * SSD_GROUPS * SSD_STATE
SWA_HEADS = 8
SWA_KV_HEADS = 2
SWA_WIDTH = SWA_HEADS * HEAD_DIM
SWA_KV_WIDTH = SWA_KV_HEADS * HEAD_DIM
SWA_WINDOW = 128
SWA_BLOCK = 128
ROPE_THETA = 10000.0
S5_WIDTH = 1024
S5_GROUP = 16
S5_GROUPS = S5_WIDTH // S5_GROUP
S5_STATE = 64
NA_HEADS = 8
NA_WIDTH = NA_HEADS * HEAD_DIM
GRID_W = 64
NA_KR = 8
NA_KW = 16
D_FF = ((8 * D_MODEL + 3 * 256 - 1) // (3 * 256)) * 256
IN_SIZES = (SSD_INNER, SSD_CONV_DIM, 2 * SSD_HEADS,
            SWA_WIDTH, SWA_KV_WIDTH, SWA_KV_WIDTH,
            S5_WIDTH,
            NA_WIDTH, NA_WIDTH, NA_WIDTH,
            N_BRANCH * D_MODEL)
IN_OFFS = tuple(int(v) for v in np.cumsum((0,) + IN_SIZES))

LANES = 128
TM = 1024
D_FF_PAD = ((D_FF + 1023) // 1024) * 1024
N_MIX = IN_OFFS[10] - IN_SIZES[2]
DT_PAD = LANES
VMEM_LIMIT = 56 * 1024 * 1024

F32 = jnp.float32
BF16 = jnp.bfloat16


def _batch_of_tile(i, tile):
    n_p = (BATCH * SEQ) // tile
    return jnp.where(i < n_p, i // (SEQ // tile), BATCH + (i - n_p) // (DEC_SEQ // tile))


def _params(sem):
    return pltpu.CompilerParams(dimension_semantics=sem, vmem_limit_bytes=VMEM_LIMIT)


def _ada_kernel(c_ref, w_ref, b_ref, o_ref):
    c = c_ref[...]
    a = (c * jax.nn.sigmoid(c)).astype(BF16)
    o_ref[...] = jnp.dot(a, w_ref[...].astype(BF16), preferred_element_type=F32) + b_ref[...]


def _ada(c8, ada_w, ada_b):
    n = ada_w.shape[1]
    tn = 512
    return pl.pallas_call(
        _ada_kernel,
        out_shape=jax.ShapeDtypeStruct((8, n), F32),
        grid=(n // tn,),
        in_specs=[pl.BlockSpec((8, D_MODEL), lambda j: (0, 0)),
                  pl.BlockSpec((D_MODEL, tn), lambda j: (0, j)),
                  pl.BlockSpec((1, tn), lambda j: (0, j))],
        out_specs=pl.BlockSpec((8, tn), lambda j: (0, j)),
        compiler_params=_params(("parallel",)),
    )(c8, ada_w, ada_b.reshape(1, n))


def _norm_mod_kernel(x_ref, g_ref, scale_ref, shift_ref, o_ref):
    x = x_ref[...]
    y = x * lax.rsqrt(jnp.mean(x * x, axis=-1, keepdims=True) + EPS)
    y = y * g_ref[...]
    o_ref[...] = (y * (1.0 + scale_ref[...]) + shift_ref[...]).astype(o_ref.dtype)


def _norm_mod(x, g, mod4, scale_idx, shift_idx):
    tr = 256
    return pl.pallas_call(
        _norm_mod_kernel,
        out_shape=jax.ShapeDtypeStruct((N_TOK, D_MODEL), BF16),
        grid=(N_TOK // tr,),
        in_specs=[pl.BlockSpec((tr, D_MODEL), lambda i: (i, 0)),
                  pl.BlockSpec((1, D_MODEL), lambda i: (0, 0)),
                  pl.BlockSpec((None, None, 1, D_MODEL), lambda i: (_batch_of_tile(i, tr), scale_idx, 0, 0)),
                  pl.BlockSpec((None, None, 1, D_MODEL), lambda i: (_batch_of_tile(i, tr), shift_idx, 0, 0))],
        out_specs=pl.BlockSpec((tr, D_MODEL), lambda i: (i, 0)),
        compiler_params=_params(("parallel",)),
    )(x, g.reshape(1, D_MODEL), mod4, mod4)


def _mm_kernel(a_ref, b_ref, o_ref):
    o_ref[...] = jnp.dot(a_ref[...], b_ref[...], preferred_element_type=F32).astype(o_ref.dtype)


def _mm(a, b, tn, out_dtype):
    m, k = a.shape
    n = b.shape[1]
    return pl.pallas_call(
        _mm_kernel,
        out_shape=jax.ShapeDtypeStruct((m, n), out_dtype),
        grid=(m // TM, n // tn),
        in_specs=[pl.BlockSpec((TM, k), lambda i, j: (i, 0)),
                  pl.BlockSpec((k, tn), lambda i, j: (0, j))],
        out_specs=pl.BlockSpec((TM, tn), lambda i, j: (i, j)),
        compiler_params=_params(("parallel", "parallel")),
    )(a, b)


def _merge_kernel(h_ref, wg_ref, m_ref, wb_ref, o_ref, acc_ref):
    b = pl.program_id(2)
    logits = jnp.dot(h_ref[...], wg_ref[...], preferred_element_type=F32)
    y = jnp.dot(m_ref[...], wb_ref[...], preferred_element_type=F32)
    contrib = jax.nn.sigmoid(logits) * y

    @pl.when(b == 0)
    def _():
        acc_ref[...] = contrib

    @pl.when(b > 0)
    def _():
        acc_ref[...] += contrib

    @pl.when(b == N_BRANCH - 1)
    def _():
        o_ref[...] = acc_ref[...].astype(o_ref.dtype)


def _merge(h, wg, mix, wb):
    tn = 512
    w = mix.shape[2]
    return pl.pallas_call(
        _merge_kernel,
        out_shape=jax.ShapeDtypeStruct((N_TOK, D_MODEL), BF16),
        grid=(N_TOK // TM, D_MODEL // tn, N_BRANCH),
        in_specs=[pl.BlockSpec((TM, D_MODEL), lambda i, j, b: (i, 0)),
                  pl.BlockSpec((None, D_MODEL, tn), lambda i, j, b: (b, 0, j)),
                  pl.BlockSpec((None, TM, w), lambda i, j, b: (b, i, 0)),
                  pl.BlockSpec((None, w, tn), lambda i, j, b: (b, 0, j))],
        out_specs=pl.BlockSpec((TM, tn), lambda i, j, b: (i, j)),
        scratch_shapes=[pltpu.VMEM((TM, tn), F32)],
        compiler_params=_params(("parallel", "parallel", "arbitrary")),
    )(h, wg, mix, wb)


def _mm_res_kernel(a_ref, b_ref, x_ref, gate_ref, o_ref, acc_ref, *, nk):
    d = jnp.dot(a_ref[...], b_ref[...], preferred_element_type=F32)
    if nk == 1:
        o_ref[...] = x_ref[...] + gate_ref[...] * d
        return
    k = pl.program_id(2)

    @pl.when(k == 0)
    def _():
        acc_ref[...] = d

    @pl.when(k > 0)
    def _():
        acc_ref[...] += d

    @pl.when(k == nk - 1)
    def _():
        o_ref[...] = x_ref[...] + gate_ref[...] * acc_ref[...]


def _mm_res(a, b, x, mod4, gate_idx, tn, tk):
    m, kdim = a.shape
    n = b.shape[1]
    nk = kdim // tk
    return pl.pallas_call(
        functools.partial(_mm_res_kernel, nk=nk),
        out_shape=jax.ShapeDtypeStruct((m, n), F32),
        grid=(m // TM, n // tn, nk),
        in_specs=[pl.BlockSpec((TM, tk), lambda i, j, k: (i, k)),
                  pl.BlockSpec((tk, tn), lambda i, j, k: (k, j)),
                  pl.BlockSpec((TM, tn), lambda i, j, k: (i, j)),
                  pl.BlockSpec((None, None, 1, tn), lambda i, j, k: (_batch_of_tile(i, TM), gate_idx, 0, j))],
        out_specs=pl.BlockSpec((TM, tn), lambda i, j, k: (i, j)),
        scratch_shapes=[pltpu.VMEM((TM, tn), F32)],
        compiler_params=_params(("parallel", "parallel", "arbitrary")),
    )(a, b, x, mod4)


def _ffn_up_kernel(h_ref, w1_ref, w3_ref, o_ref):
    h = h_ref[...]
    a = jnp.dot(h, w1_ref[...], preferred_element_type=F32)
    b = jnp.dot(h, w3_ref[...], preferred_element_type=F32)
    o_ref[...] = (a * jax.nn.sigmoid(a) * b).astype(o_ref.dtype)


def _ffn_up(h, w1, w3):
    tn = 512
    n = w1.shape[1]
    return pl.pallas_call(
        _ffn_up_kernel,
        out_shape=jax.ShapeDtypeStruct((N_TOK, n), BF16),
        grid=(N_TOK // TM, n // tn),
        in_specs=[pl.BlockSpec((TM, D_MODEL), lambda i, j: (i, 0)),
                  pl.BlockSpec((D_MODEL, tn), lambda i, j: (0, j)),
                  pl.BlockSpec((D_MODEL, tn), lambda i, j: (0, j))],
        out_specs=pl.BlockSpec((TM, tn), lambda i, j: (i, j)),
        compiler_params=_params(("parallel", "parallel")),
    )(h, w1, w3)


def _rms_norm(x, g):
    xf = x.astype(F32)
    y = xf * lax.rsqrt(jnp.mean(xf * xf, axis=-1, keepdims=True) + EPS)
    return (y * g.astype(F32)).astype(x.dtype)


def _rope(x):
    l, d = x.shape[1], x.shape[-1]
    half = d // 2
    inv_freq = ROPE_THETA ** (-jnp.arange(half, dtype=F32) / half)
    ang = jnp.arange(l, dtype=F32)[:, None] * inv_freq[None, :]
    cos = jnp.cos(ang)[:, None, :]
    sin = jnp.sin(ang)[:, None, :]
    x1, x2 = x[..., :half], x[..., half:]
    return jnp.concatenate([x1 * cos - x2 * sin, x2 * cos + x1 * sin], axis=-1)


def _segsum(a):
    t = a.shape[-1]
    cs = jnp.cumsum(a, axis=-1)
    diff = cs[..., :, None] - cs[..., None, :]
    mask = jnp.tril(jnp.ones((t, t), dtype=bool))
    return jnp.where(mask, diff, -jnp.inf)


def _ssd_scan(x, dt, a, bm, cm):
    b, l, h, p = x.shape
    n = bm.shape[-1]
    nc = l // SSD_CHUNK
    xdt = (x * dt[..., None]).reshape(b, nc, SSD_CHUNK, h, p)
    bc = bm.reshape(b, nc, SSD_CHUNK, h, n)
    cc = cm.reshape(b, nc, SSD_CHUNK, h, n)
    da = jnp.transpose((dt * a).reshape(b, nc, SSD_CHUNK, h), (0, 3, 1, 2))
    a_cs = jnp.cumsum(da, axis=-1)
    scores = jnp.einsum('bclhn,bcshn->bhcls', cc, bc) * jnp.exp(_segsum(da))
    y_diag = jnp.einsum('bhcls,bcshp->bclhp', scores, xdt)
    decay_to_end = jnp.exp(a_cs[..., -1:] - a_cs)
    states = jnp.einsum('bclhn,bhcl,bclhp->bchpn', bc, decay_to_end, xdt)
    chunk_decay = jnp.exp(_segsum(jnp.pad(a_cs[..., -1], ((0, 0), (0, 0), (1, 0)))))
    states = jnp.concatenate([jnp.zeros_like(states[:, :1]), states], axis=1)
    states = jnp.einsum('bhzc,bchpn->bzhpn', chunk_decay, states)[:, :-1]
    y_off = jnp.einsum('bclhn,bchpn,bhcl->bclhp', cc, states, jnp.exp(a_cs))
    return (y_diag + y_off).reshape(b, l, h, p)


def _ssd_mixer(z, xbc, dt_raw, conv_w, conv_b, dt_bias, a_log, d_skip, norm_g):
    b, l, _ = xbc.shape
    pad = SSD_CONV // 2
    xbc = lax.conv_general_dilated(xbc, conv_w[:, None, :], window_strides=(1,),
                                   padding=((pad, pad),), dimension_numbers=('NWC', 'WIO', 'NWC'),
                                   feature_group_count=SSD_CONV_DIM) + conv_b
    xbc = jax.nn.silu(xbc)
    xs = xbc[..., :SSD_INNER].reshape(b, l, SSD_HEADS, SSD_HEAD_DIM)
    rep = SSD_HEADS // SSD_GROUPS
    bm = jnp.repeat(xbc[..., SSD_INNER:SSD_INNER + SSD_GROUPS * SSD_STATE]
                    .reshape(b, l, SSD_GROUPS, SSD_STATE), rep, axis=2)
    cm = jnp.repeat(xbc[..., SSD_INNER + SSD_GROUPS * SSD_STATE:]
                    .reshape(b, l, SSD_GROUPS, SSD_STATE), rep, axis=2)
    dt = jax.nn.softplus(dt_raw.reshape(b, l, 2, SSD_HEADS) + dt_bias)
    a = -jnp.exp(a_log)
    y_fwd = _ssd_scan(xs, dt[:, :, 0], a[0], bm, cm)
    flip = lambda t: jnp.flip(t, axis=1)
    y_bwd = flip(_ssd_scan(flip(xs), flip(dt[:, :, 1]), a[1], flip(bm), flip(cm)))
    y = y_fwd + y_bwd + d_skip[:, None] * xs
    y = y.reshape(b, l, SSD_INNER) * jax.nn.silu(z)
    return _rms_norm(y, norm_g)


def _window_attention(q, k, v, q_norm_g, k_norm_g, sink):
    b, l, _ = q.shape
    nb = l // SWA_BLOCK
    grp = SWA_HEADS // SWA_KV_HEADS
    q = _rope(_rms_norm(q.reshape(b, l, SWA_HEADS, HEAD_DIM), q_norm_g))
    k = _rope(_rms_norm(k.reshape(b, l, SWA_KV_HEADS, HEAD_DIM), k_norm_g))
    v = v.reshape(b, l, SWA_KV_HEADS, HEAD_DIM)
    qb = q.reshape(b, nb, SWA_BLOCK, SWA_KV_HEADS, grp, HEAD_DIM)
    pad = ((0, 0), (SWA_BLOCK, SWA_BLOCK), (0, 0), (0, 0))

    def band(t):
        tp = jnp.pad(t, pad).reshape(b, nb + 2, SWA_BLOCK, SWA_KV_HEADS, HEAD_DIM)
        return jnp.concatenate([tp[:, :-2], tp[:, 1:-1], tp[:, 2:]], axis=2)

    kw, vw = band(k), band(v)
    s = jnp.einsum('bnqkgd,bnskd->bnkgqs', qb, kw) * (HEAD_DIM ** -0.5)
    qpos = jnp.arange(nb)[:, None, None] * SWA_BLOCK + jnp.arange(SWA_BLOCK)[None, :, None]
    kpos = jnp.arange(nb)[:, None, None] * SWA_BLOCK - SWA_BLOCK + jnp.arange(3 * SWA_BLOCK)[None, None, :]
    valid = (jnp.abs(kpos - qpos) <= SWA_WINDOW) & (kpos >= 0) & (kpos < l)
    s = jnp.where(valid[None, :, None, None], s, -jnp.inf)
    sk = sink.reshape(SWA_KV_HEADS, grp)[None, None, :, :, None, None]
    m = jnp.maximum(jnp.max(s, axis=-1, keepdims=True), sk)
    p = jnp.exp(s - m)
    denom = jnp.sum(p, axis=-1, keepdims=True) + jnp.exp(sk - m)
    o = jnp.einsum('bnkgqs,bnskd->bnqkgd', p / denom, vw)
    return o.reshape(b, l, SWA_WIDTH)


def _linear_recurrence(e1, e2):
    a1, b1 = e1
    a2, b2 = e2
    return a1 * a2, a2 * b1 + b2


def _s5_mixer(u, a_re, a_im, log_step, b_re, b_im, c_re, c_im, d_skip, glu_w, glu_b):
    b, l, _ = u.shape
    uf = u.reshape(b, l, S5_GROUPS, S5_GROUP)
    lam = lax.complex(a_re, a_im)
    step = jnp.exp(log_step)[..., None]
    lam_bar = jnp.exp(lam * step)
    b_c = lax.complex(b_re, b_im)
    b_bar = ((lam_bar - 1.0) / lam)[..., None] * b_c[None]
    c_c = lax.complex(c_re, c_im)

    def run(direction, reverse):
        bu = jnp.einsum('blgi,gpi->blgp', uf, b_bar[direction])
        a = jnp.broadcast_to(lam_bar[direction], bu.shape)
        _, xs = lax.associative_scan(_linear_recurrence, (a, bu), reverse=reverse, axis=1)
        return jnp.real(jnp.einsum('blgp,gip->blgi', xs, c_c[direction]))

    y = run(0, False) + run(1, True) + d_skip.reshape(S5_GROUPS, S5_GROUP) * uf
    g = jax.nn.gelu(y.reshape(b, l, S5_WIDTH))
    return g * jax.nn.sigmoid(g @ glu_w + glu_b)


def _neighborhood_attention(q, k, v, q_norm_g, k_norm_g, rpb):
    b, l, _ = q.shape
    rows = l // GRID_W
    kr = min(NA_KR, rows)
    shp = (b, rows, GRID_W, NA_HEADS, HEAD_DIM)
    q = _rms_norm(q.reshape(b, l, NA_HEADS, HEAD_DIM), q_norm_g).reshape(shp)
    k = _rms_norm(k.reshape(b, l, NA_HEADS, HEAD_DIM), k_norm_g).reshape(shp)
    v = v.reshape(shp)
    r = jnp.arange(rows)
    row_start = jnp.clip(r - kr // 2, 0, rows - kr)
    row_idx = row_start[:, None] + jnp.arange(kr)[None, :]
    kg = k[:, row_idx].reshape(b, rows, kr * GRID_W, NA_HEADS, HEAD_DIM)
    vg = v[:, row_idx].reshape(b, rows, kr * GRID_W, NA_HEADS, HEAD_DIM)
    s = jnp.einsum('brqhd,brshd->brhqs', q, kg) * (HEAD_DIM ** -0.5)
    qc = jnp.arange(GRID_W)
    kc = jnp.arange(GRID_W)
    col_start = jnp.clip(qc - NA_KW // 2, 0, GRID_W - NA_KW)
    col_valid = (kc[None, :] >= col_start[:, None]) & (kc[None, :] < col_start[:, None] + NA_KW)
    dr = row_idx - r[:, None] + (NA_KR - 1)
    dc = jnp.clip(kc[None, :] - qc[:, None], -(NA_KW - 1), NA_KW - 1) + (NA_KW - 1)
    bias = rpb[:, dr[:, None, :, None], dc[None, :, None, :]]
    bias = jnp.where(col_valid[None, None, :, None, :], bias, -jnp.inf)
    bias = jnp.transpose(bias, (1, 0, 2, 3, 4)).reshape(rows, NA_HEADS, GRID_W, kr * GRID_W)
    p = jax.nn.softmax(s + bias[None], axis=-1)
    o = jnp.einsum('brhqs,brshd->brqhd', p, vg)
    return o.reshape(b, l, NA_WIDTH)


def _mixers_group(pm, dt_raw, b, l, lw):
    pm = pm.reshape(b, l, N_MIX)
    z = pm[..., 0:1024]
    xbc = pm[..., 1024:2560]
    q_swa = pm[..., 2560:3584]
    k_swa = pm[..., 3584:3840]
    v_swa = pm[..., 3840:4096]
    u_s5 = pm[..., 4096:5120]
    q_na = pm[..., 5120:6144]
    k_na = pm[..., 6144:7168]
    v_na = pm[..., 7168:8192]
    dt_raw = dt_raw.reshape(b, l, 2 * SSD_HEADS)
    y_ssd = _ssd_mixer(z, xbc, dt_raw, lw['ssd_conv_w'], lw['ssd_conv_b'], lw['ssd_dt_bias'],
                       lw['ssd_a_log'], lw['ssd_d'], lw['ssd_norm_g'])
    y_swa = _window_attention(q_swa, k_swa, v_swa, lw['swa_q_norm_g'], lw['swa_k_norm_g'], lw['swa_sink'])
    y_s5 = _s5_mixer(u_s5, lw['s5_a_re'], lw['s5_a_im'], lw['s5_log_step'], lw['s5_b_re'], lw['s5_b_im'],
                     lw['s5_c_re'], lw['s5_c_im'], lw['s5_d'], lw['s5_glu_w'], lw['s5_glu_b'])
    y_na = _neighborhood_attention(q_na, k_na, v_na, lw['na_q_norm_g'], lw['na_k_norm_g'], lw['na_rpb'])
    return [t.reshape(b * l, -1) for t in (y_ssd, y_swa, y_s5, y_na)]


def _layer(x, c8, lw):
    w_in = lw['w_in']
    o = IN_OFFS
    w_mix = jnp.concatenate([w_in[:, :o[2]], w_in[:, o[3]:o[10]]], axis=1).astype(BF16)
    w_dt = jnp.pad(w_in[:, o[2]:o[3]], ((0, 0), (0, DT_PAD - IN_SIZES[2]))).astype(BF16)
    w_gate = jnp.transpose(w_in[:, o[10]:].reshape(D_MODEL, N_BRANCH, D_MODEL), (1, 0, 2)).astype(BF16)
    w_branch = jnp.stack([lw['w_branch_ssd'], lw['w_branch_swa'], lw['w_branch_s5'],
                          lw['w_branch_na']]).astype(BF16)
    w_out = lw['w_out'].astype(BF16)
    ff_pad = ((0, 0), (0, D_FF_PAD - D_FF))
    w1 = jnp.pad(lw['ffn_w1'], ff_pad).astype(BF16)
    w3 = jnp.pad(lw['ffn_w3'], ff_pad).astype(BF16)
    w2 = jnp.pad(lw['ffn_w2'], ((0, D_FF_PAD - D_FF), (0, 0))).astype(BF16)

    mod = _ada(c8, lw['ada_w'], lw['ada_b'])
    mod4 = mod.reshape(8, 6, 1, D_MODEL)

    h = _norm_mod(x, lw['norm1_g'], mod4, 1, 0)
    pm = _mm(h, w_mix, 1024, F32)
    dt_raw = _mm(h, w_dt, DT_PAD, F32)[:, :IN_SIZES[2]]

    n_p = BATCH * SEQ
    mp = _mixers_group(pm[:n_p], dt_raw[:n_p], BATCH, SEQ, lw)
    ms = _mixers_group(pm[n_p:], dt_raw[n_p:], DEC_BATCH, DEC_SEQ, lw)
    mix = jnp.stack([jnp.concatenate([a, b], axis=0) for a, b in zip(mp, ms)]).astype(BF16)

    merged = _merge(h, w_gate, mix, w_branch)
    x = _mm_res(merged, w_out, x, mod4, 2, 1024, D_MODEL)

    h2 = _norm_mod(x, lw['norm2_g'], mod4, 4, 3)
    u = _ffn_up(h2, w1, w3)
    return _mm_res(u, w2, x, mod4, 5, 1024, D_FF_PAD // 4)


_LAYER_KEYS = ('ada_w', 'ada_b', 'norm1_g', 'norm2_g', 'w_in',
               'ssd_conv_w', 'ssd_conv_b', 'ssd_dt_bias', 'ssd_a_log', 'ssd_d', 'ssd_norm_g',
               'swa_q_norm_g', 'swa_k_norm_g', 'swa_sink',
               's5_a_re', 's5_a_im', 's5_log_step', 's5_b_re', 's5_b_im', 's5_c_re', 's5_c_im',
               's5_d', 's5_glu_w', 's5_glu_b',
               'na_q_norm_g', 'na_k_norm_g', 'na_rpb',
               'w_branch_ssd', 'w_branch_swa', 'w_branch_s5', 'w_branch_na', 'w_out',
               'ffn_w1', 'ffn_w3', 'ffn_w2')


def kernel(x_prompt, x_sample, c_prompt, c_sample, ada_w, ada_b, norm1_g, norm2_g, w_in, ssd_conv_w, ssd_conv_b, ssd_dt_bias, ssd_a_log, ssd_d, ssd_norm_g, swa_q_norm_g, swa_k_norm_g, swa_sink, s5_a_re, s5_a_im, s5_log_step, s5_b_re, s5_b_im, s5_c_re, s5_c_im, s5_d, s5_glu_w, s5_glu_b, na_q_norm_g, na_k_norm_g, na_rpb, w_branch_ssd, w_branch_swa, w_branch_s5, w_branch_na, w_out, ffn_w1, ffn_w3, ffn_w2):
    stacked = dict(zip(_LAYER_KEYS, (ada_w, ada_b, norm1_g, norm2_g, w_in,
                                     ssd_conv_w, ssd_conv_b, ssd_dt_bias, ssd_a_log, ssd_d, ssd_norm_g,
                                     swa_q_norm_g, swa_k_norm_g, swa_sink,
                                     s5_a_re, s5_a_im, s5_log_step, s5_b_re, s5_b_im, s5_c_re, s5_c_im,
                                     s5_d, s5_glu_w, s5_glu_b,
                                     na_q_norm_g, na_k_norm_g, na_rpb,
                                     w_branch_ssd, w_branch_swa, w_branch_s5, w_branch_na, w_out,
                                     ffn_w1, ffn_w3, ffn_w2)))
    x = jnp.concatenate([x_prompt.reshape(BATCH * SEQ, D_MODEL),
                         x_sample.reshape(DEC_BATCH * DEC_SEQ, D_MODEL)], axis=0)
    c8 = jnp.concatenate([c_prompt, c_sample, jnp.zeros((8 - N_SEQS, D_MODEL), F32)], axis=0)
    for i in range(DEPTH):
        x = _layer(x, c8, {k: v[i] for k, v in stacked.items()})
    n_p = BATCH * SEQ
    return (x[:n_p].reshape(BATCH, SEQ, D_MODEL), x[n_p:].reshape(DEC_BATCH, DEC_SEQ, D_MODEL))
```

```python
import functools

import jax
import jax.numpy as jnp
import numpy as np
from jax import lax
from jax.experimental import pallas as pl
from jax.experimental.pallas import tpu as pltpu

D_MODEL = 4096
BATCH = 2
SEQ = 4096
DEPTH = 2
DEC_BATCH = 4
DEC_SEQ = 2048
N_TOK = BATCH * SEQ + DEC_BATCH * DEC_SEQ
N_SEQS = BATCH + DEC_BATCH
GROUPS = ((0, BATCH, SEQ), (BATCH * SEQ, DEC_BATCH, DEC_SEQ))
SEGS = tuple((row0 + i * l, l) for row0, b, l in GROUPS for i in range(b))

EPS = 1e-6
HEAD_DIM = 128
N_BRANCH = 4
SSD_HEADS = 16
SSD_HEAD_DIM = 64
SSD_INNER = SSD_HEADS * SSD_HEAD_DIM
SSD_GROUPS = 2
SSD_STATE = 128
SSD_CONV = 5
SSD_CHUNK = 128
SSD_CONV_DIM = SSD_INNER + 2 * SSD_GROUPS * SSD_STATE
SWA_HEADS = 8
SWA_KV_HEADS = 2
SWA_WIDTH = SWA_HEADS * HEAD_DIM
SWA_KV_WIDTH = SWA_KV_HEADS * HEAD_DIM
SWA_WINDOW = 128
SWA_BLOCK = 128
ROPE_THETA = 10000.0
S5_WIDTH = 1024
S5_GROUP = 16
S5_GROUPS = S5_WIDTH // S5_GROUP
S5_STATE = 64
NA_HEADS = 8
NA_WIDTH = NA_HEADS * HEAD_DIM
GRID_W = 64
NA_KR = 8
NA_KW = 16
D_FF = ((8 * D_MODEL + 3 * 256 - 1) // (3 * 256)) * 256
IN_SIZES = (SSD_INNER, SSD_CONV_DIM, 2 * SSD_HEADS,
            SWA_WIDTH, SWA_KV_WIDTH, SWA_KV_WIDTH,
            S5_WIDTH,
            NA_WIDTH, NA_WIDTH, NA_WIDTH,
            N_BRANCH * D_MODEL)
IN_OFFS = tuple(int(v) for v in np.cumsum((0,) + IN_SIZES))

LANES = 128
TM = 1024
D_FF_PAD = ((D_FF + 1023) // 1024) * 1024
N_MIX = IN_OFFS[10] - IN_SIZES[2]
VMEM_LIMIT = 56 * 1024 * 1024

COL_Z, COL_Q_SWA, COL_U_S5, COL_Q_NA, COL_K_NA, COL_V_NA = 0, 1024, 2048, 3072, 4096, 5120
COL_XBC = 6144
COL_K_SWA = COL_XBC + SSD_CONV_DIM
COL_V_SWA = COL_K_SWA + SWA_KV_WIDTH

F32 = jnp.float32
BF16 = jnp.bfloat16


def _batch_of_tile(i, tile):
    n_p = (BATCH * SEQ) // tile
    return jnp.where(i < n_p, i // (SEQ // tile), BATCH + (i - n_p) // (DEC_SEQ // tile))


def _params(sem):
    return pltpu.CompilerParams(dimension_semantics=sem, vmem_limit_bytes=VMEM_LIMIT)


def _ada_kernel(c_ref, w_ref, b_ref, o_ref):
    c = c_ref[...]
    a = (c * jax.nn.sigmoid(c)).astype(BF16)
    o_ref[...] = jnp.dot(a, w_ref[...].astype(BF16), preferred_element_type=F32) + b_ref[...]


def _ada(c8, ada_w, ada_b):
    n = ada_w.shape[1]
    tn = 512
    return pl.pallas_call(
        _ada_kernel,
        out_shape=jax.ShapeDtypeStruct((8, n), F32),
        grid=(n // tn,),
        in_specs=[pl.BlockSpec((8, D_MODEL), lambda j: (0, 0)),
                  pl.BlockSpec((D_MODEL, tn), lambda j: (0, j)),
                  pl.BlockSpec((1, tn), lambda j: (0, j))],
        out_specs=pl.BlockSpec((8, tn), lambda j: (0, j)),
        compiler_params=_params(("parallel",)),
    )(c8, ada_w, ada_b.reshape(1, n))


def _norm_mod_kernel(x_ref, g_ref, scale_ref, shift_ref, o_ref):
    x = x_ref[...]
    y = x * lax.rsqrt(jnp.mean(x * x, axis=-1, keepdims=True) + EPS)
    y = y * g_ref[...]
    o_ref[...] = (y * (1.0 + scale_ref[...]) + shift_ref[...]).astype(o_ref.dtype)


def _norm_mod(x, g, mod4, scale_idx, shift_idx):
    tr = 256
    return pl.pallas_call(
        _norm_mod_kernel,
        out_shape=jax.ShapeDtypeStruct((N_TOK, D_MODEL), BF16),
        grid=(N_TOK // tr,),
        in_specs=[pl.BlockSpec((tr, D_MODEL), lambda i: (i, 0)),
                  pl.BlockSpec((1, D_MODEL), lambda i: (0, 0)),
                  pl.BlockSpec((None, None, 1, D_MODEL), lambda i: (_batch_of_tile(i, tr), scale_idx, 0, 0)),
                  pl.BlockSpec((None, None, 1, D_MODEL), lambda i: (_batch_of_tile(i, tr), shift_idx, 0, 0))],
        out_specs=pl.BlockSpec((tr, D_MODEL), lambda i: (i, 0)),
        compiler_params=_params(("parallel",)),
    )(x, g.reshape(1, D_MODEL), mod4, mod4)


def _mm_kernel(a_ref, b_ref, o_ref):
    o_ref[...] = jnp.dot(a_ref[...], b_ref[...], preferred_element_type=F32).astype(o_ref.dtype)


def _mm(a, b, tn, out_dtype):
    m, k = a.shape
    n = b.shape[1]
    return pl.pallas_call(
        _mm_kernel,
        out_shape=jax.ShapeDtypeStruct((m, n), out_dtype),
        grid=(m // TM, n // tn),
        in_specs=[pl.BlockSpec((TM, k), lambda i, j: (i, 0)),
                  pl.BlockSpec((k, tn), lambda i, j: (0, j))],
        out_specs=pl.BlockSpec((TM, tn), lambda i, j: (i, j)),
        compiler_params=_params(("parallel", "parallel")),
    )(a, b)


def _merge_kernel(h_ref, wg_ref, m_ref, wb_ref, o_ref, acc_ref):
    b = pl.program_id(2)
    logits = jnp.dot(h_ref[...], wg_ref[...], preferred_element_type=F32)
    y = jnp.dot(m_ref[...], wb_ref[...], preferred_element_type=F32)
    contrib = jax.nn.sigmoid(logits) * y

    @pl.when(b == 0)
    def _():
        acc_ref[...] = contrib

    @pl.when(b > 0)
    def _():
        acc_ref[...] += contrib

    @pl.when(b == N_BRANCH - 1)
    def _():
        o_ref[...] = acc_ref[...].astype(o_ref.dtype)


def _merge(h, wg, mix, wb):
    tn = 512
    w = mix.shape[2]
    return pl.pallas_call(
        _merge_kernel,
        out_shape=jax.ShapeDtypeStruct((N_TOK, D_MODEL), BF16),
        grid=(N_TOK // TM, D_MODEL // tn, N_BRANCH),
        in_specs=[pl.BlockSpec((TM, D_MODEL), lambda i, j, b: (i, 0)),
                  pl.BlockSpec((None, D_MODEL, tn), lambda i, j, b: (b, 0, j)),
                  pl.BlockSpec((None, TM, w), lambda i, j, b: (b, i, 0)),
                  pl.BlockSpec((None, w, tn), lambda i, j, b: (b, 0, j))],
        out_specs=pl.BlockSpec((TM, tn), lambda i, j, b: (i, j)),
        scratch_shapes=[pltpu.VMEM((TM, tn), F32)],
        compiler_params=_params(("parallel", "parallel", "arbitrary")),
    )(h, wg, mix, wb)


def _mm_res_kernel(a_ref, b_ref, x_ref, gate_ref, o_ref, acc_ref, *, nk):
    d = jnp.dot(a_ref[...], b_ref[...], preferred_element_type=F32)
    if nk == 1:
        o_ref[...] = x_ref[...] + gate_ref[...] * d
        return
    k = pl.program_id(2)

    @pl.when(k == 0)
    def _():
        acc_ref[...] = d

    @pl.when(k > 0)
    def _():
        acc_ref[...] += d

    @pl.when(k == nk - 1)
    def _():
        o_ref[...] = x_ref[...] + gate_ref[...] * acc_ref[...]


def _mm_res(a, b, x, mod4, gate_idx, tn, tk):
    m, kdim = a.shape
    n = b.shape[1]
    nk = kdim // tk
    return pl.pallas_call(
        functools.partial(_mm_res_kernel, nk=nk),
        out_shape=jax.ShapeDtypeStruct((m, n), F32),
        grid=(m // TM, n // tn, nk),
        in_specs=[pl.BlockSpec((TM, tk), lambda i, j, k: (i, k)),
                  pl.BlockSpec((tk, tn), lambda i, j, k: (k, j)),
                  pl.BlockSpec((TM, tn), lambda i, j, k: (i, j)),
                  pl.BlockSpec((None, None, 1, tn), lambda i, j, k: (_batch_of_tile(i, TM), gate_idx, 0, j))],
        out_specs=pl.BlockSpec((TM, tn), lambda i, j, k: (i, j)),
        scratch_shapes=[pltpu.VMEM((TM, tn), F32)],
        compiler_params=_params(("parallel", "parallel", "arbitrary")),
    )(a, b, x, mod4)


def _ffn_up_kernel(h_ref, w1_ref, w3_ref, o_ref):
    h = h_ref[...]
    a = jnp.dot(h, w1_ref[...], preferred_element_type=F32)
    b = jnp.dot(h, w3_ref[...], preferred_element_type=F32)
    o_ref[...] = (a * jax.nn.sigmoid(a) * b).astype(o_ref.dtype)


def _ffn_up(h, w1, w3):
    tn = 512
    n = w1.shape[1]
    return pl.pallas_call(
        _ffn_up_kernel,
        out_shape=jax.ShapeDtypeStruct((N_TOK, n), BF16),
        grid=(N_TOK // TM, n // tn),
        in_specs=[pl.BlockSpec((TM, D_MODEL), lambda i, j: (i, 0)),
                  pl.BlockSpec((D_MODEL, tn), lambda i, j: (0, j)),
                  pl.BlockSpec((D_MODEL, tn), lambda i, j: (0, j))],
        out_specs=pl.BlockSpec((TM, tn), lambda i, j: (i, j)),
        compiler_params=_params(("parallel", "parallel")),
    )(h, w1, w3)


def _tile_tables(segs, tile):
    pos, flags = [], []
    for _, length in segs:
        n = length // tile
        for j in range(n):
            pos.append(j)
            flags.append((1 if j == 0 else 0) | (2 if j == n - 1 else 0))
    return np.asarray(pos, np.int32), np.asarray(flags, np.int32)


def _head_slice(h):
    return slice(h * HEAD_DIM, (h + 1) * HEAD_DIM)


def _qk_prep_kernel(pos_ref, qs_ref, ks_ref, qn_ref, kn_ref, vn_ref, cos_ref, sin_ref, g_ref,
                    oqs_ref, oks_ref, oqn_ref, okn_ref, ovn_ref):
    del pos_ref
    cos = cos_ref[...]
    sin = sin_ref[...]

    def hnorm(x, g):
        return x * lax.rsqrt(jnp.mean(x * x, axis=-1, keepdims=True) + EPS) * g

    def rope(x):
        return x * cos + pltpu.roll(x, HEAD_DIM // 2, 1) * sin

    for h in range(SWA_HEADS):
        oqs_ref[:, _head_slice(h)] = rope(hnorm(qs_ref[:, _head_slice(h)], g_ref[0:1, :])).astype(BF16)
    for h in range(SWA_KV_HEADS):
        oks_ref[:, _head_slice(h)] = rope(hnorm(ks_ref[:, _head_slice(h)], g_ref[1:2, :])).astype(BF16)
    for h in range(NA_HEADS):
        oqn_ref[:, _head_slice(h)] = hnorm(qn_ref[:, _head_slice(h)], g_ref[2:3, :]).astype(BF16)
        okn_ref[:, _head_slice(h)] = hnorm(kn_ref[:, _head_slice(h)], g_ref[3:4, :]).astype(BF16)
    ovn_ref[...] = vn_ref[...].astype(BF16)


def _rope_tables(max_len):
    half = HEAD_DIM // 2
    inv_freq = ROPE_THETA ** (-jnp.arange(half, dtype=F32) / half)
    ang = jnp.arange(max_len, dtype=F32)[:, None] * inv_freq[None, :]
    cos, sin = jnp.cos(ang), jnp.sin(ang)
    return jnp.concatenate([cos, cos], axis=1), jnp.concatenate([-sin, sin], axis=1)


def _qk_prep(pm, segs, lw):
    n_tok = pm.shape[0]
    tr = 256
    pos, _ = _tile_tables(segs, tr)
    cos, sin = _rope_tables(max(l for _, l in segs))
    gains = jnp.concatenate([lw['swa_q_norm_g'][None], lw['swa_k_norm_g'][None],
                             lw['na_q_norm_g'][None], lw['na_k_norm_g'][None],
                             jnp.zeros((4, HEAD_DIM), F32)], axis=0)
    wide = lambda cb: pl.BlockSpec((tr, 1024), lambda i, p: (i, cb))
    narrow = lambda cb: pl.BlockSpec((tr, 256), lambda i, p: (i, cb))
    grid_spec = pltpu.PrefetchScalarGridSpec(
        num_scalar_prefetch=1, grid=(n_tok // tr,),
        in_specs=[wide(COL_Q_SWA // 1024), narrow(COL_K_SWA // 256), wide(COL_Q_NA // 1024),
                  wide(COL_K_NA // 1024), wide(COL_V_NA // 1024),
                  pl.BlockSpec((tr, HEAD_DIM), lambda i, p: (p[i], 0)),
                  pl.BlockSpec((tr, HEAD_DIM), lambda i, p: (p[i], 0)),
                  pl.BlockSpec((8, HEAD_DIM), lambda i, p: (0, 0))],
        out_specs=[pl.BlockSpec((tr, 1024), lambda i, p: (i, 0)),
                   pl.BlockSpec((tr, 256), lambda i, p: (i, 0)),
                   pl.BlockSpec((tr, 1024), lambda i, p: (i, 0)),
                   pl.BlockSpec((tr, 1024), lambda i, p: (i, 0)),
                   pl.BlockSpec((tr, 1024), lambda i, p: (i, 0))])
    sds = lambda w: jax.ShapeDtypeStruct((n_tok, w), BF16)
    return pl.pallas_call(
        _qk_prep_kernel, grid_spec=grid_spec,
        out_shape=[sds(1024), sds(256), sds(1024), sds(1024), sds(1024)],
        compiler_params=_params(("parallel",)),
    )(jnp.asarray(pos), pm, pm, pm, pm, pm, cos, sin, gains)


SWA_TQ = 512
SWA_GRP = SWA_HEADS // SWA_KV_HEADS


def _swa_kernel(flags_ref, sink_ref, q_ref, kc_ref, kp_ref, kn_ref, vc_ref, vp_ref, vn_ref, o_ref, *, tq):
    fl = flags_ref[pl.program_id(0)]
    lo = jnp.where((fl & 1) > 0, SWA_BLOCK, 0)
    hi = jnp.where((fl & 2) > 0, 2 * SWA_BLOCK, 3 * SWA_BLOCK)
    nqb = tq // SWA_BLOCK
    m_rows = SWA_GRP * SWA_BLOCK
    row = lax.broadcasted_iota(jnp.int32, (m_rows, 3 * SWA_BLOCK), 0) & (SWA_BLOCK - 1)
    col = lax.broadcasted_iota(jnp.int32, (m_rows, 3 * SWA_BLOCK), 1)
    band = (col >= row) & (col <= row + 2 * SWA_WINDOW)
    scale = HEAD_DIM ** -0.5
    for g in range(SWA_KV_HEADS):
        ks = _head_slice(g)
        k_ext = jnp.concatenate([kp_ref[:, ks], kc_ref[:, ks], kn_ref[:, ks]], axis=0)
        v_ext = jnp.concatenate([vp_ref[:, ks], vc_ref[:, ks], vn_ref[:, ks]], axis=0).astype(BF16)
        sk = jnp.concatenate([jnp.full((SWA_BLOCK, 1), sink_ref[g * SWA_GRP + h], F32)
                              for h in range(SWA_GRP)], axis=0)
        for qb in range(nqb):
            rows = slice(qb * SWA_BLOCK, (qb + 1) * SWA_BLOCK)
            q = jnp.concatenate([q_ref[rows, _head_slice(g * SWA_GRP + h)] for h in range(SWA_GRP)], axis=0)
            keys = k_ext[qb * SWA_BLOCK:(qb + 3) * SWA_BLOCK]
            vals = v_ext[qb * SWA_BLOCK:(qb + 3) * SWA_BLOCK]
            s = lax.dot_general(q, keys, (((1,), (1,)), ((), ())), preferred_element_type=F32) * scale
            mask = band
            if qb == 0:
                mask = mask & (col >= lo)
            if qb == nqb - 1:
                mask = mask & (col < hi)
            s = jnp.where(mask, s, -jnp.inf)
            m = jnp.maximum(jnp.max(s, axis=-1, keepdims=True), sk)
            p = jnp.exp(s - m)
            denom = jnp.sum(p, axis=-1, keepdims=True) + jnp.exp(sk - m)
            o = jnp.dot((p / denom).astype(BF16), vals, preferred_element_type=F32)
            for h in range(SWA_GRP):
                o_ref[rows, _head_slice(g * SWA_GRP + h)] = o[h * SWA_BLOCK:(h + 1) * SWA_BLOCK].astype(o_ref.dtype)


def _swa(qs, ks, pm, sink, segs):
    n_tok = qs.shape[0]
    tq = min(SWA_TQ, min(l for _, l in segs))
    _, flags = _tile_tables(segs, tq)
    nb = tq // SWA_BLOCK
    last_blk = n_tok // SWA_BLOCK - 1
    prev_map = lambda cb: (lambda i, f: (jnp.maximum(i * nb - 1, 0), cb))
    next_map = lambda cb: (lambda i, f: (jnp.minimum((i + 1) * nb, last_blk), cb))
    vcb = COL_V_SWA // SWA_KV_WIDTH
    grid_spec = pltpu.PrefetchScalarGridSpec(
        num_scalar_prefetch=1, grid=(n_tok // tq,),
        in_specs=[pl.BlockSpec(memory_space=pltpu.SMEM),
                  pl.BlockSpec((tq, SWA_WIDTH), lambda i, f: (i, 0)),
                  pl.BlockSpec((tq, SWA_KV_WIDTH), lambda i, f: (i, 0)),
                  pl.BlockSpec((SWA_BLOCK, SWA_KV_WIDTH), prev_map(0)),
                  pl.BlockSpec((SWA_BLOCK, SWA_KV_WIDTH), next_map(0)),
                  pl.BlockSpec((tq, SWA_KV_WIDTH), lambda i, f: (i, vcb)),
                  pl.BlockSpec((SWA_BLOCK, SWA_KV_WIDTH), prev_map(vcb)),
                  pl.BlockSpec((SWA_BLOCK, SWA_KV_WIDTH), next_map(vcb))],
        out_specs=pl.BlockSpec((tq, SWA_WIDTH), lambda i, f: (i, 0)))
    return pl.pallas_call(
        functools.partial(_swa_kernel, tq=tq), grid_spec=grid_spec,
        out_shape=jax.ShapeDtypeStruct((n_tok, SWA_WIDTH), BF16),
        compiler_params=_params(("parallel",)),
    )(jnp.asarray(flags), sink, qs, ks, ks, ks, pm, pm, pm)


NA_R = NA_KR // 2


def _na_bias_table(rpb):
    p_of = np.array([np.arange(NA_R), np.full(NA_R, NA_KR // 2), NA_KR // 2 + np.arange(NA_R)])
    dr = np.arange(NA_KR)[None, None, :] - p_of[:, :, None] + (NA_KR - 1)
    qc = np.arange(GRID_W)
    kc = np.arange(GRID_W)
    dc = np.clip(kc[None, :] - qc[:, None], -(NA_KW - 1), NA_KW - 1) + (NA_KW - 1)
    col_start = np.clip(qc - NA_KW // 2, 0, GRID_W - NA_KW)
    valid = (kc[None, :] >= col_start[:, None]) & (kc[None, :] < col_start[:, None] + NA_KW)
    bias = rpb[:, dr[:, :, None, :, None], dc[None, None, :, None, :]]
    bias = jnp.where(valid[None, None, None, :, None, :], bias, -jnp.inf)
    return jnp.transpose(bias, (1, 2, 0, 3, 4, 5)).reshape(3, NA_R, NA_HEADS, GRID_W, NA_KR * GRID_W)


def _na_kernel(q_ref, k_ref, v_ref, bias_ref, o_ref, *, rows):
    step = pl.program_id(1)
    scale = HEAD_DIM ** -0.5
    for i in range(NA_R):
        r = step * NA_R + i
        start = pl.multiple_of(jnp.clip(r - NA_KR // 2, 0, rows - NA_KR) * GRID_W, GRID_W)
        qrows = slice(i * GRID_W, (i + 1) * GRID_W)
        for h in range(NA_HEADS):
            q = q_ref[qrows, _head_slice(h)]
            k = k_ref[pl.ds(start, NA_KR * GRID_W), _head_slice(h)]
            v = v_ref[pl.ds(start, NA_KR * GRID_W), _head_slice(h)]
            s = lax.dot_general(q, k, (((1,), (1,)), ((), ())), preferred_element_type=F32) * scale
            s = s + bias_ref[i, h]
            p = jnp.exp(s - jnp.max(s, axis=-1, keepdims=True))
            p = p / jnp.sum(p, axis=-1, keepdims=True)
            o = jnp.dot(p.astype(BF16), v, preferred_element_type=F32)
            o_ref[qrows, _head_slice(h)] = o.astype(o_ref.dtype)


def _na_group(qn, kn, vn, bias, row0, b, l):
    rows = l // GRID_W
    nsteps = rows // NA_R
    tq = NA_R * GRID_W
    kind = lambda st: jnp.where(st == 0, 0, jnp.where(st == nsteps - 1, 2, 1))
    seq_spec = pl.BlockSpec((l, NA_WIDTH), lambda bi, st: (row0 // l + bi, 0))
    return pl.pallas_call(
        functools.partial(_na_kernel, rows=rows),
        out_shape=jax.ShapeDtypeStruct((b * l, NA_WIDTH), BF16),
        grid=(b, nsteps),
        in_specs=[pl.BlockSpec((tq, NA_WIDTH), lambda bi, st: (row0 // tq + bi * nsteps + st, 0)),
                  seq_spec, seq_spec,
                  pl.BlockSpec((None, NA_R, NA_HEADS, GRID_W, NA_KR * GRID_W),
                               lambda bi, st: (kind(st), 0, 0, 0, 0))],
        out_specs=pl.BlockSpec((tq, NA_WIDTH), lambda bi, st: (bi * nsteps + st, 0)),
        compiler_params=_params(("parallel", "arbitrary")),
    )(qn, kn, vn, bias)


def _na(qn, kn, vn, rpb, groups):
    bias = _na_bias_table(rpb)
    return jnp.concatenate([_na_group(qn, kn, vn, bias, row0, b, l) for row0, b, l in groups], axis=0)


def _ssd_prep_kernel(flags_ref, xc_ref, xp_ref, xn_ref, dt_ref, w_ref, b_ref, dtb_ref,
                     xs_ref, bc_ref, dtf_ref, *, tr):
    fl = flags_ref[pl.program_id(0)]
    halo = 8
    xp = jnp.where((fl & 1) > 0, 0.0, xp_ref[...])
    xn = jnp.where((fl & 2) > 0, 0.0, xn_ref[...])
    ext = jnp.concatenate([xp, xc_ref[...], xn], axis=0)
    n = tr + 2 * halo
    acc = None
    for k in range(SSD_CONV):
        sh = (SSD_CONV // 2 - k) % n
        xk = ext if sh == 0 else pltpu.roll(ext, sh, 0)
        term = xk[halo:halo + tr] * w_ref[k:k + 1, :]
        acc = term if acc is None else acc + term
    acc = acc + b_ref[...]
    y = acc * jax.nn.sigmoid(acc)
    xs_ref[...] = y[:, :SSD_INNER]
    bc_ref[...] = y[:, SSD_INNER:]
    t = dt_ref[...] + dtb_ref[...]
    dtf_ref[...] = jnp.maximum(t, 0.0) + jnp.log1p(jnp.exp(-jnp.abs(t)))


def _dt_pad(v):
    z = jnp.zeros((LANES - SSD_HEADS,), v.dtype)
    return jnp.concatenate([v[0], z, v[1], z])


def _ssd_prep(pm, dt_raw, segs, lw):
    n_tok = pm.shape[0]
    tr = 256
    _, flags = _tile_tables(segs, tr)
    nb8 = tr // 8
    last8 = n_tok // 8 - 1
    ccb = COL_XBC // SSD_CONV_DIM
    w8 = jnp.concatenate([lw['ssd_conv_w'], jnp.zeros((8 - SSD_CONV, SSD_CONV_DIM), F32)], axis=0)
    dtb = _dt_pad(lw['ssd_dt_bias'])[None, :]
    grid_spec = pltpu.PrefetchScalarGridSpec(
        num_scalar_prefetch=1, grid=(n_tok // tr,),
        in_specs=[pl.BlockSpec((tr, SSD_CONV_DIM), lambda i, f: (i, ccb)),
                  pl.BlockSpec((8, SSD_CONV_DIM), lambda i, f: (jnp.maximum(i * nb8 - 1, 0), ccb)),
                  pl.BlockSpec((8, SSD_CONV_DIM), lambda i, f: (jnp.minimum((i + 1) * nb8, last8), ccb)),
                  pl.BlockSpec((tr, 2 * LANES), lambda i, f: (i, 0)),
                  pl.BlockSpec((8, SSD_CONV_DIM), lambda i, f: (0, 0)),
                  pl.BlockSpec((1, SSD_CONV_DIM), lambda i, f: (0, 0)),
                  pl.BlockSpec((1, 2 * LANES), lambda i, f: (0, 0))],
        out_specs=[pl.BlockSpec((tr, SSD_INNER), lambda i, f: (i, 0)),
                   pl.BlockSpec((tr, 2 * SSD_GROUPS * SSD_STATE), lambda i, f: (i, 0)),
                   pl.BlockSpec((tr, 2 * LANES), lambda i, f: (i, 0))])
    return pl.pallas_call(
        functools.partial(_ssd_prep_kernel, tr=tr), grid_spec=grid_spec,
        out_shape=[jax.ShapeDtypeStruct((n_tok, SSD_INNER), F32),
                   jax.ShapeDtypeStruct((n_tok, 2 * SSD_GROUPS * SSD_STATE), F32),
                   jax.ShapeDtypeStruct((n_tok, 2 * LANES), F32)],
        compiler_params=_params(("parallel",)),
    )(jnp.asarray(flags), pm, pm, pm, dt_raw, w8, lw['ssd_conv_b'][None, :], dtb)


def _split3(x):
    hi = x.astype(BF16)
    r = x - hi.astype(F32)
    mid = r.astype(BF16)
    lo = (r - mid.astype(F32)).astype(BF16)
    return hi, mid, lo


def _dot_sel_l(sel, x):
    hi, mid, lo = _split3(x)
    d = lambda t: jnp.dot(sel, t, preferred_element_type=F32)
    return d(lo) + d(mid) + d(hi)


def _dot_sel_r(x, sel):
    hi, mid, lo = _split3(x)
    d = lambda t: jnp.dot(t, sel, preferred_element_type=F32)
    return d(lo) + d(mid) + d(hi)


def _ssd_scan_kernel(order_ref, reset_ref, xs_ref, bc_ref, dt_ref, alog_ref, e_ref, *rest, bwd):
    if bwd:
        yf_ref, z_ref, dsk_ref, g_ref, o_ref, state_ref = rest
    else:
        o_ref, state_ref = rest
    del order_ref
    q = SSD_CHUNK
    n_pairs = SSD_HEADS // 2
    gs = SSD_GROUPS * SSD_STATE

    @pl.when(reset_ref[pl.program_id(0)] > 0)
    def _():
        state_ref[...] = jnp.zeros_like(state_ref)

    dt = dt_ref[...]
    da = dt * (-jnp.exp(alog_ref[...]))
    r_i = lax.broadcasted_iota(jnp.int32, (q, q), 0)
    c_i = lax.broadcasted_iota(jnp.int32, (q, q), 1)
    incl = (r_i <= c_i) if bwd else (r_i >= c_i)
    tri = jnp.where(incl, 1.0, 0.0).astype(BF16)
    cs = _dot_sel_l(tri, da)
    cs_t = cs.T
    e = e_ref[...]
    dt_x = _dot_sel_r(dt, e)
    cs_x = _dot_sel_r(cs, e)
    total = cs_x[0:1, :] if bwd else cs_x[q - 1:q, :]
    xs = xs_ref[...]
    xdt = xs * dt_x
    w_state = (xdt * jnp.exp(total - cs_x)).astype(BF16)
    xdt_b = xdt.astype(BF16)
    ecs_x = jnp.exp(cs_x)
    etot = jnp.exp(total)
    bc = bc_ref[...]
    b_t = bc[:, :gs].T
    lane = lax.broadcasted_iota(jnp.int32, (q, LANES), 1)
    ys = []
    for g in range(SSD_GROUPS):
        b_g = bc[:, g * SSD_STATE:(g + 1) * SSD_STATE].astype(BF16)
        c_g = bc[:, gs + g * SSD_STATE:gs + (g + 1) * SSD_STATE].astype(BF16)
        bt_g = b_t[g * SSD_STATE:(g + 1) * SSD_STATE, :].astype(BF16)
        cb = lax.dot_general(c_g, b_g, (((1,), (1,)), ((), ())), preferred_element_type=F32)
        for j in range(n_pairs // SSD_GROUPS):
            pair = g * (n_pairs // SSD_GROUPS) + j
            lanes = slice(pair * LANES, (pair + 1) * LANES)
            halves = []
            for hh in range(2):
                h = 2 * pair + hh
                diff = jnp.broadcast_to(cs[:, h:h + 1], (q, q)) - jnp.broadcast_to(cs_t[h:h + 1, :], (q, q))
                decay = jnp.where(incl, jnp.exp(diff), 0.0)
                halves.append(jnp.dot((cb * decay).astype(BF16), xdt_b[:, lanes], preferred_element_type=F32))
            y_diag = jnp.where(lane < SSD_HEAD_DIM, halves[0], halves[1])
            s_prev = state_ref[pair]
            y_off = jnp.dot(c_g, s_prev.astype(BF16), preferred_element_type=F32) * ecs_x[:, lanes]
            contrib = jnp.dot(bt_g, w_state[:, lanes], preferred_element_type=F32)
            state_ref[pair] = s_prev * etot[:, lanes] + contrib
            ys.append(y_diag + y_off)
    y = jnp.concatenate(ys, axis=1)
    if not bwd:
        o_ref[...] = y
        return
    y = yf_ref[...] + y + dsk_ref[...] * xs
    z = z_ref[...]
    y = y * (z * jax.nn.sigmoid(z))
    y = y * lax.rsqrt(jnp.mean(y * y, axis=-1, keepdims=True) + EPS)
    o_ref[...] = (y * g_ref[...]).astype(o_ref.dtype)


def _ssd_scan(xs, bc, dtf, alog, e_mat, segs, bwd, extra=()):
    n_tok = xs.shape[0]
    q = SSD_CHUNK
    _, flags = _tile_tables(segs, q)
    n = n_tok // q
    order = np.arange(n, dtype=np.int32)[::-1].copy() if bwd else np.arange(n, dtype=np.int32)
    reset = ((flags[order] & (2 if bwd else 1)) > 0).astype(np.int32)
    d = 1 if bwd else 0
    row = lambda w, cb=0: pl.BlockSpec((q, w), lambda i, o, r: (o[i], cb))
    const = lambda shape: pl.BlockSpec(shape, lambda i, o, r: (0,) * len(shape))
    in_specs = [row(SSD_INNER), row(2 * SSD_GROUPS * SSD_STATE), row(LANES, d),
                pl.BlockSpec((None, 1, LANES), lambda i, o, r: (d, 0, 0)), const((LANES, SSD_INNER))]
    if bwd:
        in_specs += [row(SSD_INNER), row(SSD_INNER, COL_Z // SSD_INNER), const((1, SSD_INNER)),
                     const((1, SSD_INNER))]
    grid_spec = pltpu.PrefetchScalarGridSpec(
        num_scalar_prefetch=2, grid=(n,), in_specs=in_specs, out_specs=row(SSD_INNER),
        scratch_shapes=[pltpu.VMEM((SSD_HEADS // 2, SSD_STATE, LANES), F32)])
    return pl.pallas_call(
        functools.partial(_ssd_scan_kernel, bwd=bwd), grid_spec=grid_spec,
        out_shape=jax.ShapeDtypeStruct((n_tok, SSD_INNER), BF16 if bwd else F32),
        compiler_params=_params(("arbitrary",)),
    )(jnp.asarray(order), jnp.asarray(reset), xs, bc, dtf, alog, e_mat, *extra)


def _ssd(pm, dt_raw, segs, lw):
    xs, bc, dtf = _ssd_prep(pm, dt_raw, segs, lw)
    alog = _dt_pad(lw['ssd_a_log']).reshape(2, 1, LANES)
    e_np = np.zeros((LANES, SSD_INNER), np.float32)
    for h in range(SSD_HEADS):
        e_np[h, h * SSD_HEAD_DIM:(h + 1) * SSD_HEAD_DIM] = 1.0
    e_mat = jnp.asarray(e_np, BF16)
    y_fwd = _ssd_scan(xs, bc, dtf, alog, e_mat, segs, False)
    dsk = jnp.repeat(lw['ssd_d'], SSD_HEAD_DIM)[None, :]
    return _ssd_scan(xs, bc, dtf, alog, e_mat, segs, True,
                     extra=(y_fwd, pm, dsk, lw['ssd_norm_g'][None, :]))


S5_Q = 16
S5_PAIRS = S5_GROUPS // 2
S5_BLK = S5_Q * S5_GROUP


def _s5_weights(lw):
    hp = lax.Precision.HIGHEST
    qn, g, p, c = S5_Q, S5_GROUPS, S5_STATE, S5_GROUP
    lam = lax.complex(lw['s5_a_re'], lw['s5_a_im'])
    ls = lam * jnp.exp(lw['s5_log_step'])[..., None]
    lam_bar = jnp.exp(ls)
    b_bar = ((lam_bar - 1.0) / lam)[..., None] * lax.complex(lw['s5_b_re'], lw['s5_b_im'])[None]
    c_c = lax.complex(lw['s5_c_re'], lw['s5_c_im'])
    m = jnp.arange(qn + 1, dtype=F32)
    pw = jnp.exp(ls[:, None] * m[None, :, None, None])

    def lag_kernels(d):
        t1 = c_c[d][None] * pw[d, :qn][:, :, None, :]
        return jnp.real(jnp.einsum('mgip,gpj->mgij', t1, b_bar[d], precision=hp))

    s_idx = np.arange(qn)[:, None]
    t_idx = np.arange(qn)[None, :]
    df = t_idx - s_idx
    kf = lag_kernels(0)[np.clip(df, 0, qn - 1)] * jnp.asarray(df >= 0, F32)[:, :, None, None, None]
    kb = lag_kernels(1)[np.clip(-df, 0, qn - 1)] * jnp.asarray(df <= 0, F32)[:, :, None, None, None]
    tt = jnp.transpose(kf + kb, (2, 0, 4, 1, 3)).reshape(g, qn * c, qn * c)
    tt = tt.reshape(S5_PAIRS, 2, S5_BLK, S5_BLK)

    wf = pw[0, qn - 1 - np.arange(qn)][:, :, :, None] * b_bar[0][None]
    wb = pw[1, np.arange(qn)][:, :, :, None] * b_bar[1][None]
    to_rows = lambda w: jnp.transpose(w, (1, 0, 3, 2)).reshape(g, qn * c, p)
    wf, wb = to_rows(wf), to_rows(wb)
    w4 = jnp.stack([jnp.real(wf), jnp.imag(wf), jnp.real(wb), jnp.imag(wb)], axis=2)
    w4 = w4.reshape(S5_PAIRS, 2, S5_BLK, 4, p)
    eye2 = jnp.eye(2, dtype=F32)
    wa = jnp.einsum('jgrkp,gh->jgrkhp', w4, eye2).reshape(S5_PAIRS, 2 * S5_BLK, 4 * 2 * p)

    def out_op(d, powers):
        cp = jnp.transpose(c_c[d], (0, 2, 1))[:, :, None, :]
        return (cp * jnp.transpose(pw[d, powers], (1, 2, 0))[:, :, :, None]).reshape(g, p, qn * c)

    vf = out_op(0, 1 + np.arange(qn))
    vb = out_op(1, qn - np.arange(qn))
    v4 = jnp.stack([jnp.real(vf), -jnp.imag(vf), jnp.real(vb), -jnp.imag(vb)], axis=0)
    v4 = v4.reshape(4, S5_PAIRS, 2, p, S5_BLK)
    vc = jnp.einsum('kjgpc,gh->jkgphc', v4, eye2).reshape(S5_PAIRS, 4 * 2 * p, 2 * S5_BLK)

    lam_q = pw[:, qn].reshape(2, S5_PAIRS, 2 * p)
    dsk = jnp.broadcast_to(lw['s5_d'].reshape(g, 1, c), (g, qn, c)).reshape(S5_PAIRS, 1, 2 * S5_BLK)
    return dict(tt=tt.astype(BF16), wa=wa.astype(BF16), vc=vc.astype(BF16),
                lam_re=jnp.real(lam_q), lam_im=jnp.imag(lam_q), dsk=dsk)


def _s5_in_kernel(u_ref, w_ref, fre_ref, fim_ref, bre_ref, bim_ref):
    r = jnp.dot(u_ref[...].astype(BF16), w_ref[...], preferred_element_type=F32)
    fre_ref[...] = r[:, 0 * LANES:1 * LANES]
    fim_ref[...] = r[:, 1 * LANES:2 * LANES]
    bre_ref[...] = r[:, 2 * LANES:3 * LANES]
    bim_ref[...] = r[:, 3 * LANES:4 * LANES]


def _s5_in(u2, wa):
    m = u2.shape[0]
    st = pl.BlockSpec((None, m, LANES), lambda j: (j, 0, 0))
    sds = jax.ShapeDtypeStruct((S5_PAIRS, m, LANES), F32)
    return pl.pallas_call(
        _s5_in_kernel, out_shape=[sds] * 4, grid=(S5_PAIRS,),
        in_specs=[pl.BlockSpec((m, 2 * S5_BLK), lambda j: (0, j)),
                  pl.BlockSpec((None, 2 * S5_BLK, 4 * LANES), lambda j: (j, 0, 0))],
        out_specs=[st] * 4,
        compiler_params=_params(("parallel",)),
    )(u2, wa)


def _s5_rec_kernel(order_ref, reset_ref, fre_ref, fim_ref, lr_ref, li_ref, xre_ref, xim_ref, sre_ref, sim_ref,
                   *, tc, bwd):
    del order_ref

    @pl.when(reset_ref[pl.program_id(0)] > 0)
    def _():
        sre_ref[...] = jnp.zeros_like(sre_ref)
        sim_ref[...] = jnp.zeros_like(sim_ref)

    lr = lr_ref[...]
    li = li_ref[...]

    def body(k, carry):
        xr, xi = carry
        c = (tc - 1 - k) if bwd else k
        xre_ref[c] = xr
        xim_ref[c] = xi
        nr = lr * xr - li * xi + fre_ref[c]
        ni = lr * xi + li * xr + fim_ref[c]
        return nr, ni

    xr, xi = lax.fori_loop(0, tc, body, (sre_ref[...], sim_ref[...]))
    sre_ref[...] = xr
    sim_ref[...] = xi


def _s5_rec(fre, fim, lam_re, lam_im, segs, bwd):
    m = fre.shape[0]
    tc = min(128, min(l for _, l in segs) // S5_Q)
    _, flags = _tile_tables(segs, tc * S5_Q)
    n = m // tc
    order = np.arange(n, dtype=np.int32)[::-1].copy() if bwd else np.arange(n, dtype=np.int32)
    reset = ((flags[order] & (2 if bwd else 1)) > 0).astype(np.int32)
    blk = pl.BlockSpec((tc, S5_PAIRS, LANES), lambda i, o, r: (o[i], 0, 0))
    lam = pl.BlockSpec((S5_PAIRS, LANES), lambda i, o, r: (0, 0))
    grid_spec = pltpu.PrefetchScalarGridSpec(
        num_scalar_prefetch=2, grid=(n,), in_specs=[blk, blk, lam, lam], out_specs=[blk, blk],
        scratch_shapes=[pltpu.VMEM((S5_PAIRS, LANES), F32), pltpu.VMEM((S5_PAIRS, LANES), F32)])
    sds = jax.ShapeDtypeStruct((m, S5_PAIRS, LANES), F32)
    return pl.pallas_call(
        functools.partial(_s5_rec_kernel, tc=tc, bwd=bwd), grid_spec=grid_spec, out_shape=[sds, sds],
        compiler_params=_params(("arbitrary",)),
    )(jnp.asarray(order), jnp.asarray(reset), fre, fim, lam_re, lam_im)


def _s5_out_kernel(u_ref, tt_ref, xre_ref, xim_ref, zre_ref, zim_ref, vc_ref, dsk_ref, y_ref):
    u = u_ref[...]
    ub = u.astype(BF16)
    intra = jnp.concatenate(
        [jnp.dot(ub[:, :S5_BLK], tt_ref[0], preferred_element_type=F32),
         jnp.dot(ub[:, S5_BLK:], tt_ref[1], preferred_element_type=F32)], axis=1)
    st = jnp.concatenate([xre_ref[...], xim_ref[...], zre_ref[...], zim_ref[...]], axis=1).astype(BF16)
    carry = jnp.dot(st, vc_ref[...], preferred_element_type=F32)
    y_ref[...] = intra + carry + dsk_ref[...] * u


def _s5_out(u2, w, xre, xim, zre, zim):
    m = u2.shape[0]
    st = pl.BlockSpec((None, m, LANES), lambda j: (j, 0, 0))
    return pl.pallas_call(
        _s5_out_kernel, out_shape=jax.ShapeDtypeStruct(u2.shape, F32), grid=(S5_PAIRS,),
        in_specs=[pl.BlockSpec((m, 2 * S5_BLK), lambda j: (0, j)),
                  pl.BlockSpec((None, 2, S5_BLK, S5_BLK), lambda j: (j, 0, 0, 0)),
                  st, st, st, st,
                  pl.BlockSpec((None, 4 * LANES, 2 * S5_BLK), lambda j: (j, 0, 0)),
                  pl.BlockSpec((None, 1, 2 * S5_BLK), lambda j: (j, 0, 0))],
        out_specs=pl.BlockSpec((m, 2 * S5_BLK), lambda j: (0, j)),
        compiler_params=_params(("parallel",)),
    )(u2, w['tt'], xre, xim, zre, zim, w['vc'], w['dsk'])


def _s5_glu_kernel(y_ref, w_ref, b_ref, o_ref):
    g = jax.nn.gelu(y_ref[...])
    t = jnp.dot(g.astype(BF16), w_ref[...], preferred_element_type=F32) + b_ref[...]
    o_ref[...] = (g * jax.nn.sigmoid(t)).astype(o_ref.dtype)


def _s5_glu(y, glu_w, glu_b):
    n_tok = y.shape[0]
    tr = 512
    return pl.pallas_call(
        _s5_glu_kernel, out_shape=jax.ShapeDtypeStruct((n_tok, S5_WIDTH), BF16), grid=(n_tok // tr,),
        in_specs=[pl.BlockSpec((tr, S5_WIDTH), lambda i: (i, 0)),
                  pl.BlockSpec((S5_WIDTH, S5_WIDTH), lambda i: (0, 0)),
                  pl.BlockSpec((1, S5_WIDTH), lambda i: (0, 0))],
        out_specs=pl.BlockSpec((tr, S5_WIDTH), lambda i: (i, 0)),
        compiler_params=_params(("parallel",)),
    )(y, glu_w.astype(BF16), glu_b[None, :])


def _s5(u, segs, lw):
    n_tok = u.shape[0]
    m = n_tok // S5_Q
    w = _s5_weights(lw)
    u2 = jnp.transpose(u.reshape(m, S5_Q, S5_GROUPS, S5_GROUP), (0, 2, 1, 3)).reshape(m, S5_GROUPS * S5_BLK)
    contrib = _s5_in(u2, w['wa'])
    fre, fim, bre, bim = (jnp.transpose(t, (1, 0, 2)) for t in contrib)
    xre, xim = _s5_rec(fre, fim, w['lam_re'][0], w['lam_im'][0], segs, False)
    zre, zim = _s5_rec(bre, bim, w['lam_re'][1], w['lam_im'][1], segs, True)
    xre, xim, zre, zim = (jnp.transpose(t, (1, 0, 2)) for t in (xre, xim, zre, zim))
    y2 = _s5_out(u2, w, xre, xim, zre, zim)
    y = jnp.transpose(y2.reshape(m, S5_GROUPS, S5_Q, S5_GROUP), (0, 2, 1, 3)).reshape(n_tok, S5_WIDTH)
    return _s5_glu(y, lw['s5_glu_w'], lw['s5_glu_b'])


def _layer(x, c8, lw):
    w_in = lw['w_in']
    o = IN_OFFS
    piece = lambda k: w_in[:, o[k]:o[k + 1]]
    w_mix = jnp.concatenate([piece(0), piece(3), piece(6), piece(7), piece(8), piece(9),
                             piece(1), piece(4), piece(5)], axis=1).astype(BF16)
    zpad = jnp.zeros((D_MODEL, LANES - SSD_HEADS), F32)
    w_dt = jnp.concatenate([piece(2)[:, :SSD_HEADS], zpad, piece(2)[:, SSD_HEADS:], zpad], axis=1).astype(BF16)
    w_gate = jnp.transpose(piece(10).reshape(D_MODEL, N_BRANCH, D_MODEL), (1, 0, 2)).astype(BF16)
    w_branch = jnp.stack([lw['w_branch_ssd'], lw['w_branch_swa'], lw['w_branch_s5'],
                          lw['w_branch_na']]).astype(BF16)
    w_out = lw['w_out'].astype(BF16)
    ff_pad = ((0, 0), (0, D_FF_PAD - D_FF))
    w1 = jnp.pad(lw['ffn_w1'], ff_pad).astype(BF16)
    w3 = jnp.pad(lw['ffn_w3'], ff_pad).astype(BF16)
    w2 = jnp.pad(lw['ffn_w2'], ((0, D_FF_PAD - D_FF), (0, 0))).astype(BF16)

    mod = _ada(c8, lw['ada_w'], lw['ada_b'])
    mod4 = mod.reshape(8, 6, 1, D_MODEL)

    h = _norm_mod(x, lw['norm1_g'], mod4, 1, 0)
    pm = _mm(h, w_mix, 1024, F32)
    dt_raw = _mm(h, w_dt, 2 * LANES, F32)

    y_ssd = _ssd(pm, dt_raw, SEGS, lw)
    qs, ks, qn, kn, vn = _qk_prep(pm, SEGS, lw)
    y_swa = _swa(qs, ks, pm, lw['swa_sink'], SEGS)
    y_s5 = _s5(pm[:, COL_U_S5:COL_U_S5 + S5_WIDTH], SEGS, lw)
    y_na = _na(qn, kn, vn, lw['na_rpb'], GROUPS)
    mix = jnp.stack([y_ssd, y_swa, y_s5, y_na])

    merged = _merge(h, w_gate, mix, w_branch)
    x = _mm_res(merged, w_out, x, mod4, 2, 1024, D_MODEL)

    h2 = _norm_mod(x, lw['norm2_g'], mod4, 4, 3)
    u = _ffn_up(h2, w1, w3)
    return _mm_res(u, w2, x, mod4, 5, 1024, D_FF_PAD // 4)


_LAYER_KEYS = ('ada_w', 'ada_b', 'norm1_g', 'norm2_g', 'w_in',
               'ssd_conv_w', 'ssd_conv_b', 'ssd_dt_bias', 'ssd_a_log', 'ssd_d', 'ssd_norm_g',
               'swa_q_norm_g', 'swa_k_norm_g', 'swa_sink',
               's5_a_re', 's5_a_im', 's5_log_step', 's5_b_re', 's5_b_im', 's5_c_re', 's5_c_im',
               's5_d', 's5_glu_w', 's5_glu_b',
               'na_q_norm_g', 'na_k_norm_g', 'na_rpb',
               'w_branch_ssd', 'w_branch_swa', 'w_branch_s5', 'w_branch_na', 'w_out',
               'ffn_w1', 'ffn_w3', 'ffn_w2')


def kernel(x_prompt, x_sample, c_prompt, c_sample, ada_w, ada_b, norm1_g, norm2_g, w_in, ssd_conv_w, ssd_conv_b, ssd_dt_bias, ssd_a_log, ssd_d, ssd_norm_g, swa_q_norm_g, swa_k_norm_g, swa_sink, s5_a_re, s5_a_im, s5_log_step, s5_b_re, s5_b_im, s5_c_re, s5_c_im, s5_d, s5_glu_w, s5_glu_b, na_q_norm_g, na_k_norm_g, na_rpb, w_branch_ssd, w_branch_swa, w_branch_s5, w_branch_na, w_out, ffn_w1, ffn_w3, ffn_w2):
    stacked = dict(zip(_LAYER_KEYS, (ada_w, ada_b, norm1_g, norm2_g, w_in,
                                     ssd_conv_w, ssd_conv_b, ssd_dt_bias, ssd_a_log, ssd_d, ssd_norm_g,
                                     swa_q_norm_g, swa_k_norm_g, swa_sink,
                                     s5_a_re, s5_a_im, s5_log_step, s5_b_re, s5_b_im, s5_c_re, s5_c_im,
                                     s5_d, s5_glu_w, s5_glu_b,
                                     na_q_norm_g, na_k_norm_g, na_rpb,
                                     w_branch_ssd, w_branch_swa, w_branch_s5, w_branch_na, w_out,
                                     ffn_w1, ffn_w3, ffn_w2)))
    x = jnp.concatenate([x_prompt.reshape(BATCH * SEQ, D_MODEL),
                         x_sample.reshape(DEC_BATCH * DEC_SEQ, D_MODEL)], axis=0)
    c8 = jnp.concatenate([c_prompt, c_sample, jnp.zeros((8 - N_SEQS, D_MODEL), F32)], axis=0)
    for i in range(DEPTH):
        x = _layer(x, c8, {k: v[i] for k, v in stacked.items()})
    n_p = BATCH * SEQ
    return (x[:n_p].reshape(BATCH, SEQ, D_MODEL), x[n_p:].reshape(DEC_BATCH, DEC_SEQ, D_MODEL))
```

```python
import functools

import jax
import jax.numpy as jnp
import numpy as np
from jax import lax
from jax.experimental import pallas as pl
from jax.experimental.pallas import tpu as pltpu

D_MODEL = 4096
BATCH = 2
SEQ = 4096
DEPTH = 2
DEC_BATCH = 4
DEC_SEQ = 2048
N_TOK = BATCH * SEQ + DEC_BATCH * DEC_SEQ
N_SEQS = BATCH + DEC_BATCH
GROUPS = ((0, BATCH, SEQ), (BATCH * SEQ, DEC_BATCH, DEC_SEQ))
SEGS = tuple((row0 + i * l, l) for row0, b, l in GROUPS for i in range(b))

EPS = 1e-6
HEAD_DIM = 128
N_BRANCH = 4
SSD_HEADS = 16
SSD_HEAD_DIM = 64
SSD_INNER = SSD_HEADS * SSD_HEAD_DIM
SSD_GROUPS = 2
SSD_STATE = 128
SSD_CONV = 5
SSD_CHUNK = 128
SSD_CONV_DIM = SSD_INNER + 2 * SSD_GROUPS * SSD_STATE
SWA_HEADS = 8
SWA_KV_HEADS = 2
SWA_WIDTH = SWA_HEADS * HEAD_DIM
SWA_KV_WIDTH = SWA_KV_HEADS * HEAD_DIM
SWA_WINDOW = 128
SWA_BLOCK = 128
ROPE_THETA = 10000.0
S5_WIDTH = 1024
S5_GROUP = 16
S5_GROUPS = S5_WIDTH // S5_GROUP
S5_STATE = 64
NA_HEADS = 8
NA_WIDTH = NA_HEADS * HEAD_DIM
GRID_W = 64
NA_KR = 8
NA_KW = 16
D_FF = ((8 * D_MODEL + 3 * 256 - 1) // (3 * 256)) * 256
IN_SIZES = (SSD_INNER, SSD_CONV_DIM, 2 * SSD_HEADS,
            SWA_WIDTH, SWA_KV_WIDTH, SWA_KV_WIDTH,
            S5_WIDTH,
            NA_WIDTH, NA_WIDTH, NA_WIDTH,
            N_BRANCH * D_MODEL)
IN_OFFS = tuple(int(v) for v in np.cumsum((0,) + IN_SIZES))

LANES = 128
TM = 1024
N_MIX = IN_OFFS[10] - IN_SIZES[2]
VMEM_LIMIT = 56 * 1024 * 1024

COL_Z, COL_Q_SWA, COL_U_S5, COL_Q_NA, COL_K_NA, COL_V_NA = 0, 1024, 2048, 3072, 4096, 5120
COL_XBC = 6144
COL_K_SWA = COL_XBC + SSD_CONV_DIM
COL_V_SWA = COL_K_SWA + SWA_KV_WIDTH

F32 = jnp.float32
BF16 = jnp.bfloat16


def _batch_of_tile(i, tile):
    n_p = (BATCH * SEQ) // tile
    return jnp.where(i < n_p, i // (SEQ // tile), BATCH + (i - n_p) // (DEC_SEQ // tile))


def _params(sem):
    return pltpu.CompilerParams(dimension_semantics=sem, vmem_limit_bytes=VMEM_LIMIT)


def _ada_kernel(c_ref, w_ref, b_ref, o_ref):
    c = c_ref[...]
    a = (c * jax.nn.sigmoid(c)).astype(BF16)
    o_ref[...] = jnp.dot(a, w_ref[...].astype(BF16), preferred_element_type=F32) + b_ref[...]


def _ada(c8, ada_w, ada_b, layer):
    n = ada_w.shape[2]
    tn = 512
    return pl.pallas_call(
        _ada_kernel,
        out_shape=jax.ShapeDtypeStruct((8, n), F32),
        grid=(n // tn,),
        in_specs=[pl.BlockSpec((8, D_MODEL), lambda j: (0, 0)),
                  pl.BlockSpec((None, D_MODEL, tn), lambda j: (layer, 0, j)),
                  pl.BlockSpec((1, tn), lambda j: (0, j))],
        out_specs=pl.BlockSpec((8, tn), lambda j: (0, j)),
        compiler_params=_params(("parallel",)),
    )(c8, ada_w, ada_b[layer].reshape(1, n))


def _norm_mod_kernel(x_ref, g_ref, scale_ref, shift_ref, o_ref):
    x = x_ref[...]
    y = x * lax.rsqrt(jnp.mean(x * x, axis=-1, keepdims=True) + EPS)
    y = y * g_ref[...]
    o_ref[...] = (y * (1.0 + scale_ref[...]) + shift_ref[...]).astype(o_ref.dtype)


def _norm_mod(x, g, mod4, scale_idx, shift_idx):
    tr = 256
    return pl.pallas_call(
        _norm_mod_kernel,
        out_shape=jax.ShapeDtypeStruct((N_TOK, D_MODEL), BF16),
        grid=(N_TOK // tr,),
        in_specs=[pl.BlockSpec((tr, D_MODEL), lambda i: (i, 0)),
                  pl.BlockSpec((1, D_MODEL), lambda i: (0, 0)),
                  pl.BlockSpec((None, None, 1, D_MODEL), lambda i: (_batch_of_tile(i, tr), scale_idx, 0, 0)),
                  pl.BlockSpec((None, None, 1, D_MODEL), lambda i: (_batch_of_tile(i, tr), shift_idx, 0, 0))],
        out_specs=pl.BlockSpec((tr, D_MODEL), lambda i: (i, 0)),
        compiler_params=_params(("parallel",)),
    )(x, g.reshape(1, D_MODEL), mod4, mod4)


def _mm_kernel(a_ref, b_ref, o_ref):
    o_ref[...] = jnp.dot(a_ref[...], b_ref[...], preferred_element_type=F32).astype(o_ref.dtype)


def _mm(a, b, tn, out_dtype):
    m, k = a.shape
    n = b.shape[1]
    return pl.pallas_call(
        _mm_kernel,
        out_shape=jax.ShapeDtypeStruct((m, n), out_dtype),
        grid=(m // TM, n // tn),
        in_specs=[pl.BlockSpec((TM, k), lambda i, j: (i, 0)),
                  pl.BlockSpec((k, tn), lambda i, j: (0, j))],
        out_specs=pl.BlockSpec((TM, tn), lambda i, j: (i, j)),
        compiler_params=_params(("parallel", "parallel")),
    )(a, b)


def _merge_kernel(h_ref, wg_ref, m0_ref, m1_ref, m2_ref, m3_ref, wb_ref, o_ref, acc_ref):
    b = pl.program_id(2)
    gate = jax.nn.sigmoid(jnp.dot(h_ref[...], wg_ref[...], preferred_element_type=F32))
    for bb, m_ref in enumerate((m0_ref, m1_ref, m2_ref, m3_ref)):
        @pl.when(b == bb)
        def _(bb=bb, m_ref=m_ref):
            contrib = gate * jnp.dot(m_ref[...], wb_ref[...], preferred_element_type=F32)
            if bb == 0:
                acc_ref[...] = contrib
            elif bb < N_BRANCH - 1:
                acc_ref[...] += contrib
            else:
                o_ref[...] = (acc_ref[...] + contrib).astype(o_ref.dtype)


def _merge(h, wg, mixes, wb):
    tn = 512
    w = mixes[0].shape[1]
    mix_spec = pl.BlockSpec((TM, w), lambda i, j, b: (i, 0))
    return pl.pallas_call(
        _merge_kernel,
        out_shape=jax.ShapeDtypeStruct((N_TOK, D_MODEL), BF16),
        grid=(N_TOK // TM, D_MODEL // tn, N_BRANCH),
        in_specs=[pl.BlockSpec((TM, D_MODEL), lambda i, j, b: (i, 0)),
                  pl.BlockSpec((None, D_MODEL, tn), lambda i, j, b: (b, 0, j)),
                  mix_spec, mix_spec, mix_spec, mix_spec,
                  pl.BlockSpec((None, w, tn), lambda i, j, b: (b, 0, j))],
        out_specs=pl.BlockSpec((TM, tn), lambda i, j, b: (i, j)),
        scratch_shapes=[pltpu.VMEM((TM, tn), F32)],
        compiler_params=_params(("parallel", "parallel", "arbitrary")),
    )(h, wg, *mixes, wb)


def _mm_res_kernel(a_ref, b_ref, x_ref, gate_ref, o_ref, acc_ref, *, nk):
    d = jnp.dot(a_ref[...], b_ref[...], preferred_element_type=F32)
    if nk == 1:
        o_ref[...] = x_ref[...] + gate_ref[...] * d
        return
    k = pl.program_id(2)

    @pl.when(k == 0)
    def _():
        acc_ref[...] = d

    @pl.when(k > 0)
    def _():
        acc_ref[...] += d

    @pl.when(k == nk - 1)
    def _():
        o_ref[...] = x_ref[...] + gate_ref[...] * acc_ref[...]


def _mm_res(a, b, x, mod4, gate_idx, tn, tk):
    m, kdim = a.shape
    n = b.shape[1]
    nk = kdim // tk
    return pl.pallas_call(
        functools.partial(_mm_res_kernel, nk=nk),
        out_shape=jax.ShapeDtypeStruct((m, n), F32),
        grid=(m // TM, n // tn, nk),
        in_specs=[pl.BlockSpec((TM, tk), lambda i, j, k: (i, k)),
                  pl.BlockSpec((tk, tn), lambda i, j, k: (k, j)),
                  pl.BlockSpec((TM, tn), lambda i, j, k: (i, j)),
                  pl.BlockSpec((None, None, 1, tn), lambda i, j, k: (_batch_of_tile(i, TM), gate_idx, 0, j))],
        out_specs=pl.BlockSpec((TM, tn), lambda i, j, k: (i, j)),
        scratch_shapes=[pltpu.VMEM((TM, tn), F32)],
        compiler_params=_params(("parallel", "parallel", "arbitrary")),
    )(a, b, x, mod4)


def _ffn_up_kernel(h_ref, w1_ref, w3_ref, o_ref):
    h = h_ref[...]
    a = jnp.dot(h, w1_ref[...], preferred_element_type=F32)
    b = jnp.dot(h, w3_ref[...], preferred_element_type=F32)
    o_ref[...] = (a * jax.nn.sigmoid(a) * b).astype(o_ref.dtype)


def _ffn_up(h, w1, w3):
    tn = 512
    n = w1.shape[1]
    return pl.pallas_call(
        _ffn_up_kernel,
        out_shape=jax.ShapeDtypeStruct((N_TOK, n), BF16),
        grid=(N_TOK // TM, pl.cdiv(n, tn)),
        in_specs=[pl.BlockSpec((TM, D_MODEL), lambda i, j: (i, 0)),
                  pl.BlockSpec((D_MODEL, tn), lambda i, j: (0, j)),
                  pl.BlockSpec((D_MODEL, tn), lambda i, j: (0, j))],
        out_specs=pl.BlockSpec((TM, tn), lambda i, j: (i, j)),
        compiler_params=_params(("parallel", "parallel")),
    )(h, w1, w3)


def _tile_tables(segs, tile):
    pos, flags = [], []
    for _, length in segs:
        n = length // tile
        for j in range(n):
            pos.append(j)
            flags.append((1 if j == 0 else 0) | (2 if j == n - 1 else 0))
    return np.asarray(pos, np.int32), np.asarray(flags, np.int32)


def _head_slice(h):
    return slice(h * HEAD_DIM, (h + 1) * HEAD_DIM)


def _qk_prep_kernel(pos_ref, qs_ref, ks_ref, qn_ref, kn_ref, vn_ref, cos_ref, sin_ref, g_ref,
                    oqs_ref, oks_ref, oqn_ref, okn_ref, ovn_ref):
    del pos_ref
    cos = cos_ref[...]
    sin = sin_ref[...]

    def hnorm(x, g):
        return x * lax.rsqrt(jnp.mean(x * x, axis=-1, keepdims=True) + EPS) * g

    def rope(x):
        return x * cos + pltpu.roll(x, HEAD_DIM // 2, 1) * sin

    for h in range(SWA_HEADS):
        oqs_ref[:, _head_slice(h)] = rope(hnorm(qs_ref[:, _head_slice(h)], g_ref[0:1, :])).astype(BF16)
    for h in range(SWA_KV_HEADS):
        oks_ref[:, _head_slice(h)] = rope(hnorm(ks_ref[:, _head_slice(h)], g_ref[1:2, :])).astype(BF16)
    for h in range(NA_HEADS):
        oqn_ref[:, _head_slice(h)] = hnorm(qn_ref[:, _head_slice(h)], g_ref[2:3, :]).astype(BF16)
        okn_ref[:, _head_slice(h)] = hnorm(kn_ref[:, _head_slice(h)], g_ref[3:4, :]).astype(BF16)
    ovn_ref[...] = vn_ref[...].astype(BF16)


def _rope_tables(max_len):
    half = HEAD_DIM // 2
    inv_freq = ROPE_THETA ** (-jnp.arange(half, dtype=F32) / half)
    ang = jnp.arange(max_len, dtype=F32)[:, None] * inv_freq[None, :]
    cos, sin = jnp.cos(ang), jnp.sin(ang)
    return jnp.concatenate([cos, cos], axis=1), jnp.concatenate([-sin, sin], axis=1)


def _qk_prep(pm, segs, lw):
    n_tok = pm.shape[0]
    tr = 256
    pos, _ = _tile_tables(segs, tr)
    cos, sin = _rope_tables(max(l for _, l in segs))
    gains = jnp.concatenate([lw['swa_q_norm_g'][None], lw['swa_k_norm_g'][None],
                             lw['na_q_norm_g'][None], lw['na_k_norm_g'][None],
                             jnp.zeros((4, HEAD_DIM), F32)], axis=0)
    wide = lambda cb: pl.BlockSpec((tr, 1024), lambda i, p: (i, cb))
    narrow = lambda cb: pl.BlockSpec((tr, 256), lambda i, p: (i, cb))
    grid_spec = pltpu.PrefetchScalarGridSpec(
        num_scalar_prefetch=1, grid=(n_tok // tr,),
        in_specs=[wide(COL_Q_SWA // 1024), narrow(COL_K_SWA // 256), wide(COL_Q_NA // 1024),
                  wide(COL_K_NA // 1024), wide(COL_V_NA // 1024),
                  pl.BlockSpec((tr, HEAD_DIM), lambda i, p: (p[i], 0)),
                  pl.BlockSpec((tr, HEAD_DIM), lambda i, p: (p[i], 0)),
                  pl.BlockSpec((8, HEAD_DIM), lambda i, p: (0, 0))],
        out_specs=[pl.BlockSpec((tr, 1024), lambda i, p: (i, 0)),
                   pl.BlockSpec((tr, 256), lambda i, p: (i, 0)),
                   pl.BlockSpec((tr, 1024), lambda i, p: (i, 0)),
                   pl.BlockSpec((tr, 1024), lambda i, p: (i, 0)),
                   pl.BlockSpec((tr, 1024), lambda i, p: (i, 0))])
    sds = lambda w: jax.ShapeDtypeStruct((n_tok, w), BF16)
    return pl.pallas_call(
        _qk_prep_kernel, grid_spec=grid_spec,
        out_shape=[sds(1024), sds(256), sds(1024), sds(1024), sds(1024)],
        compiler_params=_params(("parallel",)),
    )(jnp.asarray(pos), pm, pm, pm, pm, pm, cos, sin, gains)


SWA_TQ = 512
SWA_GRP = SWA_HEADS // SWA_KV_HEADS


def _swa_kernel(flags_ref, sink_ref, q_ref, kc_ref, kp_ref, kn_ref, vc_ref, vp_ref, vn_ref, o_ref, *, tq):
    fl = flags_ref[pl.program_id(0)]
    lo = jnp.where((fl & 1) > 0, SWA_BLOCK, 0)
    hi = jnp.where((fl & 2) > 0, 2 * SWA_BLOCK, 3 * SWA_BLOCK)
    nqb = tq // SWA_BLOCK
    m_rows = SWA_GRP * SWA_BLOCK
    row = lax.broadcasted_iota(jnp.int32, (m_rows, 3 * SWA_BLOCK), 0) & (SWA_BLOCK - 1)
    col = lax.broadcasted_iota(jnp.int32, (m_rows, 3 * SWA_BLOCK), 1)
    band = (col >= row) & (col <= row + 2 * SWA_WINDOW)
    scale = HEAD_DIM ** -0.5
    for g in range(SWA_KV_HEADS):
        ks = _head_slice(g)
        k_ext = jnp.concatenate([kp_ref[:, ks], kc_ref[:, ks], kn_ref[:, ks]], axis=0)
        v_ext = jnp.concatenate([vp_ref[:, ks], vc_ref[:, ks], vn_ref[:, ks]], axis=0).astype(BF16)
        sk = jnp.concatenate([jnp.full((SWA_BLOCK, 1), sink_ref[g * SWA_GRP + h], F32)
                              for h in range(SWA_GRP)], axis=0)
        for qb in range(nqb):
            rows = slice(qb * SWA_BLOCK, (qb + 1) * SWA_BLOCK)
            q = jnp.concatenate([q_ref[rows, _head_slice(g * SWA_GRP + h)] for h in range(SWA_GRP)], axis=0)
            keys = k_ext[qb * SWA_BLOCK:(qb + 3) * SWA_BLOCK]
            vals = v_ext[qb * SWA_BLOCK:(qb + 3) * SWA_BLOCK]
            s = lax.dot_general(q, keys, (((1,), (1,)), ((), ())), preferred_element_type=F32) * scale
            mask = band
            if qb == 0:
                mask = mask & (col >= lo)
            if qb == nqb - 1:
                mask = mask & (col < hi)
            s = jnp.where(mask, s, -jnp.inf)
            m = jnp.maximum(jnp.max(s, axis=-1, keepdims=True), sk)
            p = jnp.exp(s - m)
            denom = jnp.sum(p, axis=-1, keepdims=True) + jnp.exp(sk - m)
            o = jnp.dot((p / denom).astype(BF16), vals, preferred_element_type=F32)
            for h in range(SWA_GRP):
                o_ref[rows, _head_slice(g * SWA_GRP + h)] = o[h * SWA_BLOCK:(h + 1) * SWA_BLOCK].astype(o_ref.dtype)


def _swa(qs, ks, pm, sink, segs):
    n_tok = qs.shape[0]
    tq = min(SWA_TQ, min(l for _, l in segs))
    _, flags = _tile_tables(segs, tq)
    nb = tq // SWA_BLOCK
    last_blk = n_tok // SWA_BLOCK - 1
    prev_map = lambda cb: (lambda i, f: (jnp.maximum(i * nb - 1, 0), cb))
    next_map = lambda cb: (lambda i, f: (jnp.minimum((i + 1) * nb, last_blk), cb))
    vcb = COL_V_SWA // SWA_KV_WIDTH
    grid_spec = pltpu.PrefetchScalarGridSpec(
        num_scalar_prefetch=1, grid=(n_tok // tq,),
        in_specs=[pl.BlockSpec(memory_space=pltpu.SMEM),
                  pl.BlockSpec((tq, SWA_WIDTH), lambda i, f: (i, 0)),
                  pl.BlockSpec((tq, SWA_KV_WIDTH), lambda i, f: (i, 0)),
                  pl.BlockSpec((SWA_BLOCK, SWA_KV_WIDTH), prev_map(0)),
                  pl.BlockSpec((SWA_BLOCK, SWA_KV_WIDTH), next_map(0)),
                  pl.BlockSpec((tq, SWA_KV_WIDTH), lambda i, f: (i, vcb)),
                  pl.BlockSpec((SWA_BLOCK, SWA_KV_WIDTH), prev_map(vcb)),
                  pl.BlockSpec((SWA_BLOCK, SWA_KV_WIDTH), next_map(vcb))],
        out_specs=pl.BlockSpec((tq, SWA_WIDTH), lambda i, f: (i, 0)))
    return pl.pallas_call(
        functools.partial(_swa_kernel, tq=tq), grid_spec=grid_spec,
        out_shape=jax.ShapeDtypeStruct((n_tok, SWA_WIDTH), BF16),
        compiler_params=_params(("parallel",)),
    )(jnp.asarray(flags), sink, qs, ks, ks, ks, pm, pm, pm)


NA_R = NA_KR // 2
NA_WIN = NA_R + NA_KR


def _na_bias_table(rpb):
    off = np.array([np.zeros(NA_R, int), np.arange(NA_R), np.full(NA_R, NA_R)])
    p_of = np.array([np.arange(NA_R), np.full(NA_R, NA_KR // 2), NA_KR // 2 + np.arange(NA_R)])
    j = np.arange(NA_WIN)[None, None, :] - off[:, :, None]
    row_ok = (j >= 0) & (j < NA_KR)
    dr = np.clip(j - p_of[:, :, None] + (NA_KR - 1), 0, 2 * NA_KR - 2)
    qc = np.arange(GRID_W)
    kc = np.arange(GRID_W)
    dc = np.clip(kc[None, :] - qc[:, None], -(NA_KW - 1), NA_KW - 1) + (NA_KW - 1)
    col_start = np.clip(qc - NA_KW // 2, 0, GRID_W - NA_KW)
    valid = (kc[None, :] >= col_start[:, None]) & (kc[None, :] < col_start[:, None] + NA_KW)
    hp = lax.Precision.HIGHEST
    sel_dc = jnp.asarray(np.eye(2 * NA_KW - 1, dtype=np.float32)[dc.reshape(-1)])
    sel_dr = jnp.asarray(np.eye(2 * NA_KR - 1, dtype=np.float32)[dr.reshape(-1)])
    cols = jnp.einsum('hab,xb->hax', rpb, sel_dc, precision=hp)
    bias = jnp.einsum('ya,hax->yhx', sel_dr, cols, precision=hp)
    bias = bias.reshape(3, NA_R, NA_WIN, NA_HEADS, GRID_W, GRID_W)
    ok = row_ok[:, :, :, None, None, None] & valid[None, None, None, None, :, :]
    bias = jnp.where(ok, bias, -jnp.inf)
    return jnp.transpose(bias, (0, 3, 1, 4, 2, 5)).reshape(3, NA_HEADS, NA_R * GRID_W, NA_WIN * GRID_W)


def _na_kernel(q_ref, k_ref, v_ref, bias_ref, o_ref, *, rows):
    first_row = jnp.clip(pl.program_id(1) * NA_R - NA_KR // 2, 0, rows - NA_WIN)
    keys = pl.ds(pl.multiple_of(first_row * GRID_W, GRID_W), NA_WIN * GRID_W)
    scale = HEAD_DIM ** -0.5
    for h in range(NA_HEADS):
        s = lax.dot_general(q_ref[:, _head_slice(h)], k_ref[keys, _head_slice(h)], (((1,), (1,)), ((), ())),
                            preferred_element_type=F32) * scale
        s = s + bias_ref[h]
        p = jnp.exp(s - jnp.max(s, axis=-1, keepdims=True))
        p = p / jnp.sum(p, axis=-1, keepdims=True)
        o = jnp.dot(p.astype(BF16), v_ref[keys, _head_slice(h)], preferred_element_type=F32)
        o_ref[:, _head_slice(h)] = o.astype(o_ref.dtype)


def _na_group(qn, kn, vn, bias, row0, b, l):
    rows = l // GRID_W
    nsteps = rows // NA_R
    tq = NA_R * GRID_W
    kind = lambda st: jnp.where(st == 0, 0, jnp.where(st == nsteps - 1, 2, 1))
    seq_spec = pl.BlockSpec((l, NA_WIDTH), lambda bi, st: (row0 // l + bi, 0))
    return pl.pallas_call(
        functools.partial(_na_kernel, rows=rows),
        out_shape=jax.ShapeDtypeStruct((b * l, NA_WIDTH), BF16),
        grid=(b, nsteps),
        in_specs=[pl.BlockSpec((tq, NA_WIDTH), lambda bi, st: (row0 // tq + bi * nsteps + st, 0)),
                  seq_spec, seq_spec,
                  pl.BlockSpec((None, NA_HEADS, tq, NA_WIN * GRID_W), lambda bi, st: (kind(st), 0, 0, 0))],
        out_specs=pl.BlockSpec((tq, NA_WIDTH), lambda bi, st: (bi * nsteps + st, 0)),
        compiler_params=_params(("parallel", "arbitrary")),
    )(qn, kn, vn, bias)


def _na(qn, kn, vn, rpb, groups):
    bias = _na_bias_table(rpb)
    return jnp.concatenate([_na_group(qn, kn, vn, bias, row0, b, l) for row0, b, l in groups], axis=0)


def _ssd_prep_kernel(flags_ref, xc_ref, xp_ref, xn_ref, dt_ref, w_ref, b_ref, dtb_ref,
                     xs_ref, bc_ref, dtf_ref, *, tr):
    fl = flags_ref[pl.program_id(0)]
    halo = 8
    xp = jnp.where((fl & 1) > 0, 0.0, xp_ref[...])
    xn = jnp.where((fl & 2) > 0, 0.0, xn_ref[...])
    ext = jnp.concatenate([xp, xc_ref[...], xn], axis=0)
    n = tr + 2 * halo
    acc = None
    for k in range(SSD_CONV):
        sh = (SSD_CONV // 2 - k) % n
        xk = ext if sh == 0 else pltpu.roll(ext, sh, 0)
        term = xk[halo:halo + tr] * w_ref[k:k + 1, :]
        acc = term if acc is None else acc + term
    acc = acc + b_ref[...]
    y = acc * jax.nn.sigmoid(acc)
    xs_ref[...] = y[:, :SSD_INNER]
    bc_ref[...] = y[:, SSD_INNER:]
    t = dt_ref[...] + dtb_ref[...]
    dtf_ref[...] = jnp.maximum(t, 0.0) + jnp.log1p(jnp.exp(-jnp.abs(t)))


def _dt_pad(v):
    z = jnp.zeros((LANES - SSD_HEADS,), v.dtype)
    return jnp.concatenate([v[0], z, v[1], z])


def _ssd_prep(pm, dt_raw, segs, lw):
    n_tok = pm.shape[0]
    tr = 256
    _, flags = _tile_tables(segs, tr)
    nb8 = tr // 8
    last8 = n_tok // 8 - 1
    ccb = COL_XBC // SSD_CONV_DIM
    w8 = jnp.concatenate([lw['ssd_conv_w'], jnp.zeros((8 - SSD_CONV, SSD_CONV_DIM), F32)], axis=0)
    dtb = _dt_pad(lw['ssd_dt_bias'])[None, :]
    grid_spec = pltpu.PrefetchScalarGridSpec(
        num_scalar_prefetch=1, grid=(n_tok // tr,),
        in_specs=[pl.BlockSpec((tr, SSD_CONV_DIM), lambda i, f: (i, ccb)),
                  pl.BlockSpec((8, SSD_CONV_DIM), lambda i, f: (jnp.maximum(i * nb8 - 1, 0), ccb)),
                  pl.BlockSpec((8, SSD_CONV_DIM), lambda i, f: (jnp.minimum((i + 1) * nb8, last8), ccb)),
                  pl.BlockSpec((tr, 2 * LANES), lambda i, f: (i, 0)),
                  pl.BlockSpec((8, SSD_CONV_DIM), lambda i, f: (0, 0)),
                  pl.BlockSpec((1, SSD_CONV_DIM), lambda i, f: (0, 0)),
                  pl.BlockSpec((1, 2 * LANES), lambda i, f: (0, 0))],
        out_specs=[pl.BlockSpec((tr, SSD_INNER), lambda i, f: (i, 0)),
                   pl.BlockSpec((tr, 2 * SSD_GROUPS * SSD_STATE), lambda i, f: (i, 0)),
                   pl.BlockSpec((tr, 2 * LANES), lambda i, f: (i, 0))])
    return pl.pallas_call(
        functools.partial(_ssd_prep_kernel, tr=tr), grid_spec=grid_spec,
        out_shape=[jax.ShapeDtypeStruct((n_tok, SSD_INNER), F32),
                   jax.ShapeDtypeStruct((n_tok, 2 * SSD_GROUPS * SSD_STATE), F32),
                   jax.ShapeDtypeStruct((n_tok, 2 * LANES), F32)],
        compiler_params=_params(("parallel",)),
    )(jnp.asarray(flags), pm, pm, pm, dt_raw, w8, lw['ssd_conv_b'][None, :], dtb)


def _split3(x):
    hi = x.astype(BF16)
    r = x - hi.astype(F32)
    mid = r.astype(BF16)
    lo = (r - mid.astype(F32)).astype(BF16)
    return hi, mid, lo


def _dot_sel_l(sel, x):
    hi, mid, lo = _split3(x)
    d = lambda t: jnp.dot(sel, t, preferred_element_type=F32)
    return d(lo) + d(mid) + d(hi)


def _dot_sel_r(x, sel):
    hi, mid, lo = _split3(x)
    d = lambda t: jnp.dot(t, sel, preferred_element_type=F32)
    return d(lo) + d(mid) + d(hi)


def _ssd_scan_kernel(order_ref, reset_ref, xs_ref, bc_ref, dt_ref, alog_ref, e_ref, *rest, bwd):
    if bwd:
        yf_ref, z_ref, dsk_ref, g_ref, o_ref, state_ref = rest
    else:
        o_ref, state_ref = rest
    del order_ref
    q = SSD_CHUNK
    n_pairs = SSD_HEADS // 2
    gs = SSD_GROUPS * SSD_STATE

    @pl.when(reset_ref[pl.program_id(0)] > 0)
    def _():
        state_ref[...] = jnp.zeros_like(state_ref)

    dt = dt_ref[...]
    da = dt * (-jnp.exp(alog_ref[...]))
    r_i = lax.broadcasted_iota(jnp.int32, (q, q), 0)
    c_i = lax.broadcasted_iota(jnp.int32, (q, q), 1)
    incl = (r_i <= c_i) if bwd else (r_i >= c_i)
    tri = jnp.where(incl, 1.0, 0.0).astype(BF16)
    cs = _dot_sel_l(tri, da)
    cs_t = cs.T
    e = e_ref[...]
    dt_x = _dot_sel_r(dt, e)
    cs_x = _dot_sel_r(cs, e)
    total = cs_x[0:1, :] if bwd else cs_x[q - 1:q, :]
    xs = xs_ref[...]
    xdt = xs * dt_x
    w_state = (xdt * jnp.exp(total - cs_x)).astype(BF16)
    xdt_b = xdt.astype(BF16)
    ecs_x = jnp.exp(cs_x)
    etot = jnp.exp(total)
    bc = bc_ref[...]
    b_t = bc[:, :gs].T
    lane = lax.broadcasted_iota(jnp.int32, (q, LANES), 1)
    ys = []
    for g in range(SSD_GROUPS):
        b_g = bc[:, g * SSD_STATE:(g + 1) * SSD_STATE].astype(BF16)
        c_g = bc[:, gs + g * SSD_STATE:gs + (g + 1) * SSD_STATE].astype(BF16)
        bt_g = b_t[g * SSD_STATE:(g + 1) * SSD_STATE, :].astype(BF16)
        cb = lax.dot_general(c_g, b_g, (((1,), (1,)), ((), ())), preferred_element_type=F32)
        for j in range(n_pairs // SSD_GROUPS):
            pair = g * (n_pairs // SSD_GROUPS) + j
            lanes = slice(pair * LANES, (pair + 1) * LANES)
            halves = []
            for hh in range(2):
                h = 2 * pair + hh
                diff = jnp.broadcast_to(cs[:, h:h + 1], (q, q)) - jnp.broadcast_to(cs_t[h:h + 1, :], (q, q))
                decay = jnp.where(incl, jnp.exp(diff), 0.0)
                halves.append(jnp.dot((cb * decay).astype(BF16), xdt_b[:, lanes], preferred_element_type=F32))
            y_diag = jnp.where(lane < SSD_HEAD_DIM, halves[0], halves[1])
            s_prev = state_ref[pair]
            y_off = jnp.dot(c_g, s_prev.astype(BF16), preferred_element_type=F32) * ecs_x[:, lanes]
            contrib = jnp.dot(bt_g, w_state[:, lanes], preferred_element_type=F32)
            state_ref[pair] = s_prev * etot[:, lanes] + contrib
            ys.append(y_diag + y_off)
    y = jnp.concatenate(ys, axis=1)
    if not bwd:
        o_ref[...] = y
        return
    y = yf_ref[...] + y + dsk_ref[...] * xs
    z = z_ref[...]
    y = y * (z * jax.nn.sigmoid(z))
    y = y * lax.rsqrt(jnp.mean(y * y, axis=-1, keepdims=True) + EPS)
    o_ref[...] = (y * g_ref[...]).astype(o_ref.dtype)


def _ssd_scan(xs, bc, dtf, alog, e_mat, segs, bwd, extra=()):
    n_tok = xs.shape[0]
    q = SSD_CHUNK
    _, flags = _tile_tables(segs, q)
    n = n_tok // q
    order = np.arange(n, dtype=np.int32)[::-1].copy() if bwd else np.arange(n, dtype=np.int32)
    reset = ((flags[order] & (2 if bwd else 1)) > 0).astype(np.int32)
    d = 1 if bwd else 0
    row = lambda w, cb=0: pl.BlockSpec((q, w), lambda i, o, r: (o[i], cb))
    const = lambda shape: pl.BlockSpec(shape, lambda i, o, r: (0,) * len(shape))
    in_specs = [row(SSD_INNER), row(2 * SSD_GROUPS * SSD_STATE), row(LANES, d),
                pl.BlockSpec((None, 1, LANES), lambda i, o, r: (d, 0, 0)), const((LANES, SSD_INNER))]
    if bwd:
        in_specs += [row(SSD_INNER), row(SSD_INNER, COL_Z // SSD_INNER), const((1, SSD_INNER)),
                     const((1, SSD_INNER))]
    grid_spec = pltpu.PrefetchScalarGridSpec(
        num_scalar_prefetch=2, grid=(n,), in_specs=in_specs, out_specs=row(SSD_INNER),
        scratch_shapes=[pltpu.VMEM((SSD_HEADS // 2, SSD_STATE, LANES), F32)])
    return pl.pallas_call(
        functools.partial(_ssd_scan_kernel, bwd=bwd), grid_spec=grid_spec,
        out_shape=jax.ShapeDtypeStruct((n_tok, SSD_INNER), BF16 if bwd else F32),
        compiler_params=_params(("arbitrary",)),
    )(jnp.asarray(order), jnp.asarray(reset), xs, bc, dtf, alog, e_mat, *extra)


def _ssd(pm, dt_raw, segs, lw):
    xs, bc, dtf = _ssd_prep(pm, dt_raw, segs, lw)
    alog = _dt_pad(lw['ssd_a_log']).reshape(2, 1, LANES)
    e_np = np.zeros((LANES, SSD_INNER), np.float32)
    for h in range(SSD_HEADS):
        e_np[h, h * SSD_HEAD_DIM:(h + 1) * SSD_HEAD_DIM] = 1.0
    e_mat = jnp.asarray(e_np, BF16)
    y_fwd = _ssd_scan(xs, bc, dtf, alog, e_mat, segs, False)
    dsk = jnp.repeat(lw['ssd_d'], SSD_HEAD_DIM)[None, :]
    return _ssd_scan(xs, bc, dtf, alog, e_mat, segs, True,
                     extra=(y_fwd, pm, dsk, lw['ssd_norm_g'][None, :]))


S5_Q = 16
S5_SB = LANES // S5_GROUP
S5_NSB = S5_GROUPS // S5_SB
S5_SBW = S5_Q * LANES
S5_SW = S5_SB * S5_STATE
S5_NS = S5_GROUPS * S5_STATE


def _s5_weights(lw):
    hp = lax.Precision.HIGHEST
    qn, g, p, c = S5_Q, S5_GROUPS, S5_STATE, S5_GROUP
    nsb, sb = S5_NSB, S5_SB
    cmul = lambda a, b: (a[0] * b[0] - a[1] * b[1], a[0] * b[1] + a[1] * b[0])
    a_re, a_im = lw['s5_a_re'], lw['s5_a_im']
    step = jnp.exp(lw['s5_log_step'])[..., None]
    mag = jnp.exp(a_re * step)
    lam_bar = (mag * jnp.cos(a_im * step), mag * jnp.sin(a_im * step))
    den = a_re * a_re + a_im * a_im
    coef = cmul((lam_bar[0] - 1.0, lam_bar[1]), (a_re / den, -a_im / den))
    b_bar = cmul((coef[0][..., None], coef[1][..., None]),
                 (lw['s5_b_re'][None], lw['s5_b_im'][None]))
    c_c = (lw['s5_c_re'], lw['s5_c_im'])
    pows = [(jnp.ones_like(mag), jnp.zeros_like(mag))]
    for _ in range(qn):
        pows.append(cmul(pows[-1], lam_bar))
    pw = (jnp.stack([t[0] for t in pows], axis=1), jnp.stack([t[1] for t in pows], axis=1))

    def lag_kernels(d):
        t1 = cmul((c_c[0][d][None], c_c[1][d][None]),
                  (pw[0][d, :qn][:, :, None, :], pw[1][d, :qn][:, :, None, :]))
        return (jnp.einsum('mgip,gpj->mgij', t1[0], b_bar[0][d], precision=hp)
                - jnp.einsum('mgip,gpj->mgij', t1[1], b_bar[1][d], precision=hp))

    s_idx = np.arange(qn)[:, None]
    t_idx = np.arange(qn)[None, :]
    df = t_idx - s_idx
    kf = lag_kernels(0)[np.clip(df, 0, qn - 1)] * jnp.asarray(df >= 0, F32)[:, :, None, None, None]
    kb = lag_kernels(1)[np.clip(-df, 0, qn - 1)] * jnp.asarray(df <= 0, F32)[:, :, None, None, None]
    tt = jnp.transpose(kf + kb, (2, 0, 4, 1, 3))
    eye = jnp.eye(sb, dtype=F32)
    tt = jnp.einsum('kgsjti,gh->ksgjthi', tt.reshape(nsb, sb, qn, c, qn, c), eye).reshape(nsb, S5_SBW, S5_SBW)

    def in_op(d, powers):
        w = cmul((pw[0][d, powers][:, :, :, None], pw[1][d, powers][:, :, :, None]),
                 (b_bar[0][d][None], b_bar[1][d][None]))
        return tuple(jnp.transpose(t, (1, 0, 3, 2)) for t in w)

    wf = in_op(0, qn - 1 - np.arange(qn))
    wb = in_op(1, np.arange(qn))
    w4 = jnp.stack([wf[0], wf[1], wb[0], wb[1]], axis=3)
    wa = jnp.einsum('kgsjcp,gh->ksgjchp', w4.reshape(nsb, sb, qn, c, 4, p), eye).reshape(nsb, S5_SBW, 4 * S5_SW)

    def out_op(d, powers):
        return cmul((jnp.transpose(c_c[0][d], (0, 2, 1))[:, :, None, :],
                     jnp.transpose(c_c[1][d], (0, 2, 1))[:, :, None, :]),
                    (jnp.transpose(pw[0][d, powers], (1, 2, 0))[:, :, :, None],
                     jnp.transpose(pw[1][d, powers], (1, 2, 0))[:, :, :, None]))

    vf = out_op(0, 1 + np.arange(qn))
    vb = out_op(1, qn - np.arange(qn))
    v4 = jnp.stack([vf[0], -vf[1], vb[0], -vb[1]], axis=0)
    vc = jnp.einsum('ckgpti,gh->kcgpthi', v4.reshape(4, nsb, sb, p, qn, c), eye).reshape(nsb, 4 * S5_SW, S5_SBW)

    lam_q = (pw[0][:, qn].reshape(2, 1, S5_NS), pw[1][:, qn].reshape(2, 1, S5_NS))
    dsk = jnp.broadcast_to(lw['s5_d'].reshape(nsb, 1, 1, LANES), (nsb, 1, qn, LANES)).reshape(nsb, 1, S5_SBW)
    return dict(tt=tt.astype(BF16), wa=wa.astype(BF16), vc=vc.astype(BF16),
                lam_re=lam_q[0], lam_im=lam_q[1], dsk=dsk)


S5_TT = 4096


def _s5_blocks(u_ref, mt):
    return jnp.concatenate([u_ref[pl.ds(t, mt, stride=S5_Q), :] for t in range(S5_Q)], axis=1)


def _s5_in_kernel(u_ref, w_ref, fre_ref, fim_ref, bre_ref, bim_ref, *, mt):
    r = jnp.dot(_s5_blocks(u_ref, mt).astype(BF16), w_ref[...], preferred_element_type=F32)
    fre_ref[...] = r[:, 0 * S5_SW:1 * S5_SW]
    fim_ref[...] = r[:, 1 * S5_SW:2 * S5_SW]
    bre_ref[...] = r[:, 2 * S5_SW:3 * S5_SW]
    bim_ref[...] = r[:, 3 * S5_SW:4 * S5_SW]


def _s5_in(pm, wa):
    n_tok = pm.shape[0]
    tt = min(S5_TT, n_tok)
    mt = tt // S5_Q
    ucb = COL_U_S5 // LANES
    st = pl.BlockSpec((mt, S5_SW), lambda k, i: (i, k))
    sds = jax.ShapeDtypeStruct((n_tok // S5_Q, S5_NS), F32)
    return pl.pallas_call(
        functools.partial(_s5_in_kernel, mt=mt), out_shape=[sds] * 4, grid=(S5_NSB, n_tok // tt),
        in_specs=[pl.BlockSpec((tt, LANES), lambda k, i: (i, ucb + k)),
                  pl.BlockSpec((None, S5_SBW, 4 * S5_SW), lambda k, i: (k, 0, 0))],
        out_specs=[st] * 4,
        compiler_params=_params(("parallel", "parallel")),
    )(pm, wa)


def _s5_rec_kernel(flags_ref, fre_ref, fim_ref, bre_ref, bim_ref, lam_re_ref, lam_im_ref,
                   xre_ref, xim_ref, zre_ref, zim_ref, state_ref, *, tc):
    i = pl.program_id(0)
    n = pl.num_programs(0)

    @pl.when((flags_ref[i] & 1) > 0)
    def _():
        state_ref[0:2] = jnp.zeros((2, 1, S5_NS), F32)

    @pl.when((flags_ref[n - 1 - i] & 2) > 0)
    def _():
        state_ref[2:4] = jnp.zeros((2, 1, S5_NS), F32)

    flr, fli = lam_re_ref[0], lam_im_ref[0]
    blr, bli = lam_re_ref[1], lam_im_ref[1]

    def body(k, carry):
        xr, xi, zr, zi = carry
        cf = pl.ds(k, 1)
        cb = pl.ds(tc - 1 - k, 1)
        xre_ref[cf, :] = xr
        xim_ref[cf, :] = xi
        zre_ref[cb, :] = zr
        zim_ref[cb, :] = zi
        return (flr * xr - fli * xi + fre_ref[cf, :], flr * xi + fli * xr + fim_ref[cf, :],
                blr * zr - bli * zi + bre_ref[cb, :], blr * zi + bli * zr + bim_ref[cb, :])

    out = lax.fori_loop(0, tc, body, tuple(state_ref[j] for j in range(4)))
    for j in range(4):
        state_ref[j] = out[j]


def _s5_rec(fre, fim, bre, bim, lam_re, lam_im, segs):
    m = fre.shape[0]
    tc = min(128, min(l for _, l in segs) // S5_Q)
    _, flags = _tile_tables(segs, tc * S5_Q)
    n = m // tc
    fwd = pl.BlockSpec((tc, S5_NS), lambda i, f: (i, 0))
    bwd = pl.BlockSpec((tc, S5_NS), lambda i, f: (n - 1 - i, 0))
    lam = pl.BlockSpec((2, 1, S5_NS), lambda i, f: (0, 0, 0))
    grid_spec = pltpu.PrefetchScalarGridSpec(
        num_scalar_prefetch=1, grid=(n,), in_specs=[fwd, fwd, bwd, bwd, lam, lam],
        out_specs=[fwd, fwd, bwd, bwd],
        scratch_shapes=[pltpu.VMEM((4, 1, S5_NS), F32)])
    sds = jax.ShapeDtypeStruct((m, S5_NS), F32)
    return pl.pallas_call(
        functools.partial(_s5_rec_kernel, tc=tc), grid_spec=grid_spec, out_shape=[sds] * 4,
        compiler_params=_params(("arbitrary",)),
    )(jnp.asarray(flags), fre, fim, bre, bim, lam_re, lam_im)


def _s5_out_kernel(u_ref, tt_ref, xre_ref, xim_ref, zre_ref, zim_ref, vc_ref, dsk_ref, y_ref, *, mt):
    u = _s5_blocks(u_ref, mt)
    intra = jnp.dot(u.astype(BF16), tt_ref[...], preferred_element_type=F32)
    st = jnp.concatenate([xre_ref[...], xim_ref[...], zre_ref[...], zim_ref[...]], axis=1).astype(BF16)
    carry = jnp.dot(st, vc_ref[...], preferred_element_type=F32)
    y = intra + carry + dsk_ref[...] * u
    for t in range(S5_Q):
        y_ref[pl.ds(t, mt, stride=S5_Q), :] = y[:, t * LANES:(t + 1) * LANES]


def _s5_out(pm, w, xre, xim, zre, zim):
    n_tok = pm.shape[0]
    tt = min(S5_TT, n_tok)
    mt = tt // S5_Q
    ucb = COL_U_S5 // LANES
    st = pl.BlockSpec((mt, S5_SW), lambda k, i: (i, k))
    return pl.pallas_call(
        functools.partial(_s5_out_kernel, mt=mt),
        out_shape=jax.ShapeDtypeStruct((n_tok, S5_WIDTH), F32), grid=(S5_NSB, n_tok // tt),
        in_specs=[pl.BlockSpec((tt, LANES), lambda k, i: (i, ucb + k)),
                  pl.BlockSpec((None, S5_SBW, S5_SBW), lambda k, i: (k, 0, 0)),
                  st, st, st, st,
                  pl.BlockSpec((None, 4 * S5_SW, S5_SBW), lambda k, i: (k, 0, 0)),
                  pl.BlockSpec((None, 1, S5_SBW), lambda k, i: (k, 0, 0))],
        out_specs=pl.BlockSpec((tt, LANES), lambda k, i: (i, k)),
        compiler_params=_params(("parallel", "parallel")),
    )(pm, w['tt'], xre, xim, zre, zim, w['vc'], w['dsk'])


def _s5_glu_kernel(y_ref, w_ref, b_ref, o_ref):
    g = jax.nn.gelu(y_ref[...])
    t = jnp.dot(g.astype(BF16), w_ref[...], preferred_element_type=F32) + b_ref[...]
    o_ref[...] = (g * jax.nn.sigmoid(t)).astype(o_ref.dtype)


def _s5_glu(y, glu_w, glu_b):
    n_tok = y.shape[0]
    tr = 512
    return pl.pallas_call(
        _s5_glu_kernel, out_shape=jax.ShapeDtypeStruct((n_tok, S5_WIDTH), BF16), grid=(n_tok // tr,),
        in_specs=[pl.BlockSpec((tr, S5_WIDTH), lambda i: (i, 0)),
                  pl.BlockSpec((S5_WIDTH, S5_WIDTH), lambda i: (0, 0)),
                  pl.BlockSpec((1, S5_WIDTH), lambda i: (0, 0))],
        out_specs=pl.BlockSpec((tr, S5_WIDTH), lambda i: (i, 0)),
        compiler_params=_params(("parallel",)),
    )(y, glu_w.astype(BF16), glu_b[None, :])


def _s5(pm, segs, lw):
    w = _s5_weights(lw)
    fre, fim, bre, bim = _s5_in(pm, w['wa'])
    xre, xim, zre, zim = _s5_rec(fre, fim, bre, bim, w['lam_re'], w['lam_im'], segs)
    y = _s5_out(pm, w, xre, xim, zre, zim)
    return _s5_glu(y, lw['s5_glu_w'], lw['s5_glu_b'])


def _layer(x, c8, lw, ada_w, ada_b, layer):
    w_in = lw['w_in']
    o = IN_OFFS
    piece = lambda k: w_in[:, o[k]:o[k + 1]]
    w_mix = jnp.concatenate([piece(0), piece(3), piece(6), piece(7), piece(8), piece(9),
                             piece(1), piece(4), piece(5)], axis=1).astype(BF16)
    zpad = jnp.zeros((D_MODEL, LANES - SSD_HEADS), F32)
    w_dt = jnp.concatenate([piece(2)[:, :SSD_HEADS], zpad, piece(2)[:, SSD_HEADS:], zpad], axis=1).astype(BF16)
    w_gate = jnp.transpose(piece(10).reshape(D_MODEL, N_BRANCH, D_MODEL), (1, 0, 2)).astype(BF16)
    w_branch = jnp.stack([lw['w_branch_ssd'], lw['w_branch_swa'], lw['w_branch_s5'],
                          lw['w_branch_na']]).astype(BF16)
    w_out = lw['w_out'].astype(BF16)
    w1 = lw['ffn_w1'].astype(BF16)
    w3 = lw['ffn_w3'].astype(BF16)
    w2 = lw['ffn_w2'].astype(BF16)

    mod = _ada(c8, ada_w, ada_b, layer)
    mod4 = mod.reshape(8, 6, 1, D_MODEL)

    h = _norm_mod(x, lw['norm1_g'], mod4, 1, 0)
    pm = _mm(h, w_mix, 1024, F32)
    dt_raw = _mm(h, w_dt, 2 * LANES, F32)

    y_ssd = _ssd(pm, dt_raw, SEGS, lw)
    qs, ks, qn, kn, vn = _qk_prep(pm, SEGS, lw)
    y_swa = _swa(qs, ks, pm, lw['swa_sink'], SEGS)
    y_s5 = _s5(pm, SEGS, lw)
    y_na = _na(qn, kn, vn, lw['na_rpb'], GROUPS)
    merged = _merge(h, w_gate, (y_ssd, y_swa, y_s5, y_na), w_branch)
    x = _mm_res(merged, w_out, x, mod4, 2, 1024, D_MODEL)

    h2 = _norm_mod(x, lw['norm2_g'], mod4, 4, 3)
    u = _ffn_up(h2, w1, w3)
    return _mm_res(u, w2, x, mod4, 5, 512, D_FF // 2)


_LAYER_KEYS = ('ada_w', 'ada_b', 'norm1_g', 'norm2_g', 'w_in',
               'ssd_conv_w', 'ssd_conv_b', 'ssd_dt_bias', 'ssd_a_log', 'ssd_d', 'ssd_norm_g',
               'swa_q_norm_g', 'swa_k_norm_g', 'swa_sink',
               's5_a_re', 's5_a_im', 's5_log_step', 's5_b_re', 's5_b_im', 's5_c_re', 's5_c_im',
               's5_d', 's5_glu_w', 's5_glu_b',
               'na_q_norm_g', 'na_k_norm_g', 'na_rpb',
               'w_branch_ssd', 'w_branch_swa', 'w_branch_s5', 'w_branch_na', 'w_out',
               'ffn_w1', 'ffn_w3', 'ffn_w2')


def kernel(x_prompt, x_sample, c_prompt, c_sample, ada_w, ada_b, norm1_g, norm2_g, w_in, ssd_conv_w, ssd_conv_b, ssd_dt_bias, ssd_a_log, ssd_d, ssd_norm_g, swa_q_norm_g, swa_k_norm_g, swa_sink, s5_a_re, s5_a_im, s5_log_step, s5_b_re, s5_b_im, s5_c_re, s5_c_im, s5_d, s5_glu_w, s5_glu_b, na_q_norm_g, na_k_norm_g, na_rpb, w_branch_ssd, w_branch_swa, w_branch_s5, w_branch_na, w_out, ffn_w1, ffn_w3, ffn_w2):
    stacked = dict(zip(_LAYER_KEYS, (ada_w, ada_b, norm1_g, norm2_g, w_in,
                                     ssd_conv_w, ssd_conv_b, ssd_dt_bias, ssd_a_log, ssd_d, ssd_norm_g,
                                     swa_q_norm_g, swa_k_norm_g, swa_sink,
                                     s5_a_re, s5_a_im, s5_log_step, s5_b_re, s5_b_im, s5_c_re, s5_c_im,
                                     s5_d, s5_glu_w, s5_glu_b,
                                     na_q_norm_g, na_k_norm_g, na_rpb,
                                     w_branch_ssd, w_branch_swa, w_branch_s5, w_branch_na, w_out,
                                     ffn_w1, ffn_w3, ffn_w2)))
    x = jnp.concatenate([x_prompt.reshape(BATCH * SEQ, D_MODEL),
                         x_sample.reshape(DEC_BATCH * DEC_SEQ, D_MODEL)], axis=0)
    c8 = jnp.concatenate([c_prompt, c_sample, jnp.zeros((8 - N_SEQS, D_MODEL), F32)], axis=0)
    for i in range(DEPTH):
        lw = {k: v[i] for k, v in stacked.items() if k not in ('ada_w', 'ada_b')}
        x = _layer(x, c8, lw, ada_w, ada_b, i)
    n_p = BATCH * SEQ
    return (x[:n_p].reshape(BATCH, SEQ, D_MODEL), x[n_p:].reshape(DEC_BATCH, DEC_SEQ, D_MODEL))
```

```python
import functools

import jax
import jax.numpy as jnp
import numpy as np
from jax import lax
from jax.experimental import pallas as pl
from jax.experimental.pallas import tpu as pltpu

D_MODEL = 4096
BATCH = 2
SEQ = 4096
DEPTH = 2
DEC_BATCH = 4
DEC_SEQ = 2048
N_TOK = BATCH * SEQ + DEC_BATCH * DEC_SEQ
N_SEQS = BATCH + DEC_BATCH
GROUPS = ((0, BATCH, SEQ), (BATCH * SEQ, DEC_BATCH, DEC_SEQ))
SEGS = tuple((row0 + i * l, l) for row0, b, l in GROUPS for i in range(b))

EPS = 1e-6
HEAD_DIM = 128
N_BRANCH = 4
SSD_HEADS = 16
SSD_HEAD_DIM = 64
SSD_INNER = SSD_HEADS * SSD_HEAD_DIM
SSD_GROUPS = 2
SSD_STATE = 128
SSD_CONV = 5
SSD_CHUNK = 128
SSD_CONV_DIM = SSD_INNER + 2 * SSD_GROUPS * SSD_STATE
SWA_HEADS = 8
SWA_KV_HEADS = 2
SWA_WIDTH = SWA_HEADS * HEAD_DIM
SWA_KV_WIDTH = SWA_KV_HEADS * HEAD_DIM
SWA_WINDOW = 128
SWA_BLOCK = 128
ROPE_THETA = 10000.0
S5_WIDTH = 1024
S5_GROUP = 16
S5_GROUPS = S5_WIDTH // S5_GROUP
S5_STATE = 64
NA_HEADS = 8
NA_WIDTH = NA_HEADS * HEAD_DIM
GRID_W = 64
NA_KR = 8
NA_KW = 16
D_FF = ((8 * D_MODEL + 3 * 256 - 1) // (3 * 256)) * 256
IN_SIZES = (SSD_INNER, SSD_CONV_DIM, 2 * SSD_HEADS,
            SWA_WIDTH, SWA_KV_WIDTH, SWA_KV_WIDTH,
            S5_WIDTH,
            NA_WIDTH, NA_WIDTH, NA_WIDTH,
            N_BRANCH * D_MODEL)
IN_OFFS = tuple(int(v) for v in np.cumsum((0,) + IN_SIZES))

LANES = 128
TM = 1024
N_MIX = IN_OFFS[10] - IN_SIZES[2]
VMEM_LIMIT = 56 * 1024 * 1024

COL_Z, COL_Q_SWA, COL_U_S5, COL_Q_NA, COL_K_NA, COL_V_NA = 0, 1024, 2048, 3072, 4096, 5120
COL_XBC = 6144
COL_K_SWA = COL_XBC + SSD_CONV_DIM
COL_V_SWA = COL_K_SWA + SWA_KV_WIDTH

F32 = jnp.float32
BF16 = jnp.bfloat16


def _batch_of_tile(i, tile):
    n_p = (BATCH * SEQ) // tile
    return jnp.where(i < n_p, i // (SEQ // tile), BATCH + (i - n_p) // (DEC_SEQ // tile))


def _params(sem):
    return pltpu.CompilerParams(dimension_semantics=sem, vmem_limit_bytes=VMEM_LIMIT)


def _ada_kernel(c_ref, w_ref, b_ref, o_ref):
    c = c_ref[...]
    a = (c * jax.nn.sigmoid(c)).astype(BF16)
    o_ref[...] = jnp.dot(a, w_ref[...].astype(BF16), preferred_element_type=F32) + b_ref[...]


def _ada(c8, ada_w, ada_b, layer):
    n = ada_w.shape[2]
    tn = 512
    return pl.pallas_call(
        _ada_kernel,
        out_shape=jax.ShapeDtypeStruct((8, n), F32),
        grid=(n // tn,),
        in_specs=[pl.BlockSpec((8, D_MODEL), lambda j: (0, 0)),
                  pl.BlockSpec((None, D_MODEL, tn), lambda j: (layer, 0, j)),
                  pl.BlockSpec((1, tn), lambda j: (0, j))],
        out_specs=pl.BlockSpec((8, tn), lambda j: (0, j)),
        compiler_params=_params(("parallel",)),
    )(c8, ada_w, ada_b[layer].reshape(1, n))


def _norm_mod_kernel(x_ref, g_ref, scale_ref, shift_ref, o_ref):
    x = x_ref[...]
    y = x * lax.rsqrt(jnp.mean(x * x, axis=-1, keepdims=True) + EPS)
    y = y * g_ref[...]
    o_ref[...] = (y * (1.0 + scale_ref[...]) + shift_ref[...]).astype(o_ref.dtype)


def _norm_mod(x, g, mod4, scale_idx, shift_idx):
    tr = 256
    return pl.pallas_call(
        _norm_mod_kernel,
        out_shape=jax.ShapeDtypeStruct((N_TOK, D_MODEL), BF16),
        grid=(N_TOK // tr,),
        in_specs=[pl.BlockSpec((tr, D_MODEL), lambda i: (i, 0)),
                  pl.BlockSpec((1, D_MODEL), lambda i: (0, 0)),
                  pl.BlockSpec((None, None, 1, D_MODEL), lambda i: (_batch_of_tile(i, tr), scale_idx, 0, 0)),
                  pl.BlockSpec((None, None, 1, D_MODEL), lambda i: (_batch_of_tile(i, tr), shift_idx, 0, 0))],
        out_specs=pl.BlockSpec((tr, D_MODEL), lambda i: (i, 0)),
        compiler_params=_params(("parallel",)),
    )(x, g.reshape(1, D_MODEL), mod4, mod4)


def _mm_kernel(a_ref, b_ref, o_ref):
    o_ref[...] = jnp.dot(a_ref[...], b_ref[...], preferred_element_type=F32).astype(o_ref.dtype)


def _mm(a, b, layer, tn, out_dtype):
    m, k = a.shape
    n = b.shape[2]
    return pl.pallas_call(
        _mm_kernel,
        out_shape=jax.ShapeDtypeStruct((m, n), out_dtype),
        grid=(m // TM, n // tn),
        in_specs=[pl.BlockSpec((TM, k), lambda i, j: (i, 0)),
                  pl.BlockSpec((None, k, tn), lambda i, j: (layer, 0, j))],
        out_specs=pl.BlockSpec((TM, tn), lambda i, j: (i, j)),
        compiler_params=_params(("parallel", "parallel")),
    )(a, b)


def _merge_kernel(h_ref, wg_ref, m_ref, wb_ref, o_ref, acc_ref):
    b = pl.program_id(2)
    logits = jnp.dot(h_ref[...], wg_ref[...], preferred_element_type=F32)
    y = jnp.dot(m_ref[...], wb_ref[...], preferred_element_type=F32)
    contrib = jax.nn.sigmoid(logits) * y

    @pl.when(b == 0)
    def _():
        acc_ref[...] = contrib

    @pl.when(b > 0)
    def _():
        acc_ref[...] += contrib

    @pl.when(b == N_BRANCH - 1)
    def _():
        o_ref[...] = acc_ref[...].astype(o_ref.dtype)


def _merge(h, wg, mix, wb, layer):
    tn = 512
    w = mix.shape[2]
    return pl.pallas_call(
        _merge_kernel,
        out_shape=jax.ShapeDtypeStruct((N_TOK, D_MODEL), BF16),
        grid=(N_TOK // TM, D_MODEL // tn, N_BRANCH),
        in_specs=[pl.BlockSpec((TM, D_MODEL), lambda i, j, b: (i, 0)),
                  pl.BlockSpec((None, None, D_MODEL, tn), lambda i, j, b: (layer, b, 0, j)),
                  pl.BlockSpec((None, TM, w), lambda i, j, b: (b, i, 0)),
                  pl.BlockSpec((None, None, w, tn), lambda i, j, b: (layer, b, 0, j))],
        out_specs=pl.BlockSpec((TM, tn), lambda i, j, b: (i, j)),
        scratch_shapes=[pltpu.VMEM((TM, tn), F32)],
        compiler_params=_params(("parallel", "parallel", "arbitrary")),
    )(h, wg, mix, wb)


def _mm_res_kernel(a_ref, b_ref, x_ref, gate_ref, o_ref, acc_ref, *, nk):
    d = jnp.dot(a_ref[...], b_ref[...], preferred_element_type=F32)
    if nk == 1:
        o_ref[...] = x_ref[...] + gate_ref[...] * d
        return
    k = pl.program_id(2)

    @pl.when(k == 0)
    def _():
        acc_ref[...] = d

    @pl.when(k > 0)
    def _():
        acc_ref[...] += d

    @pl.when(k == nk - 1)
    def _():
        o_ref[...] = x_ref[...] + gate_ref[...] * acc_ref[...]


def _mm_res(a, b, layer, x, mod4, gate_idx, tn, tk):
    m, kdim = a.shape
    n = b.shape[2]
    nk = kdim // tk
    return pl.pallas_call(
        functools.partial(_mm_res_kernel, nk=nk),
        out_shape=jax.ShapeDtypeStruct((m, n), F32),
        grid=(m // TM, n // tn, nk),
        in_specs=[pl.BlockSpec((TM, tk), lambda i, j, k: (i, k)),
                  pl.BlockSpec((None, tk, tn), lambda i, j, k: (layer, k, j)),
                  pl.BlockSpec((TM, tn), lambda i, j, k: (i, j)),
                  pl.BlockSpec((None, None, 1, tn), lambda i, j, k: (_batch_of_tile(i, TM), gate_idx, 0, j))],
        out_specs=pl.BlockSpec((TM, tn), lambda i, j, k: (i, j)),
        scratch_shapes=[pltpu.VMEM((TM, tn), F32)],
        compiler_params=_params(("parallel", "parallel", "arbitrary")),
    )(a, b, x, mod4)


def _ffn_up_kernel(h_ref, w1_ref, w3_ref, o_ref):
    h = h_ref[...]
    a = jnp.dot(h, w1_ref[...], preferred_element_type=F32)
    b = jnp.dot(h, w3_ref[...], preferred_element_type=F32)
    o_ref[...] = (a * jax.nn.sigmoid(a) * b).astype(o_ref.dtype)


def _ffn_up(h, w1, w3, layer):
    tn = 512
    n = w1.shape[2]
    return pl.pallas_call(
        _ffn_up_kernel,
        out_shape=jax.ShapeDtypeStruct((N_TOK, n), BF16),
        grid=(N_TOK // TM, pl.cdiv(n, tn)),
        in_specs=[pl.BlockSpec((TM, D_MODEL), lambda i, j: (i, 0)),
                  pl.BlockSpec((None, D_MODEL, tn), lambda i, j: (layer, 0, j)),
                  pl.BlockSpec((None, D_MODEL, tn), lambda i, j: (layer, 0, j))],
        out_specs=pl.BlockSpec((TM, tn), lambda i, j: (i, j)),
        compiler_params=_params(("parallel", "parallel")),
    )(h, w1, w3)


def _tile_tables(segs, tile):
    pos, flags = [], []
    for _, length in segs:
        n = length // tile
        for j in range(n):
            pos.append(j)
            flags.append((1 if j == 0 else 0) | (2 if j == n - 1 else 0))
    return np.asarray(pos, np.int32), np.asarray(flags, np.int32)


def _head_slice(h):
    return slice(h * HEAD_DIM, (h + 1) * HEAD_DIM)


def _qk_prep_kernel(pos_ref, qs_ref, ks_ref, qn_ref, kn_ref, vn_ref, cos_ref, sin_ref, g_ref,
                    oqs_ref, oks_ref, oqn_ref, okn_ref, ovn_ref):
    del pos_ref
    cos = cos_ref[...]
    sin = sin_ref[...]

    def hnorm(x, g):
        return x * lax.rsqrt(jnp.mean(x * x, axis=-1, keepdims=True) + EPS) * g

    def rope(x):
        return x * cos + pltpu.roll(x, HEAD_DIM // 2, 1) * sin

    for h in range(SWA_HEADS):
        oqs_ref[:, _head_slice(h)] = rope(hnorm(qs_ref[:, _head_slice(h)], g_ref[0:1, :])).astype(BF16)
    for h in range(SWA_KV_HEADS):
        oks_ref[:, _head_slice(h)] = rope(hnorm(ks_ref[:, _head_slice(h)], g_ref[1:2, :])).astype(BF16)
    for h in range(NA_HEADS):
        oqn_ref[:, _head_slice(h)] = hnorm(qn_ref[:, _head_slice(h)], g_ref[2:3, :]).astype(BF16)
        okn_ref[:, _head_slice(h)] = hnorm(kn_ref[:, _head_slice(h)], g_ref[3:4, :]).astype(BF16)
    ovn_ref[...] = vn_ref[...].astype(BF16)


def _rope_tables(max_len):
    half = HEAD_DIM // 2
    inv_freq = ROPE_THETA ** (-jnp.arange(half, dtype=F32) / half)
    ang = jnp.arange(max_len, dtype=F32)[:, None] * inv_freq[None, :]
    cos, sin = jnp.cos(ang), jnp.sin(ang)
    return jnp.concatenate([cos, cos], axis=1), jnp.concatenate([-sin, sin], axis=1)


def _qk_prep(pm, segs, lw):
    n_tok = pm.shape[0]
    tr = 256
    pos, _ = _tile_tables(segs, tr)
    cos, sin = _rope_tables(max(l for _, l in segs))
    gains = jnp.concatenate([lw['swa_q_norm_g'][None], lw['swa_k_norm_g'][None],
                             lw['na_q_norm_g'][None], lw['na_k_norm_g'][None],
                             jnp.zeros((4, HEAD_DIM), F32)], axis=0)
    wide = lambda cb: pl.BlockSpec((tr, 1024), lambda i, p: (i, cb))
    narrow = lambda cb: pl.BlockSpec((tr, 256), lambda i, p: (i, cb))
    grid_spec = pltpu.PrefetchScalarGridSpec(
        num_scalar_prefetch=1, grid=(n_tok // tr,),
        in_specs=[wide(COL_Q_SWA // 1024), narrow(COL_K_SWA // 256), wide(COL_Q_NA // 1024),
                  wide(COL_K_NA // 1024), wide(COL_V_NA // 1024),
                  pl.BlockSpec((tr, HEAD_DIM), lambda i, p: (p[i], 0)),
                  pl.BlockSpec((tr, HEAD_DIM), lambda i, p: (p[i], 0)),
                  pl.BlockSpec((8, HEAD_DIM), lambda i, p: (0, 0))],
        out_specs=[pl.BlockSpec((tr, 1024), lambda i, p: (i, 0)),
                   pl.BlockSpec((tr, 256), lambda i, p: (i, 0)),
                   pl.BlockSpec((tr, 1024), lambda i, p: (i, 0)),
                   pl.BlockSpec((tr, 1024), lambda i, p: (i, 0)),
                   pl.BlockSpec((tr, 1024), lambda i, p: (i, 0))])
    sds = lambda w: jax.ShapeDtypeStruct((n_tok, w), BF16)
    return pl.pallas_call(
        _qk_prep_kernel, grid_spec=grid_spec,
        out_shape=[sds(1024), sds(256), sds(1024), sds(1024), sds(1024)],
        compiler_params=_params(("parallel",)),
    )(jnp.asarray(pos), pm, pm, pm, pm, pm, cos, sin, gains)


SWA_TQ = 512
SWA_GRP = SWA_HEADS // SWA_KV_HEADS


def _swa_kernel(flags_ref, sink_ref, q_ref, kc_ref, kp_ref, kn_ref, vc_ref, vp_ref, vn_ref, o_ref, *, tq):
    fl = flags_ref[pl.program_id(0)]
    lo = jnp.where((fl & 1) > 0, SWA_BLOCK, 0)
    hi = jnp.where((fl & 2) > 0, 2 * SWA_BLOCK, 3 * SWA_BLOCK)
    nqb = tq // SWA_BLOCK
    m_rows = SWA_GRP * SWA_BLOCK
    row = lax.broadcasted_iota(jnp.int32, (m_rows, 3 * SWA_BLOCK), 0) & (SWA_BLOCK - 1)
    col = lax.broadcasted_iota(jnp.int32, (m_rows, 3 * SWA_BLOCK), 1)
    band = (col >= row) & (col <= row + 2 * SWA_WINDOW)
    scale = HEAD_DIM ** -0.5
    for g in range(SWA_KV_HEADS):
        ks = _head_slice(g)
        k_ext = jnp.concatenate([kp_ref[:, ks], kc_ref[:, ks], kn_ref[:, ks]], axis=0)
        v_ext = jnp.concatenate([vp_ref[:, ks], vc_ref[:, ks], vn_ref[:, ks]], axis=0).astype(BF16)
        sk = jnp.concatenate([jnp.full((SWA_BLOCK, 1), sink_ref[g * SWA_GRP + h], F32)
                              for h in range(SWA_GRP)], axis=0)
        for qb in range(nqb):
            rows = slice(qb * SWA_BLOCK, (qb + 1) * SWA_BLOCK)
            q = jnp.concatenate([q_ref[rows, _head_slice(g * SWA_GRP + h)] for h in range(SWA_GRP)], axis=0)
            keys = k_ext[qb * SWA_BLOCK:(qb + 3) * SWA_BLOCK]
            vals = v_ext[qb * SWA_BLOCK:(qb + 3) * SWA_BLOCK]
            s = lax.dot_general(q, keys, (((1,), (1,)), ((), ())), preferred_element_type=F32) * scale
            mask = band
            if qb == 0:
                mask = mask & (col >= lo)
            if qb == nqb - 1:
                mask = mask & (col < hi)
            s = jnp.where(mask, s, -jnp.inf)
            m = jnp.maximum(jnp.max(s, axis=-1, keepdims=True), sk)
            p = jnp.exp(s - m)
            denom = jnp.sum(p, axis=-1, keepdims=True) + jnp.exp(sk - m)
            o = jnp.dot((p / denom).astype(BF16), vals, preferred_element_type=F32)
            for h in range(SWA_GRP):
                o_ref[rows, _head_slice(g * SWA_GRP + h)] = o[h * SWA_BLOCK:(h + 1) * SWA_BLOCK].astype(o_ref.dtype)


def _swa(qs, ks, pm, sink, segs):
    n_tok = qs.shape[0]
    tq = min(SWA_TQ, min(l for _, l in segs))
    _, flags = _tile_tables(segs, tq)
    nb = tq // SWA_BLOCK
    last_blk = n_tok // SWA_BLOCK - 1
    prev_map = lambda cb: (lambda i, f: (jnp.maximum(i * nb - 1, 0), cb))
    next_map = lambda cb: (lambda i, f: (jnp.minimum((i + 1) * nb, last_blk), cb))
    vcb = COL_V_SWA // SWA_KV_WIDTH
    grid_spec = pltpu.PrefetchScalarGridSpec(
        num_scalar_prefetch=1, grid=(n_tok // tq,),
        in_specs=[pl.BlockSpec(memory_space=pltpu.SMEM),
                  pl.BlockSpec((tq, SWA_WIDTH), lambda i, f: (i, 0)),
                  pl.BlockSpec((tq, SWA_KV_WIDTH), lambda i, f: (i, 0)),
                  pl.BlockSpec((SWA_BLOCK, SWA_KV_WIDTH), prev_map(0)),
                  pl.BlockSpec((SWA_BLOCK, SWA_KV_WIDTH), next_map(0)),
                  pl.BlockSpec((tq, SWA_KV_WIDTH), lambda i, f: (i, vcb)),
                  pl.BlockSpec((SWA_BLOCK, SWA_KV_WIDTH), prev_map(vcb)),
                  pl.BlockSpec((SWA_BLOCK, SWA_KV_WIDTH), next_map(vcb))],
        out_specs=pl.BlockSpec((tq, SWA_WIDTH), lambda i, f: (i, 0)))
    return pl.pallas_call(
        functools.partial(_swa_kernel, tq=tq), grid_spec=grid_spec,
        out_shape=jax.ShapeDtypeStruct((n_tok, SWA_WIDTH), BF16),
        compiler_params=_params(("parallel",)),
    )(jnp.asarray(flags), sink, qs, ks, ks, ks, pm, pm, pm)


NA_R = NA_KR // 2
NA_WIN = NA_R + NA_KR


def _na_bias_table(rpb):
    off = np.array([np.zeros(NA_R, int), np.arange(NA_R), np.full(NA_R, NA_R)])
    p_of = np.array([np.arange(NA_R), np.full(NA_R, NA_KR // 2), NA_KR // 2 + np.arange(NA_R)])
    j = np.arange(NA_WIN)[None, None, :] - off[:, :, None]
    row_ok = (j >= 0) & (j < NA_KR)
    dr = np.clip(j - p_of[:, :, None] + (NA_KR - 1), 0, 2 * NA_KR - 2)
    qc = np.arange(GRID_W)
    kc = np.arange(GRID_W)
    dc = np.clip(kc[None, :] - qc[:, None], -(NA_KW - 1), NA_KW - 1) + (NA_KW - 1)
    col_start = np.clip(qc - NA_KW // 2, 0, GRID_W - NA_KW)
    valid = (kc[None, :] >= col_start[:, None]) & (kc[None, :] < col_start[:, None] + NA_KW)
    hp = lax.Precision.HIGHEST
    sel_dc = jnp.asarray(np.eye(2 * NA_KW - 1, dtype=np.float32)[dc.reshape(-1)])
    sel_dr = jnp.asarray(np.eye(2 * NA_KR - 1, dtype=np.float32)[dr.reshape(-1)])
    cols = jnp.einsum('hab,xb->hax', rpb, sel_dc, precision=hp)
    bias = jnp.einsum('ya,hax->yhx', sel_dr, cols, precision=hp)
    bias = bias.reshape(3, NA_R, NA_WIN, NA_HEADS, GRID_W, GRID_W)
    ok = row_ok[:, :, :, None, None, None] & valid[None, None, None, None, :, :]
    bias = jnp.where(ok, bias, -jnp.inf)
    return jnp.transpose(bias, (0, 3, 1, 4, 2, 5)).reshape(3, NA_HEADS, NA_R * GRID_W, NA_WIN * GRID_W)


def _na_kernel(q_ref, k_ref, v_ref, bias_ref, o_ref, *, rows):
    first_row = jnp.clip(pl.program_id(1) * NA_R - NA_KR // 2, 0, rows - NA_WIN)
    keys = pl.ds(pl.multiple_of(first_row * GRID_W, GRID_W), NA_WIN * GRID_W)
    scale = HEAD_DIM ** -0.5
    for h in range(NA_HEADS):
        s = lax.dot_general(q_ref[:, _head_slice(h)], k_ref[keys, _head_slice(h)], (((1,), (1,)), ((), ())),
                            preferred_element_type=F32) * scale
        s = s + bias_ref[h]
        p = jnp.exp(s - jnp.max(s, axis=-1, keepdims=True))
        p = p / jnp.sum(p, axis=-1, keepdims=True)
        o = jnp.dot(p.astype(BF16), v_ref[keys, _head_slice(h)], preferred_element_type=F32)
        o_ref[:, _head_slice(h)] = o.astype(o_ref.dtype)


def _na_group(qn, kn, vn, bias, row0, b, l):
    rows = l // GRID_W
    nsteps = rows // NA_R
    tq = NA_R * GRID_W
    kind = lambda st: jnp.where(st == 0, 0, jnp.where(st == nsteps - 1, 2, 1))
    seq_spec = pl.BlockSpec((l, NA_WIDTH), lambda bi, st: (row0 // l + bi, 0))
    return pl.pallas_call(
        functools.partial(_na_kernel, rows=rows),
        out_shape=jax.ShapeDtypeStruct((b * l, NA_WIDTH), BF16),
        grid=(b, nsteps),
        in_specs=[pl.BlockSpec((tq, NA_WIDTH), lambda bi, st: (row0 // tq + bi * nsteps + st, 0)),
                  seq_spec, seq_spec,
                  pl.BlockSpec((None, NA_HEADS, tq, NA_WIN * GRID_W), lambda bi, st: (kind(st), 0, 0, 0))],
        out_specs=pl.BlockSpec((tq, NA_WIDTH), lambda bi, st: (bi * nsteps + st, 0)),
        compiler_params=_params(("parallel", "arbitrary")),
    )(qn, kn, vn, bias)


def _na(qn, kn, vn, rpb, groups):
    bias = _na_bias_table(rpb)
    return jnp.concatenate([_na_group(qn, kn, vn, bias, row0, b, l) for row0, b, l in groups], axis=0)


def _ssd_prep_kernel(flags_ref, xc_ref, xp_ref, xn_ref, dt_ref, w_ref, b_ref, dtb_ref,
                     xs_ref, bc_ref, dtf_ref, *, tr):
    fl = flags_ref[pl.program_id(0)]
    halo = 8
    xp = jnp.where((fl & 1) > 0, 0.0, xp_ref[...])
    xn = jnp.where((fl & 2) > 0, 0.0, xn_ref[...])
    ext = jnp.concatenate([xp, xc_ref[...], xn], axis=0)
    n = tr + 2 * halo
    acc = None
    for k in range(SSD_CONV):
        sh = (SSD_CONV // 2 - k) % n
        xk = ext if sh == 0 else pltpu.roll(ext, sh, 0)
        term = xk[halo:halo + tr] * w_ref[k:k + 1, :]
        acc = term if acc is None else acc + term
    acc = acc + b_ref[...]
    y = acc * jax.nn.sigmoid(acc)
    xs_ref[...] = y[:, :SSD_INNER]
    bc_ref[...] = y[:, SSD_INNER:]
    t = dt_ref[...] + dtb_ref[...]
    dtf_ref[...] = jnp.maximum(t, 0.0) + jnp.log1p(jnp.exp(-jnp.abs(t)))


def _dt_pad(v):
    z = jnp.zeros((LANES - SSD_HEADS,), v.dtype)
    return jnp.concatenate([v[0], z, v[1], z])


def _ssd_prep(pm, dt_raw, segs, lw):
    n_tok = pm.shape[0]
    tr = 256
    _, flags = _tile_tables(segs, tr)
    nb8 = tr // 8
    last8 = n_tok // 8 - 1
    ccb = COL_XBC // SSD_CONV_DIM
    w8 = jnp.concatenate([lw['ssd_conv_w'], jnp.zeros((8 - SSD_CONV, SSD_CONV_DIM), F32)], axis=0)
    dtb = _dt_pad(lw['ssd_dt_bias'])[None, :]
    grid_spec = pltpu.PrefetchScalarGridSpec(
        num_scalar_prefetch=1, grid=(n_tok // tr,),
        in_specs=[pl.BlockSpec((tr, SSD_CONV_DIM), lambda i, f: (i, ccb)),
                  pl.BlockSpec((8, SSD_CONV_DIM), lambda i, f: (jnp.maximum(i * nb8 - 1, 0), ccb)),
                  pl.BlockSpec((8, SSD_CONV_DIM), lambda i, f: (jnp.minimum((i + 1) * nb8, last8), ccb)),
                  pl.BlockSpec((tr, 2 * LANES), lambda i, f: (i, 0)),
                  pl.BlockSpec((8, SSD_CONV_DIM), lambda i, f: (0, 0)),
                  pl.BlockSpec((1, SSD_CONV_DIM), lambda i, f: (0, 0)),
                  pl.BlockSpec((1, 2 * LANES), lambda i, f: (0, 0))],
        out_specs=[pl.BlockSpec((tr, SSD_INNER), lambda i, f: (i, 0)),
                   pl.BlockSpec((tr, 2 * SSD_GROUPS * SSD_STATE), lambda i, f: (i, 0)),
                   pl.BlockSpec((tr, 2 * LANES), lambda i, f: (i, 0))])
    return pl.pallas_call(
        functools.partial(_ssd_prep_kernel, tr=tr), grid_spec=grid_spec,
        out_shape=[jax.ShapeDtypeStruct((n_tok, SSD_INNER), F32),
                   jax.ShapeDtypeStruct((n_tok, 2 * SSD_GROUPS * SSD_STATE), F32),
                   jax.ShapeDtypeStruct((n_tok, 2 * LANES), F32)],
        compiler_params=_params(("parallel",)),
    )(jnp.asarray(flags), pm, pm, pm, dt_raw, w8, lw['ssd_conv_b'][None, :], dtb)


def _split3(x):
    hi = x.astype(BF16)
    r = x - hi.astype(F32)
    mid = r.astype(BF16)
    lo = (r - mid.astype(F32)).astype(BF16)
    return hi, mid, lo


def _dot_sel_l(sel, x):
    hi, mid, lo = _split3(x)
    d = lambda t: jnp.dot(sel, t, preferred_element_type=F32)
    return d(lo) + d(mid) + d(hi)


def _dot_sel_r(x, sel):
    hi, mid, lo = _split3(x)
    d = lambda t: jnp.dot(t, sel, preferred_element_type=F32)
    return d(lo) + d(mid) + d(hi)


def _ssd_scan_kernel(order_ref, reset_ref, xs_ref, bc_ref, dt_ref, alog_ref, e_ref, *rest, bwd):
    if bwd:
        yf_ref, z_ref, dsk_ref, g_ref, o_ref, state_ref = rest
    else:
        o_ref, state_ref = rest
    del order_ref
    q = SSD_CHUNK
    n_pairs = SSD_HEADS // 2
    gs = SSD_GROUPS * SSD_STATE

    @pl.when(reset_ref[pl.program_id(0)] > 0)
    def _():
        state_ref[...] = jnp.zeros_like(state_ref)

    dt = dt_ref[...]
    da = dt * (-jnp.exp(alog_ref[...]))
    r_i = lax.broadcasted_iota(jnp.int32, (q, q), 0)
    c_i = lax.broadcasted_iota(jnp.int32, (q, q), 1)
    incl = (r_i <= c_i) if bwd else (r_i >= c_i)
    tri = jnp.where(incl, 1.0, 0.0).astype(BF16)
    cs = _dot_sel_l(tri, da)
    cs_t = cs.T
    e = e_ref[...]
    dt_x = _dot_sel_r(dt, e)
    cs_x = _dot_sel_r(cs, e)
    total = cs_x[0:1, :] if bwd else cs_x[q - 1:q, :]
    xs = xs_ref[...]
    xdt = xs * dt_x
    w_state = (xdt * jnp.exp(total - cs_x)).astype(BF16)
    xdt_b = xdt.astype(BF16)
    ecs_x = jnp.exp(cs_x)
    etot = jnp.exp(total)
    bc = bc_ref[...]
    b_t = bc[:, :gs].T
    lane = lax.broadcasted_iota(jnp.int32, (q, LANES), 1)
    ys = []
    for g in range(SSD_GROUPS):
        b_g = bc[:, g * SSD_STATE:(g + 1) * SSD_STATE].astype(BF16)
        c_g = bc[:, gs + g * SSD_STATE:gs + (g + 1) * SSD_STATE].astype(BF16)
        bt_g = b_t[g * SSD_STATE:(g + 1) * SSD_STATE, :].astype(BF16)
        cb = lax.dot_general(c_g, b_g, (((1,), (1,)), ((), ())), preferred_element_type=F32)
        for j in range(n_pairs // SSD_GROUPS):
            pair = g * (n_pairs // SSD_GROUPS) + j
            lanes = slice(pair * LANES, (pair + 1) * LANES)
            halves = []
            for hh in range(2):
                h = 2 * pair + hh
                diff = jnp.broadcast_to(cs[:, h:h + 1], (q, q)) - jnp.broadcast_to(cs_t[h:h + 1, :], (q, q))
                decay = jnp.where(incl, jnp.exp(diff), 0.0)
                halves.append(jnp.dot((cb * decay).astype(BF16), xdt_b[:, lanes], preferred_element_type=F32))
            y_diag = jnp.where(lane < SSD_HEAD_DIM, halves[0], halves[1])
            s_prev = state_ref[pair]
            y_off = jnp.dot(c_g, s_prev.astype(BF16), preferred_element_type=F32) * ecs_x[:, lanes]
            contrib = jnp.dot(bt_g, w_state[:, lanes], preferred_element_type=F32)
            state_ref[pair] = s_prev * etot[:, lanes] + contrib
            ys.append(y_diag + y_off)
    y = jnp.concatenate(ys, axis=1)
    if not bwd:
        o_ref[...] = y
        return
    y = yf_ref[...] + y + dsk_ref[...] * xs
    z = z_ref[...]
    y = y * (z * jax.nn.sigmoid(z))
    y = y * lax.rsqrt(jnp.mean(y * y, axis=-1, keepdims=True) + EPS)
    o_ref[...] = (y * g_ref[...]).astype(o_ref.dtype)


def _ssd_scan(xs, bc, dtf, alog, e_mat, segs, bwd, extra=()):
    n_tok = xs.shape[0]
    q = SSD_CHUNK
    _, flags = _tile_tables(segs, q)
    n = n_tok // q
    order = np.arange(n, dtype=np.int32)[::-1].copy() if bwd else np.arange(n, dtype=np.int32)
    reset = ((flags[order] & (2 if bwd else 1)) > 0).astype(np.int32)
    d = 1 if bwd else 0
    row = lambda w, cb=0: pl.BlockSpec((q, w), lambda i, o, r: (o[i], cb))
    const = lambda shape: pl.BlockSpec(shape, lambda i, o, r: (0,) * len(shape))
    in_specs = [row(SSD_INNER), row(2 * SSD_GROUPS * SSD_STATE), row(LANES, d),
                pl.BlockSpec((None, 1, LANES), lambda i, o, r: (d, 0, 0)), const((LANES, SSD_INNER))]
    if bwd:
        in_specs += [row(SSD_INNER), row(SSD_INNER, COL_Z // SSD_INNER), const((1, SSD_INNER)),
                     const((1, SSD_INNER))]
    grid_spec = pltpu.PrefetchScalarGridSpec(
        num_scalar_prefetch=2, grid=(n,), in_specs=in_specs, out_specs=row(SSD_INNER),
        scratch_shapes=[pltpu.VMEM((SSD_HEADS // 2, SSD_STATE, LANES), F32)])
    return pl.pallas_call(
        functools.partial(_ssd_scan_kernel, bwd=bwd), grid_spec=grid_spec,
        out_shape=jax.ShapeDtypeStruct((n_tok, SSD_INNER), BF16 if bwd else F32),
        compiler_params=_params(("arbitrary",)),
    )(jnp.asarray(order), jnp.asarray(reset), xs, bc, dtf, alog, e_mat, *extra)


def _ssd(pm, dt_raw, segs, lw):
    xs, bc, dtf = _ssd_prep(pm, dt_raw, segs, lw)
    alog = _dt_pad(lw['ssd_a_log']).reshape(2, 1, LANES)
    e_np = np.zeros((LANES, SSD_INNER), np.float32)
    for h in range(SSD_HEADS):
        e_np[h, h * SSD_HEAD_DIM:(h + 1) * SSD_HEAD_DIM] = 1.0
    e_mat = jnp.asarray(e_np, BF16)
    y_fwd = _ssd_scan(xs, bc, dtf, alog, e_mat, segs, False)
    dsk = jnp.repeat(lw['ssd_d'], SSD_HEAD_DIM)[None, :]
    return _ssd_scan(xs, bc, dtf, alog, e_mat, segs, True,
                     extra=(y_fwd, pm, dsk, lw['ssd_norm_g'][None, :]))


S5_Q = 16
S5_SB = LANES // S5_GROUP
S5_NSB = S5_GROUPS // S5_SB
S5_SBW = S5_Q * LANES
S5_SW = S5_SB * S5_STATE
S5_NS = S5_GROUPS * S5_STATE
S5_CW = S5_Q * S5_GROUP
assert 4 * S5_STATE == S5_CW


def _s5_weights(lw):
    hp = lax.Precision.HIGHEST
    qn, g, p, c = S5_Q, S5_GROUPS, S5_STATE, S5_GROUP
    nsb, sb = S5_NSB, S5_SB
    cmul = lambda a, b: (a[0] * b[0] - a[1] * b[1], a[0] * b[1] + a[1] * b[0])
    a_re, a_im = lw['s5_a_re'], lw['s5_a_im']
    step = jnp.exp(lw['s5_log_step'])[..., None]
    mag = jnp.exp(a_re * step)
    lam_bar = (mag * jnp.cos(a_im * step), mag * jnp.sin(a_im * step))
    den = a_re * a_re + a_im * a_im
    coef = cmul((lam_bar[0] - 1.0, lam_bar[1]), (a_re / den, -a_im / den))
    b_bar = cmul((coef[0][..., None], coef[1][..., None]),
                 (lw['s5_b_re'][None], lw['s5_b_im'][None]))
    c_c = (lw['s5_c_re'], lw['s5_c_im'])
    pows = [(jnp.ones_like(mag), jnp.zeros_like(mag))]
    for _ in range(qn):
        pows.append(cmul(pows[-1], lam_bar))
    pw = (jnp.stack([t[0] for t in pows], axis=1), jnp.stack([t[1] for t in pows], axis=1))

    def lag_kernels(d):
        t1 = cmul((c_c[0][d][None], c_c[1][d][None]),
                  (pw[0][d, :qn][:, :, None, :], pw[1][d, :qn][:, :, None, :]))
        return (jnp.einsum('mgip,gpj->mgij', t1[0], b_bar[0][d], precision=hp)
                - jnp.einsum('mgip,gpj->mgij', t1[1], b_bar[1][d], precision=hp))

    s_idx = np.arange(qn)[:, None]
    t_idx = np.arange(qn)[None, :]
    df = t_idx - s_idx
    kf = lag_kernels(0)[np.clip(df, 0, qn - 1)] * jnp.asarray(df >= 0, F32)[:, :, None, None, None]
    kb = lag_kernels(1)[np.clip(-df, 0, qn - 1)] * jnp.asarray(df <= 0, F32)[:, :, None, None, None]
    tt = jnp.transpose(kf + kb, (2, 0, 4, 1, 3))
    by_block_row = lambda a: jnp.transpose(a.reshape(nsb, sb, qn, c, S5_CW), (0, 2, 1, 3, 4)).reshape(
        nsb, S5_SBW, S5_CW)
    tt = by_block_row(tt)

    def in_op(d, powers):
        w = cmul((pw[0][d, powers][:, :, :, None], pw[1][d, powers][:, :, :, None]),
                 (b_bar[0][d][None], b_bar[1][d][None]))
        return tuple(jnp.transpose(t, (1, 0, 3, 2)) for t in w)

    wf = in_op(0, qn - 1 - np.arange(qn))
    wb = in_op(1, np.arange(qn))
    wa = by_block_row(jnp.stack([wf[0], wf[1], wb[0], wb[1]], axis=3))

    def out_op(d, powers):
        return cmul((jnp.transpose(c_c[0][d], (0, 2, 1))[:, :, None, :],
                     jnp.transpose(c_c[1][d], (0, 2, 1))[:, :, None, :]),
                    (jnp.transpose(pw[0][d, powers], (1, 2, 0))[:, :, :, None],
                     jnp.transpose(pw[1][d, powers], (1, 2, 0))[:, :, :, None]))

    vf = out_op(0, 1 + np.arange(qn))
    vb = out_op(1, qn - np.arange(qn))
    v4 = jnp.stack([vf[0], -vf[1], vb[0], -vb[1]], axis=0)
    vc = jnp.transpose(v4.reshape(4, nsb, sb * p, S5_CW), (1, 0, 2, 3)).reshape(nsb, 4 * S5_SW, S5_CW)

    lam_q = (pw[0][:, qn].reshape(2, 1, S5_NS), pw[1][:, qn].reshape(2, 1, S5_NS))
    dsk = jnp.broadcast_to(lw['s5_d'].reshape(nsb, 1, 1, LANES), (nsb, 1, qn, LANES)).reshape(nsb, 1, S5_SBW)
    return dict(tt=tt.astype(BF16), wa=wa.astype(BF16), vc=vc.astype(BF16),
                lam_re=lam_q[0], lam_im=lam_q[1], dsk=dsk)


S5_TT = 4096


def _s5_blocks(u_ref, mt):
    return jnp.concatenate([u_ref[pl.ds(t, mt, stride=S5_Q), :] for t in range(S5_Q)], axis=1)


def _s5_spread_matrix(inner):
    src = np.arange(S5_CW)
    dst = np.arange(S5_SBW)
    same = (src[:, None] // inner == dst[None, :] // (S5_SB * inner)) & (src[:, None] % inner == dst[None, :] % inner)
    return jnp.asarray(same, BF16)


def _s5_expand(c_ref, spread_ref, out_ref, row_inner, col_inner):
    rows = S5_CW
    col_g = (lax.broadcasted_iota(jnp.int32, (rows, S5_SBW), 1) // col_inner) % S5_SB
    row_l = lax.broadcasted_iota(jnp.int32, (rows, S5_SBW), 0)
    for r0 in range(0, S5_SBW, rows):
        row_g = ((row_l + r0) // row_inner) % S5_SB
        full = jnp.dot(c_ref[r0:r0 + rows, :], spread_ref[...], preferred_element_type=F32)
        out_ref[r0:r0 + rows, :] = jnp.where(row_g == col_g, full, 0.0).astype(out_ref.dtype)


def _s5_in_kernel(u_ref, wc_ref, spread_ref, fre_ref, fim_ref, bre_ref, bim_ref, w_ref, *, mt):
    @pl.when(pl.program_id(1) == 0)
    def _():
        _s5_expand(wc_ref, spread_ref, w_ref, S5_GROUP, S5_STATE)

    r = jnp.dot(_s5_blocks(u_ref, mt).astype(BF16), w_ref[...], preferred_element_type=F32)
    fre_ref[...] = r[:, 0 * S5_SW:1 * S5_SW]
    fim_ref[...] = r[:, 1 * S5_SW:2 * S5_SW]
    bre_ref[...] = r[:, 2 * S5_SW:3 * S5_SW]
    bim_ref[...] = r[:, 3 * S5_SW:4 * S5_SW]


def _s5_in(pm, wa):
    n_tok = pm.shape[0]
    tt = min(S5_TT, n_tok)
    mt = tt // S5_Q
    ucb = COL_U_S5 // LANES
    st = pl.BlockSpec((mt, S5_SW), lambda k, i: (i, k))
    sds = jax.ShapeDtypeStruct((n_tok // S5_Q, S5_NS), F32)
    return pl.pallas_call(
        functools.partial(_s5_in_kernel, mt=mt), out_shape=[sds] * 4, grid=(S5_NSB, n_tok // tt),
        in_specs=[pl.BlockSpec((tt, LANES), lambda k, i: (i, ucb + k)),
                  pl.BlockSpec((None, S5_SBW, S5_CW), lambda k, i: (k, 0, 0)),
                  pl.BlockSpec((S5_CW, S5_SBW), lambda k, i: (0, 0))],
        out_specs=[st] * 4,
        scratch_shapes=[pltpu.VMEM((S5_SBW, 4 * S5_SW), BF16)],
        compiler_params=_params(("parallel", "arbitrary")),
    )(pm, wa, _s5_spread_matrix(S5_STATE))


def _s5_rec_kernel(flags_ref, fre_ref, fim_ref, bre_ref, bim_ref, lam_re_ref, lam_im_ref,
                   xre_ref, xim_ref, zre_ref, zim_ref, state_ref, *, tc):
    i = pl.program_id(0)
    n = pl.num_programs(0)

    @pl.when((flags_ref[i] & 1) > 0)
    def _():
        state_ref[0:2] = jnp.zeros((2, 1, S5_NS), F32)

    @pl.when((flags_ref[n - 1 - i] & 2) > 0)
    def _():
        state_ref[2:4] = jnp.zeros((2, 1, S5_NS), F32)

    flr, fli = lam_re_ref[0], lam_im_ref[0]
    blr, bli = lam_re_ref[1], lam_im_ref[1]

    def body(k, carry):
        xr, xi, zr, zi = carry
        cf = pl.ds(k, 1)
        cb = pl.ds(tc - 1 - k, 1)
        xre_ref[cf, :] = xr
        xim_ref[cf, :] = xi
        zre_ref[cb, :] = zr
        zim_ref[cb, :] = zi
        return (flr * xr - fli * xi + fre_ref[cf, :], flr * xi + fli * xr + fim_ref[cf, :],
                blr * zr - bli * zi + bre_ref[cb, :], blr * zi + bli * zr + bim_ref[cb, :])

    out = lax.fori_loop(0, tc, body, tuple(state_ref[j] for j in range(4)))
    for j in range(4):
        state_ref[j] = out[j]


def _s5_rec(fre, fim, bre, bim, lam_re, lam_im, segs):
    m = fre.shape[0]
    tc = min(128, min(l for _, l in segs) // S5_Q)
    _, flags = _tile_tables(segs, tc * S5_Q)
    n = m // tc
    fwd = pl.BlockSpec((tc, S5_NS), lambda i, f: (i, 0))
    bwd = pl.BlockSpec((tc, S5_NS), lambda i, f: (n - 1 - i, 0))
    lam = pl.BlockSpec((2, 1, S5_NS), lambda i, f: (0, 0, 0))
    grid_spec = pltpu.PrefetchScalarGridSpec(
        num_scalar_prefetch=1, grid=(n,), in_specs=[fwd, fwd, bwd, bwd, lam, lam],
        out_specs=[fwd, fwd, bwd, bwd],
        scratch_shapes=[pltpu.VMEM((4, 1, S5_NS), F32)])
    sds = jax.ShapeDtypeStruct((m, S5_NS), F32)
    return pl.pallas_call(
        functools.partial(_s5_rec_kernel, tc=tc), grid_spec=grid_spec, out_shape=[sds] * 4,
        compiler_params=_params(("arbitrary",)),
    )(jnp.asarray(flags), fre, fim, bre, bim, lam_re, lam_im)


def _s5_out_kernel(u_ref, ttc_ref, xre_ref, xim_ref, zre_ref, zim_ref, vcc_ref, dsk_ref, spread_ref, y_ref,
                   tt_ref, vc_ref, *, mt):
    @pl.when(pl.program_id(1) == 0)
    def _():
        _s5_expand(ttc_ref, spread_ref, tt_ref, S5_GROUP, S5_GROUP)
        _s5_expand(vcc_ref, spread_ref, vc_ref, S5_STATE, S5_GROUP)

    u = _s5_blocks(u_ref, mt)
    intra = jnp.dot(u.astype(BF16), tt_ref[...], preferred_element_type=F32)
    st = jnp.concatenate([xre_ref[...], xim_ref[...], zre_ref[...], zim_ref[...]], axis=1).astype(BF16)
    carry = jnp.dot(st, vc_ref[...], preferred_element_type=F32)
    y = intra + carry + dsk_ref[...] * u
    for t in range(S5_Q):
        y_ref[pl.ds(t, mt, stride=S5_Q), :] = y[:, t * LANES:(t + 1) * LANES]


def _s5_out(pm, w, xre, xim, zre, zim):
    n_tok = pm.shape[0]
    tt = min(S5_TT, n_tok)
    mt = tt // S5_Q
    ucb = COL_U_S5 // LANES
    st = pl.BlockSpec((mt, S5_SW), lambda k, i: (i, k))
    return pl.pallas_call(
        functools.partial(_s5_out_kernel, mt=mt),
        out_shape=jax.ShapeDtypeStruct((n_tok, S5_WIDTH), F32), grid=(S5_NSB, n_tok // tt),
        in_specs=[pl.BlockSpec((tt, LANES), lambda k, i: (i, ucb + k)),
                  pl.BlockSpec((None, S5_SBW, S5_CW), lambda k, i: (k, 0, 0)),
                  st, st, st, st,
                  pl.BlockSpec((None, 4 * S5_SW, S5_CW), lambda k, i: (k, 0, 0)),
                  pl.BlockSpec((None, 1, S5_SBW), lambda k, i: (k, 0, 0)),
                  pl.BlockSpec((S5_CW, S5_SBW), lambda k, i: (0, 0))],
        out_specs=pl.BlockSpec((tt, LANES), lambda k, i: (i, k)),
        scratch_shapes=[pltpu.VMEM((S5_SBW, S5_SBW), BF16), pltpu.VMEM((4 * S5_SW, S5_SBW), BF16)],
        compiler_params=_params(("parallel", "arbitrary")),
    )(pm, w['tt'], xre, xim, zre, zim, w['vc'], w['dsk'], _s5_spread_matrix(S5_GROUP))


def _s5_glu_kernel(y_ref, w_ref, b_ref, o_ref):
    g = jax.nn.gelu(y_ref[...])
    t = jnp.dot(g.astype(BF16), w_ref[...], preferred_element_type=F32) + b_ref[...]
    o_ref[...] = (g * jax.nn.sigmoid(t)).astype(o_ref.dtype)


def _s5_glu(y, glu_w, layer, glu_b):
    n_tok = y.shape[0]
    tr = 512
    return pl.pallas_call(
        _s5_glu_kernel, out_shape=jax.ShapeDtypeStruct((n_tok, S5_WIDTH), BF16), grid=(n_tok // tr,),
        in_specs=[pl.BlockSpec((tr, S5_WIDTH), lambda i: (i, 0)),
                  pl.BlockSpec((None, S5_WIDTH, S5_WIDTH), lambda i: (layer, 0, 0)),
                  pl.BlockSpec((1, S5_WIDTH), lambda i: (0, 0))],
        out_specs=pl.BlockSpec((tr, S5_WIDTH), lambda i: (i, 0)),
        compiler_params=_params(("parallel",)),
    )(y, glu_w, glu_b[None, :])


def _s5(pm, segs, lw, glu_w, layer):
    w = _s5_weights(lw)
    fre, fim, bre, bim = _s5_in(pm, w['wa'])
    xre, xim, zre, zim = _s5_rec(fre, fim, bre, bim, w['lam_re'], w['lam_im'], segs)
    y = _s5_out(pm, w, xre, xim, zre, zim)
    return _s5_glu(y, glu_w, layer, lw['s5_glu_b'])


def _matmul_weights(w_in, w_branch_ssd, w_branch_swa, w_branch_s5, w_branch_na, w_out, ffn_w1, ffn_w3, ffn_w2,
                    s5_glu_w):
    o = IN_OFFS
    piece = lambda k: w_in[:, :, o[k]:o[k + 1]]
    w_mix = jnp.concatenate([piece(0), piece(3), piece(6), piece(7), piece(8), piece(9),
                             piece(1), piece(4), piece(5)], axis=2).astype(BF16)
    zpad = jnp.zeros((DEPTH, D_MODEL, LANES - SSD_HEADS), F32)
    w_dt = jnp.concatenate([piece(2)[:, :, :SSD_HEADS], zpad, piece(2)[:, :, SSD_HEADS:], zpad],
                           axis=2).astype(BF16)
    w_gate = jnp.transpose(piece(10).reshape(DEPTH, D_MODEL, N_BRANCH, D_MODEL), (0, 2, 1, 3)).astype(BF16)
    w_branch = jnp.stack([w_branch_ssd, w_branch_swa, w_branch_s5, w_branch_na], axis=1).astype(BF16)
    return dict(mix=w_mix, dt=w_dt, gate=w_gate, branch=w_branch, out=w_out.astype(BF16),
                w1=ffn_w1.astype(BF16), w3=ffn_w3.astype(BF16), w2=ffn_w2.astype(BF16),
                glu=s5_glu_w.astype(BF16))


def _layer(x, c8, lw, mw, ada_w, ada_b, layer):
    mod = _ada(c8, ada_w, ada_b, layer)
    mod4 = mod.reshape(8, 6, 1, D_MODEL)

    h = _norm_mod(x, lw['norm1_g'], mod4, 1, 0)
    pm = _mm(h, mw['mix'], layer, 1024, F32)
    dt_raw = _mm(h, mw['dt'], layer, 2 * LANES, F32)

    y_ssd = _ssd(pm, dt_raw, SEGS, lw)
    qs, ks, qn, kn, vn = _qk_prep(pm, SEGS, lw)
    y_swa = _swa(qs, ks, pm, lw['swa_sink'], SEGS)
    y_s5 = _s5(pm, SEGS, lw, mw['glu'], layer)
    y_na = _na(qn, kn, vn, lw['na_rpb'], GROUPS)
    merged = _merge(h, mw['gate'], jnp.stack([y_ssd, y_swa, y_s5, y_na]), mw['branch'], layer)
    x = _mm_res(merged, mw['out'], layer, x, mod4, 2, 1024, D_MODEL)

    h2 = _norm_mod(x, lw['norm2_g'], mod4, 4, 3)
    u = _ffn_up(h2, mw['w1'], mw['w3'], layer)
    return _mm_res(u, mw['w2'], layer, x, mod4, 5, 512, D_FF // 2)


_LAYER_KEYS = ('ada_w', 'ada_b', 'norm1_g', 'norm2_g', 'w_in',
               'ssd_conv_w', 'ssd_conv_b', 'ssd_dt_bias', 'ssd_a_log', 'ssd_d', 'ssd_norm_g',
               'swa_q_norm_g', 'swa_k_norm_g', 'swa_sink',
               's5_a_re', 's5_a_im', 's5_log_step', 's5_b_re', 's5_b_im', 's5_c_re', 's5_c_im',
               's5_d', 's5_glu_w', 's5_glu_b',
               'na_q_norm_g', 'na_k_norm_g', 'na_rpb',
               'w_branch_ssd', 'w_branch_swa', 'w_branch_s5', 'w_branch_na', 'w_out',
               'ffn_w1', 'ffn_w3', 'ffn_w2')


def kernel(x_prompt, x_sample, c_prompt, c_sample, ada_w, ada_b, norm1_g, norm2_g, w_in, ssd_conv_w, ssd_conv_b, ssd_dt_bias, ssd_a_log, ssd_d, ssd_norm_g, swa_q_norm_g, swa_k_norm_g, swa_sink, s5_a_re, s5_a_im, s5_log_step, s5_b_re, s5_b_im, s5_c_re, s5_c_im, s5_d, s5_glu_w, s5_glu_b, na_q_norm_g, na_k_norm_g, na_rpb, w_branch_ssd, w_branch_swa, w_branch_s5, w_branch_na, w_out, ffn_w1, ffn_w3, ffn_w2):
    stacked = dict(zip(_LAYER_KEYS, (ada_w, ada_b, norm1_g, norm2_g, w_in,
                                     ssd_conv_w, ssd_conv_b, ssd_dt_bias, ssd_a_log, ssd_d, ssd_norm_g,
                                     swa_q_norm_g, swa_k_norm_g, swa_sink,
                                     s5_a_re, s5_a_im, s5_log_step, s5_b_re, s5_b_im, s5_c_re, s5_c_im,
                                     s5_d, s5_glu_w, s5_glu_b,
                                     na_q_norm_g, na_k_norm_g, na_rpb,
                                     w_branch_ssd, w_branch_swa, w_branch_s5, w_branch_na, w_out,
                                     ffn_w1, ffn_w3, ffn_w2)))
    x = jnp.concatenate([x_prompt.reshape(BATCH * SEQ, D_MODEL),
                         x_sample.reshape(DEC_BATCH * DEC_SEQ, D_MODEL)], axis=0)
    c8 = jnp.concatenate([c_prompt, c_sample, jnp.zeros((8 - N_SEQS, D_MODEL), F32)], axis=0)
    mw = _matmul_weights(w_in, w_branch_ssd, w_branch_swa, w_branch_s5, w_branch_na, w_out, ffn_w1, ffn_w3, ffn_w2,
                         s5_glu_w)
    for i in range(DEPTH):
        lw = {k: v[i] for k, v in stacked.items() if v.size < D_MODEL * D_MODEL}
        x = _layer(x, c8, lw, mw, ada_w, ada_b, i)
    n_p = BATCH * SEQ
    return (x[:n_p].reshape(BATCH, SEQ, D_MODEL), x[n_p:].reshape(DEC_BATCH, DEC_SEQ, D_MODEL))
```

```python
import functools

import jax
import jax.numpy as jnp
import numpy as np
from jax import lax
from jax.experimental import pallas as pl
from jax.experimental.pallas import tpu as pltpu

D_MODEL = 4096
BATCH = 2
SEQ = 4096
DEPTH = 2
DEC_BATCH = 4
DEC_SEQ = 2048
N_TOK = BATCH * SEQ + DEC_BATCH * DEC_SEQ
N_SEQS = BATCH + DEC_BATCH
GROUPS = ((0, BATCH, SEQ), (BATCH * SEQ, DEC_BATCH, DEC_SEQ))
SEGS = tuple((row0 + i * l, l) for row0, b, l in GROUPS for i in range(b))

EPS = 1e-6
HEAD_DIM = 128
N_BRANCH = 4
SSD_HEADS = 16
SSD_HEAD_DIM = 64
SSD_INNER = SSD_HEADS * SSD_HEAD_DIM
SSD_GROUPS = 2
SSD_STATE = 128
SSD_CONV = 5
SSD_CHUNK = 128
SSD_CONV_DIM = SSD_INNER + 2 * SSD_GROUPS * SSD_STATE
SWA_HEADS = 8
SWA_KV_HEADS = 2
SWA_WIDTH = SWA_HEADS * HEAD_DIM
SWA_KV_WIDTH = SWA_KV_HEADS * HEAD_DIM
SWA_WINDOW = 128
SWA_BLOCK = 128
ROPE_THETA = 10000.0
S5_WIDTH = 1024
S5_GROUP = 16
S5_GROUPS = S5_WIDTH // S5_GROUP
S5_STATE = 64
NA_HEADS = 8
NA_WIDTH = NA_HEADS * HEAD_DIM
GRID_W = 64
NA_KR = 8
NA_KW = 16
D_FF = ((8 * D_MODEL + 3 * 256 - 1) // (3 * 256)) * 256
IN_SIZES = (SSD_INNER, SSD_CONV_DIM, 2 * SSD_HEADS,
            SWA_WIDTH, SWA_KV_WIDTH, SWA_KV_WIDTH,
            S5_WIDTH,
            NA_WIDTH, NA_WIDTH, NA_WIDTH,
            N_BRANCH * D_MODEL)
IN_OFFS = tuple(int(v) for v in np.cumsum((0,) + IN_SIZES))

LANES = 128
TM = 1024
N_MIX = IN_OFFS[10] - IN_SIZES[2]
VMEM_LIMIT = 56 * 1024 * 1024

COL_Z = 0
COL_XBC = COL_Z + SSD_INNER
COL_Q_SWA = COL_XBC + SSD_CONV_DIM
COL_K_SWA = COL_Q_SWA + SWA_WIDTH
COL_V_SWA = COL_K_SWA + SWA_KV_WIDTH
COL_U_S5 = COL_V_SWA + SWA_KV_WIDTH
COL_Q_NA = COL_U_S5 + S5_WIDTH
COL_K_NA = COL_Q_NA + NA_WIDTH
COL_V_NA = COL_K_NA + NA_WIDTH
COL_SSD_END = COL_XBC + SSD_CONV_DIM

F32 = jnp.float32
BF16 = jnp.bfloat16


def _batch_of_tile(i, tile):
    n_p = (BATCH * SEQ) // tile
    return jnp.where(i < n_p, i // (SEQ // tile), BATCH + (i - n_p) // (DEC_SEQ // tile))


def _params(sem):
    return pltpu.CompilerParams(dimension_semantics=sem, vmem_limit_bytes=VMEM_LIMIT)


def _ada_kernel(c_ref, w_ref, b_ref, o_ref):
    c = c_ref[...]
    a = (c * jax.nn.sigmoid(c)).astype(BF16)
    o_ref[...] = jnp.dot(a, w_ref[...].astype(BF16), preferred_element_type=F32) + b_ref[...]


def _ada(c8, ada_w, ada_b, layer):
    n = ada_w.shape[2]
    tn = 512
    return pl.pallas_call(
        _ada_kernel,
        out_shape=jax.ShapeDtypeStruct((8, n), F32),
        grid=(n // tn,),
        in_specs=[pl.BlockSpec((8, D_MODEL), lambda j: (0, 0)),
                  pl.BlockSpec((None, D_MODEL, tn), lambda j: (layer, 0, j)),
                  pl.BlockSpec((1, tn), lambda j: (0, j))],
        out_specs=pl.BlockSpec((8, tn), lambda j: (0, j)),
        compiler_params=_params(("parallel",)),
    )(c8, ada_w, ada_b[layer].reshape(1, n))


def _norm_mod_kernel(*refs, n_first):
    *x_refs, g_ref, scale_ref, shift_ref, o_ref = refs

    def body(x_ref):
        x = x_ref[...]
        y = x * lax.rsqrt(jnp.mean(x * x, axis=-1, keepdims=True) + EPS)
        y = y * g_ref[...]
        o_ref[...] = (y * (1.0 + scale_ref[...]) + shift_ref[...]).astype(o_ref.dtype)

    if len(x_refs) == 1:
        body(x_refs[0])
        return
    i = pl.program_id(0)
    pl.when(i < n_first)(lambda: body(x_refs[0]))
    pl.when(i >= n_first)(lambda: body(x_refs[1]))


def _group_specs(block, n_first):
    def first(i, j=0, *_):
        return (jnp.minimum(i, n_first - 1), jnp.where(i < n_first, j, 0))

    def second(i, j=0, *_):
        return (jnp.maximum(i - n_first, 0), jnp.where(i >= n_first, j, 0))

    return [pl.BlockSpec(block, first), pl.BlockSpec(block, second)]


def _norm_mod(xs, g, mod4, scale_idx, shift_idx):
    tr = 256
    n_first = xs[0].shape[0] // tr
    x_specs = ([pl.BlockSpec((tr, D_MODEL), lambda i: (i, 0))] if len(xs) == 1
               else _group_specs((tr, D_MODEL), n_first))
    return pl.pallas_call(
        functools.partial(_norm_mod_kernel, n_first=n_first),
        out_shape=jax.ShapeDtypeStruct((N_TOK, D_MODEL), BF16),
        grid=(N_TOK // tr,),
        in_specs=x_specs + [
            pl.BlockSpec((1, D_MODEL), lambda i: (0, 0)),
            pl.BlockSpec((None, None, 1, D_MODEL), lambda i: (_batch_of_tile(i, tr), scale_idx, 0, 0)),
            pl.BlockSpec((None, None, 1, D_MODEL), lambda i: (_batch_of_tile(i, tr), shift_idx, 0, 0))],
        out_specs=pl.BlockSpec((tr, D_MODEL), lambda i: (i, 0)),
        compiler_params=_params(("parallel",)),
    )(*xs, g.reshape(1, D_MODEL), mod4, mod4)


def _mm_kernel(a_ref, b_ref, o_ref):
    o_ref[...] = jnp.dot(a_ref[...], b_ref[...], preferred_element_type=F32).astype(o_ref.dtype)


def _mm(a, b, layer, n, tn, out_dtype):
    m, k = a.shape
    return pl.pallas_call(
        _mm_kernel,
        out_shape=jax.ShapeDtypeStruct((m, n), out_dtype),
        grid=(m // TM, n // tn),
        in_specs=[pl.BlockSpec((TM, k), lambda i, j: (i, 0)),
                  pl.BlockSpec((None, k, tn), lambda i, j: (layer, 0, j))],
        out_specs=pl.BlockSpec((TM, tn), lambda i, j: (i, j)),
        compiler_params=_params(("parallel", "parallel")),
    )(a, b)


def _merge_kernel(h_ref, wg_ref, m_ref, wb_ref, o_ref, acc_ref):
    b = pl.program_id(2)
    logits = jnp.dot(h_ref[...], wg_ref[...], preferred_element_type=F32)
    y = jnp.dot(m_ref[...], wb_ref[...], preferred_element_type=F32)
    contrib = jax.nn.sigmoid(logits) * y

    @pl.when(b == 0)
    def _():
        acc_ref[...] = contrib

    @pl.when(b > 0)
    def _():
        acc_ref[...] += contrib

    @pl.when(b == N_BRANCH - 1)
    def _():
        o_ref[...] = acc_ref[...].astype(o_ref.dtype)


def _merge(h, w_main, mix, wb, layer):
    tn = 512
    w = mix.shape[2]
    gate0 = N_MIX // tn
    per_branch = D_MODEL // tn
    return pl.pallas_call(
        _merge_kernel,
        out_shape=jax.ShapeDtypeStruct((N_TOK, D_MODEL), BF16),
        grid=(N_TOK // TM, D_MODEL // tn, N_BRANCH),
        in_specs=[pl.BlockSpec((TM, D_MODEL), lambda i, j, b: (i, 0)),
                  pl.BlockSpec((None, D_MODEL, tn), lambda i, j, b: (layer, 0, gate0 + b * per_branch + j)),
                  pl.BlockSpec((None, TM, w), lambda i, j, b: (b, i, 0)),
                  pl.BlockSpec((None, None, w, tn), lambda i, j, b: (layer, b, 0, j))],
        out_specs=pl.BlockSpec((TM, tn), lambda i, j, b: (i, j)),
        scratch_shapes=[pltpu.VMEM((TM, tn), F32)],
        compiler_params=_params(("parallel", "parallel", "arbitrary")),
    )(h, w_main, mix, wb)


def _mm_res_kernel(a_ref, b_ref, *rest, nk, n_first):
    *x_refs, gate_ref, o_ref, acc_ref = rest
    d = jnp.dot(a_ref[...], b_ref[...], preferred_element_type=F32)

    def finish(total):
        def store(x_ref):
            o_ref[...] = x_ref[...] + gate_ref[...] * total

        if len(x_refs) == 1:
            store(x_refs[0])
            return
        i = pl.program_id(0)
        pl.when(i < n_first)(functools.partial(store, x_refs[0]))
        pl.when(i >= n_first)(functools.partial(store, x_refs[1]))

    if nk == 1:
        finish(d)
        return
    k = pl.program_id(2)

    @pl.when(k == 0)
    def _():
        acc_ref[...] = d

    @pl.when(k > 0)
    def _():
        acc_ref[...] += d

    @pl.when(k == nk - 1)
    def _():
        finish(acc_ref[...])


def _mm_res(a, b, layer, xs, mod4, gate_idx, tn, tk, rows=None):
    kdim = a.shape[1]
    n = b.shape[2]
    nk = kdim // tk
    off, m_tiles = rows if rows is not None else (0, a.shape[0] // TM)
    n_first = xs[0].shape[0] // TM
    x_specs = ([pl.BlockSpec((TM, tn), lambda i, j, k: (i + off, j))] if len(xs) == 1
               else _group_specs((TM, tn), n_first))
    return pl.pallas_call(
        functools.partial(_mm_res_kernel, nk=nk, n_first=n_first),
        out_shape=jax.ShapeDtypeStruct((m_tiles * TM, n), F32),
        grid=(m_tiles, n // tn, nk),
        in_specs=[pl.BlockSpec((TM, tk), lambda i, j, k: (i + off, k)),
                  pl.BlockSpec((None, tk, tn), lambda i, j, k: (layer, k, j))] + x_specs + [
                  pl.BlockSpec((None, None, 1, tn),
                               lambda i, j, k: (_batch_of_tile(i + off, TM), gate_idx, 0, j))],
        out_specs=pl.BlockSpec((TM, tn), lambda i, j, k: (i, j)),
        scratch_shapes=[pltpu.VMEM((TM, tn), F32)],
        compiler_params=_params(("parallel", "parallel", "arbitrary")),
    )(a, b, *xs, mod4)


def _ffn_up_kernel(h_ref, w1_ref, w3_ref, o_ref):
    h = h_ref[...]
    a = jnp.dot(h, w1_ref[...], preferred_element_type=F32)
    b = jnp.dot(h, w3_ref[...], preferred_element_type=F32)
    o_ref[...] = (a * jax.nn.sigmoid(a) * b).astype(o_ref.dtype)


def _ffn_up(h, w1, w3, layer):
    tn = 512
    n = w1.shape[2]
    return pl.pallas_call(
        _ffn_up_kernel,
        out_shape=jax.ShapeDtypeStruct((N_TOK, n), BF16),
        grid=(N_TOK // TM, pl.cdiv(n, tn)),
        in_specs=[pl.BlockSpec((TM, D_MODEL), lambda i, j: (i, 0)),
                  pl.BlockSpec((None, D_MODEL, tn), lambda i, j: (layer, 0, j)),
                  pl.BlockSpec((None, D_MODEL, tn), lambda i, j: (layer, 0, j))],
        out_specs=pl.BlockSpec((TM, tn), lambda i, j: (i, j)),
        compiler_params=_params(("parallel", "parallel")),
    )(h, w1, w3)


def _tile_tables(segs, tile):
    pos, flags = [], []
    for _, length in segs:
        n = length // tile
        for j in range(n):
            pos.append(j)
            flags.append((1 if j == 0 else 0) | (2 if j == n - 1 else 0))
    return np.asarray(pos, np.int32), np.asarray(flags, np.int32)


def _head_slice(h):
    return slice(h * HEAD_DIM, (h + 1) * HEAD_DIM)


def _qk_prep_kernel(pos_ref, qs_lo_ref, qs_hi_ref, ks_ref, qn_ref, kn_ref, vn_ref, cos_ref, sin_ref, g_ref,
                    oqs_ref, oks_ref, oqn_ref, okn_ref, ovn_ref):
    del pos_ref
    half_heads = SWA_HEADS // 2
    cos = cos_ref[...]
    sin = sin_ref[...]

    def hnorm(x, g):
        return x * lax.rsqrt(jnp.mean(x * x, axis=-1, keepdims=True) + EPS) * g

    def rope(x):
        return x * cos + pltpu.roll(x, HEAD_DIM // 2, 1) * sin

    for h in range(SWA_HEADS):
        src = qs_lo_ref if h < half_heads else qs_hi_ref
        oqs_ref[:, _head_slice(h)] = rope(hnorm(src[:, _head_slice(h % half_heads)], g_ref[0:1, :])).astype(BF16)
    for h in range(SWA_KV_HEADS):
        oks_ref[:, _head_slice(h)] = rope(hnorm(ks_ref[:, _head_slice(h)], g_ref[1:2, :])).astype(BF16)
    for h in range(NA_HEADS):
        oqn_ref[:, _head_slice(h)] = hnorm(qn_ref[:, _head_slice(h)], g_ref[2:3, :]).astype(BF16)
        okn_ref[:, _head_slice(h)] = hnorm(kn_ref[:, _head_slice(h)], g_ref[3:4, :]).astype(BF16)
    ovn_ref[...] = vn_ref[...].astype(BF16)


def _rope_tables(max_len):
    half = HEAD_DIM // 2
    inv_freq = ROPE_THETA ** (-jnp.arange(half, dtype=F32) / half)
    ang = jnp.arange(max_len, dtype=F32)[:, None] * inv_freq[None, :]
    cos, sin = jnp.cos(ang), jnp.sin(ang)
    return jnp.concatenate([cos, cos], axis=1), jnp.concatenate([-sin, sin], axis=1)


def _qk_prep(pm, segs, lw):
    n_tok = pm.shape[0]
    tr = 256
    pos, _ = _tile_tables(segs, tr)
    cos, sin = _rope_tables(max(l for _, l in segs))
    gains = jnp.concatenate([lw['swa_q_norm_g'][None], lw['swa_k_norm_g'][None],
                             lw['na_q_norm_g'][None], lw['na_k_norm_g'][None],
                             jnp.zeros((4, HEAD_DIM), F32)], axis=0)
    wide = lambda cb: pl.BlockSpec((tr, 1024), lambda i, p: (i, cb))
    narrow = lambda cb: pl.BlockSpec((tr, 256), lambda i, p: (i, cb))
    grid_spec = pltpu.PrefetchScalarGridSpec(
        num_scalar_prefetch=1, grid=(n_tok // tr,),
        in_specs=[pl.BlockSpec((tr, 512), lambda i, p: (i, COL_Q_SWA // 512)),
                  pl.BlockSpec((tr, 512), lambda i, p: (i, COL_Q_SWA // 512 + 1)),
                  narrow(COL_K_SWA // 256), wide(COL_Q_NA // 1024),
                  wide(COL_K_NA // 1024), wide(COL_V_NA // 1024),
                  pl.BlockSpec((tr, HEAD_DIM), lambda i, p: (p[i], 0)),
                  pl.BlockSpec((tr, HEAD_DIM), lambda i, p: (p[i], 0)),
                  pl.BlockSpec((8, HEAD_DIM), lambda i, p: (0, 0))],
        out_specs=[pl.BlockSpec((tr, 1024), lambda i, p: (i, 0)),
                   pl.BlockSpec((tr, 256), lambda i, p: (i, 0)),
                   pl.BlockSpec((tr, 1024), lambda i, p: (i, 0)),
                   pl.BlockSpec((tr, 1024), lambda i, p: (i, 0)),
                   pl.BlockSpec((tr, 1024), lambda i, p: (i, 0))])
    sds = lambda w: jax.ShapeDtypeStruct((n_tok, w), BF16)
    return pl.pallas_call(
        _qk_prep_kernel, grid_spec=grid_spec,
        out_shape=[sds(1024), sds(256), sds(1024), sds(1024), sds(1024)],
        compiler_params=_params(("parallel",)),
    )(jnp.asarray(pos), pm, pm, pm, pm, pm, pm, cos, sin, gains)


SWA_TQ = 512
SWA_GRP = SWA_HEADS // SWA_KV_HEADS


def _swa_kernel(flags_ref, sink_ref, q_ref, kc_ref, kp_ref, kn_ref, vc_ref, vp_ref, vn_ref, o_ref, *, tq):
    fl = flags_ref[pl.program_id(0)]
    lo = jnp.where((fl & 1) > 0, SWA_BLOCK, 0)
    hi = jnp.where((fl & 2) > 0, 2 * SWA_BLOCK, 3 * SWA_BLOCK)
    nqb = tq // SWA_BLOCK
    m_rows = SWA_GRP * SWA_BLOCK
    row = lax.broadcasted_iota(jnp.int32, (m_rows, 3 * SWA_BLOCK), 0) & (SWA_BLOCK - 1)
    col = lax.broadcasted_iota(jnp.int32, (m_rows, 3 * SWA_BLOCK), 1)
    band = (col >= row) & (col <= row + 2 * SWA_WINDOW)
    scale = HEAD_DIM ** -0.5
    for g in range(SWA_KV_HEADS):
        ks = _head_slice(g)
        k_ext = jnp.concatenate([kp_ref[:, ks], kc_ref[:, ks], kn_ref[:, ks]], axis=0)
        v_ext = jnp.concatenate([vp_ref[:, ks], vc_ref[:, ks], vn_ref[:, ks]], axis=0).astype(BF16)
        sk = jnp.concatenate([jnp.full((SWA_BLOCK, 1), sink_ref[g * SWA_GRP + h], F32)
                              for h in range(SWA_GRP)], axis=0)
        for qb in range(nqb):
            rows = slice(qb * SWA_BLOCK, (qb + 1) * SWA_BLOCK)
            q = jnp.concatenate([q_ref[rows, _head_slice(g * SWA_GRP + h)] for h in range(SWA_GRP)], axis=0)
            keys = k_ext[qb * SWA_BLOCK:(qb + 3) * SWA_BLOCK]
            vals = v_ext[qb * SWA_BLOCK:(qb + 3) * SWA_BLOCK]
            s = lax.dot_general(q, keys, (((1,), (1,)), ((), ())), preferred_element_type=F32) * scale
            mask = band
            if qb == 0:
                mask = mask & (col >= lo)
            if qb == nqb - 1:
                mask = mask & (col < hi)
            s = jnp.where(mask, s, -jnp.inf)
            m = jnp.maximum(jnp.max(s, axis=-1, keepdims=True), sk)
            p = jnp.exp(s - m)
            denom = jnp.sum(p, axis=-1, keepdims=True) + jnp.exp(sk - m)
            o = jnp.dot((p / denom).astype(BF16), vals, preferred_element_type=F32)
            for h in range(SWA_GRP):
                o_ref[rows, _head_slice(g * SWA_GRP + h)] = o[h * SWA_BLOCK:(h + 1) * SWA_BLOCK].astype(o_ref.dtype)


def _swa(qs, ks, pm, sink, segs):
    n_tok = qs.shape[0]
    tq = min(SWA_TQ, min(l for _, l in segs))
    _, flags = _tile_tables(segs, tq)
    nb = tq // SWA_BLOCK
    last_blk = n_tok // SWA_BLOCK - 1
    prev_map = lambda cb: (lambda i, f: (jnp.maximum(i * nb - 1, 0), cb))
    next_map = lambda cb: (lambda i, f: (jnp.minimum((i + 1) * nb, last_blk), cb))
    vcb = COL_V_SWA // SWA_KV_WIDTH
    grid_spec = pltpu.PrefetchScalarGridSpec(
        num_scalar_prefetch=1, grid=(n_tok // tq,),
        in_specs=[pl.BlockSpec(memory_space=pltpu.SMEM),
                  pl.BlockSpec((tq, SWA_WIDTH), lambda i, f: (i, 0)),
                  pl.BlockSpec((tq, SWA_KV_WIDTH), lambda i, f: (i, 0)),
                  pl.BlockSpec((SWA_BLOCK, SWA_KV_WIDTH), prev_map(0)),
                  pl.BlockSpec((SWA_BLOCK, SWA_KV_WIDTH), next_map(0)),
                  pl.BlockSpec((tq, SWA_KV_WIDTH), lambda i, f: (i, vcb)),
                  pl.BlockSpec((SWA_BLOCK, SWA_KV_WIDTH), prev_map(vcb)),
                  pl.BlockSpec((SWA_BLOCK, SWA_KV_WIDTH), next_map(vcb))],
        out_specs=pl.BlockSpec((tq, SWA_WIDTH), lambda i, f: (i, 0)))
    return pl.pallas_call(
        functools.partial(_swa_kernel, tq=tq), grid_spec=grid_spec,
        out_shape=jax.ShapeDtypeStruct((n_tok, SWA_WIDTH), BF16),
        compiler_params=_params(("parallel",)),
    )(jnp.asarray(flags), sink, qs, ks, ks, ks, pm, pm, pm)


NA_R = NA_KR // 2
NA_WIN = NA_R + NA_KR


def _na_bias_table(rpb):
    off = np.array([np.zeros(NA_R, int), np.arange(NA_R), np.full(NA_R, NA_R)])
    p_of = np.array([np.arange(NA_R), np.full(NA_R, NA_KR // 2), NA_KR // 2 + np.arange(NA_R)])
    j = np.arange(NA_WIN)[None, None, :] - off[:, :, None]
    row_ok = (j >= 0) & (j < NA_KR)
    dr = np.clip(j - p_of[:, :, None] + (NA_KR - 1), 0, 2 * NA_KR - 2)
    qc = np.arange(GRID_W)
    kc = np.arange(GRID_W)
    dc = np.clip(kc[None, :] - qc[:, None], -(NA_KW - 1), NA_KW - 1) + (NA_KW - 1)
    col_start = np.clip(qc - NA_KW // 2, 0, GRID_W - NA_KW)
    valid = (kc[None, :] >= col_start[:, None]) & (kc[None, :] < col_start[:, None] + NA_KW)
    hp = lax.Precision.HIGHEST
    sel_dc = jnp.asarray(np.eye(2 * NA_KW - 1, dtype=np.float32)[dc.reshape(-1)])
    sel_dr = jnp.asarray(np.eye(2 * NA_KR - 1, dtype=np.float32)[dr.reshape(-1)])
    cols = jnp.einsum('hab,xb->hax', rpb, sel_dc, precision=hp)
    bias = jnp.einsum('ya,hax->yhx', sel_dr, cols, precision=hp)
    bias = bias.reshape(3, NA_R, NA_WIN, NA_HEADS, GRID_W, GRID_W)
    ok = row_ok[:, :, :, None, None, None] & valid[None, None, None, None, :, :]
    bias = jnp.where(ok, bias, -jnp.inf)
    return jnp.transpose(bias, (0, 3, 1, 4, 2, 5)).reshape(3, NA_HEADS, NA_R * GRID_W, NA_WIN * GRID_W)


def _na_kernel(q_ref, k_ref, v_ref, bias_ref, o_ref, *, rows):
    first_row = jnp.clip(pl.program_id(1) * NA_R - NA_KR // 2, 0, rows - NA_WIN)
    keys = pl.ds(pl.multiple_of(first_row * GRID_W, GRID_W), NA_WIN * GRID_W)
    scale = HEAD_DIM ** -0.5
    for h in range(NA_HEADS):
        s = lax.dot_general(q_ref[:, _head_slice(h)], k_ref[keys, _head_slice(h)], (((1,), (1,)), ((), ())),
                            preferred_element_type=F32) * scale
        s = s + bias_ref[h]
        p = jnp.exp(s - jnp.max(s, axis=-1, keepdims=True))
        p = p / jnp.sum(p, axis=-1, keepdims=True)
        o = jnp.dot(p.astype(BF16), v_ref[keys, _head_slice(h)], preferred_element_type=F32)
        o_ref[:, _head_slice(h)] = o.astype(o_ref.dtype)


def _na_group(qn, kn, vn, bias, row0, b, l):
    rows = l // GRID_W
    nsteps = rows // NA_R
    tq = NA_R * GRID_W
    kind = lambda st: jnp.where(st == 0, 0, jnp.where(st == nsteps - 1, 2, 1))
    seq_spec = pl.BlockSpec((l, NA_WIDTH), lambda bi, st: (row0 // l + bi, 0))
    return pl.pallas_call(
        functools.partial(_na_kernel, rows=rows),
        out_shape=jax.ShapeDtypeStruct((b * l, NA_WIDTH), BF16),
        grid=(b, nsteps),
        in_specs=[pl.BlockSpec((tq, NA_WIDTH), lambda bi, st: (row0 // tq + bi * nsteps + st, 0)),
                  seq_spec, seq_spec,
                  pl.BlockSpec((None, NA_HEADS, tq, NA_WIN * GRID_W), lambda bi, st: (kind(st), 0, 0, 0))],
        out_specs=pl.BlockSpec((tq, NA_WIDTH), lambda bi, st: (bi * nsteps + st, 0)),
        compiler_params=_params(("parallel", "arbitrary")),
    )(qn, kn, vn, bias)


def _na(qn, kn, vn, rpb, groups):
    bias = _na_bias_table(rpb)
    return jnp.concatenate([_na_group(qn, kn, vn, bias, row0, b, l) for row0, b, l in groups], axis=0)


def _ssd_prep_kernel(flags_ref, xc_ref, xp_ref, xn_ref, dt_ref, w_ref, b_ref, dtb_ref,
                     xs_ref, bc_ref, dtf_ref, *, tr):
    fl = flags_ref[pl.program_id(0)]
    halo = 8
    xbc = slice(COL_XBC, COL_SSD_END)
    xp = jnp.where((fl & 1) > 0, 0.0, xp_ref[:, xbc])
    xn = jnp.where((fl & 2) > 0, 0.0, xn_ref[:, xbc])
    ext = jnp.concatenate([xp, xc_ref[:, xbc], xn], axis=0)
    n = tr + 2 * halo
    acc = None
    for k in range(SSD_CONV):
        sh = (SSD_CONV // 2 - k) % n
        xk = ext if sh == 0 else pltpu.roll(ext, sh, 0)
        term = xk[halo:halo + tr] * w_ref[k:k + 1, :]
        acc = term if acc is None else acc + term
    acc = acc + b_ref[...]
    y = acc * jax.nn.sigmoid(acc)
    xs_ref[...] = y[:, :SSD_INNER]
    bc_ref[...] = y[:, SSD_INNER:]
    t = dt_ref[...] + dtb_ref[...]
    dtf_ref[...] = jnp.maximum(t, 0.0) + jnp.log1p(jnp.exp(-jnp.abs(t)))


def _dt_pad(v):
    z = jnp.zeros((LANES - SSD_HEADS,), v.dtype)
    return jnp.concatenate([v[0], z, v[1], z])


def _ssd_prep(pm, dt_raw, segs, lw):
    n_tok = pm.shape[0]
    tr = 256
    _, flags = _tile_tables(segs, tr)
    nb8 = tr // 8
    last8 = n_tok // 8 - 1
    w8 = jnp.concatenate([lw['ssd_conv_w'], jnp.zeros((8 - SSD_CONV, SSD_CONV_DIM), F32)], axis=0)
    dtb = _dt_pad(lw['ssd_dt_bias'])[None, :]
    grid_spec = pltpu.PrefetchScalarGridSpec(
        num_scalar_prefetch=1, grid=(n_tok // tr,),
        in_specs=[pl.BlockSpec((tr, COL_SSD_END), lambda i, f: (i, 0)),
                  pl.BlockSpec((8, COL_SSD_END), lambda i, f: (jnp.maximum(i * nb8 - 1, 0), 0)),
                  pl.BlockSpec((8, COL_SSD_END), lambda i, f: (jnp.minimum((i + 1) * nb8, last8), 0)),
                  pl.BlockSpec((tr, 2 * LANES), lambda i, f: (i, 0)),
                  pl.BlockSpec((8, SSD_CONV_DIM), lambda i, f: (0, 0)),
                  pl.BlockSpec((1, SSD_CONV_DIM), lambda i, f: (0, 0)),
                  pl.BlockSpec((1, 2 * LANES), lambda i, f: (0, 0))],
        out_specs=[pl.BlockSpec((tr, SSD_INNER), lambda i, f: (i, 0)),
                   pl.BlockSpec((tr, 2 * SSD_GROUPS * SSD_STATE), lambda i, f: (i, 0)),
                   pl.BlockSpec((tr, 2 * LANES), lambda i, f: (i, 0))])
    return pl.pallas_call(
        functools.partial(_ssd_prep_kernel, tr=tr), grid_spec=grid_spec,
        out_shape=[jax.ShapeDtypeStruct((n_tok, SSD_INNER), F32),
                   jax.ShapeDtypeStruct((n_tok, 2 * SSD_GROUPS * SSD_STATE), F32),
                   jax.ShapeDtypeStruct((n_tok, 2 * LANES), F32)],
        compiler_params=_params(("parallel",)),
    )(jnp.asarray(flags), pm, pm, pm, dt_raw, w8, lw['ssd_conv_b'][None, :], dtb)


def _split3(x):
    hi = x.astype(BF16)
    r = x - hi.astype(F32)
    mid = r.astype(BF16)
    lo = (r - mid.astype(F32)).astype(BF16)
    return hi, mid, lo


def _dot_sel_l(sel, x):
    hi, mid, lo = _split3(x)
    d = lambda t: jnp.dot(sel, t, preferred_element_type=F32)
    return d(lo) + d(mid) + d(hi)


def _dot_sel_r(x, sel):
    hi, mid, lo = _split3(x)
    d = lambda t: jnp.dot(t, sel, preferred_element_type=F32)
    return d(lo) + d(mid) + d(hi)


def _ssd_scan_kernel(order_ref, reset_ref, xs_ref, bc_ref, dt_ref, alog_ref, e_ref, *rest, bwd):
    if bwd:
        yf_ref, z_ref, dsk_ref, g_ref, o_ref, state_ref = rest
    else:
        o_ref, state_ref = rest
    del order_ref
    q = SSD_CHUNK
    n_pairs = SSD_HEADS // 2
    gs = SSD_GROUPS * SSD_STATE

    @pl.when(reset_ref[pl.program_id(0)] > 0)
    def _():
        state_ref[...] = jnp.zeros_like(state_ref)

    dt = dt_ref[...]
    da = dt * (-jnp.exp(alog_ref[...]))
    r_i = lax.broadcasted_iota(jnp.int32, (q, q), 0)
    c_i = lax.broadcasted_iota(jnp.int32, (q, q), 1)
    incl = (r_i <= c_i) if bwd else (r_i >= c_i)
    tri = jnp.where(incl, 1.0, 0.0).astype(BF16)
    cs = _dot_sel_l(tri, da)
    cs_t = cs.T
    e = e_ref[...]
    dt_x = _dot_sel_r(dt, e)
    cs_x = _dot_sel_r(cs, e)
    total = cs_x[0:1, :] if bwd else cs_x[q - 1:q, :]
    xs = xs_ref[...]
    xdt = xs * dt_x
    w_state = (xdt * jnp.exp(total - cs_x)).astype(BF16)
    xdt_b = xdt.astype(BF16)
    ecs_x = jnp.exp(cs_x)
    etot = jnp.exp(total)
    bc = bc_ref[...]
    b_t = bc[:, :gs].T
    lane = lax.broadcasted_iota(jnp.int32, (q, LANES), 1)
    ys = []
    for g in range(SSD_GROUPS):
        b_g = bc[:, g * SSD_STATE:(g + 1) * SSD_STATE].astype(BF16)
        c_g = bc[:, gs + g * SSD_STATE:gs + (g + 1) * SSD_STATE].astype(BF16)
        bt_g = b_t[g * SSD_STATE:(g + 1) * SSD_STATE, :].astype(BF16)
        cb = lax.dot_general(c_g, b_g, (((1,), (1,)), ((), ())), preferred_element_type=F32)
        for j in range(n_pairs // SSD_GROUPS):
            pair = g * (n_pairs // SSD_GROUPS) + j
            lanes = slice(pair * LANES, (pair + 1) * LANES)
            halves = []
            for hh in range(2):
                h = 2 * pair + hh
                diff = jnp.broadcast_to(cs[:, h:h + 1], (q, q)) - jnp.broadcast_to(cs_t[h:h + 1, :], (q, q))
                decay = jnp.where(incl, jnp.exp(diff), 0.0)
                halves.append(jnp.dot((cb * decay).astype(BF16), xdt_b[:, lanes], preferred_element_type=F32))
            y_diag = jnp.where(lane < SSD_HEAD_DIM, halves[0], halves[1])
            s_prev = state_ref[pair]
            y_off = jnp.dot(c_g, s_prev.astype(BF16), preferred_element_type=F32) * ecs_x[:, lanes]
            contrib = jnp.dot(bt_g, w_state[:, lanes], preferred_element_type=F32)
            state_ref[pair] = s_prev * etot[:, lanes] + contrib
            ys.append(y_diag + y_off)
    y = jnp.concatenate(ys, axis=1)
    if not bwd:
        o_ref[...] = y
        return
    y = yf_ref[...] + y + dsk_ref[...] * xs
    z = z_ref[...]
    y = y * (z * jax.nn.sigmoid(z))
    y = y * lax.rsqrt(jnp.mean(y * y, axis=-1, keepdims=True) + EPS)
    o_ref[...] = (y * g_ref[...]).astype(o_ref.dtype)


def _ssd_scan(xs, bc, dtf, alog, e_mat, segs, bwd, extra=()):
    n_tok = xs.shape[0]
    q = SSD_CHUNK
    _, flags = _tile_tables(segs, q)
    n = n_tok // q
    order = np.arange(n, dtype=np.int32)[::-1].copy() if bwd else np.arange(n, dtype=np.int32)
    reset = ((flags[order] & (2 if bwd else 1)) > 0).astype(np.int32)
    d = 1 if bwd else 0
    row = lambda w, cb=0: pl.BlockSpec((q, w), lambda i, o, r: (o[i], cb))
    const = lambda shape: pl.BlockSpec(shape, lambda i, o, r: (0,) * len(shape))
    in_specs = [row(SSD_INNER), row(2 * SSD_GROUPS * SSD_STATE), row(LANES, d),
                pl.BlockSpec((None, 1, LANES), lambda i, o, r: (d, 0, 0)), const((LANES, SSD_INNER))]
    if bwd:
        in_specs += [row(SSD_INNER), row(SSD_INNER, COL_Z // SSD_INNER), const((1, SSD_INNER)),
                     const((1, SSD_INNER))]
    grid_spec = pltpu.PrefetchScalarGridSpec(
        num_scalar_prefetch=2, grid=(n,), in_specs=in_specs, out_specs=row(SSD_INNER),
        scratch_shapes=[pltpu.VMEM((SSD_HEADS // 2, SSD_STATE, LANES), F32)])
    return pl.pallas_call(
        functools.partial(_ssd_scan_kernel, bwd=bwd), grid_spec=grid_spec,
        out_shape=jax.ShapeDtypeStruct((n_tok, SSD_INNER), BF16 if bwd else F32),
        compiler_params=_params(("arbitrary",)),
    )(jnp.asarray(order), jnp.asarray(reset), xs, bc, dtf, alog, e_mat, *extra)


def _ssd(pm, dt_raw, segs, lw):
    xs, bc, dtf = _ssd_prep(pm, dt_raw, segs, lw)
    alog = _dt_pad(lw['ssd_a_log']).reshape(2, 1, LANES)
    e_np = np.zeros((LANES, SSD_INNER), np.float32)
    for h in range(SSD_HEADS):
        e_np[h, h * SSD_HEAD_DIM:(h + 1) * SSD_HEAD_DIM] = 1.0
    e_mat = jnp.asarray(e_np, BF16)
    y_fwd = _ssd_scan(xs, bc, dtf, alog, e_mat, segs, False)
    dsk = jnp.repeat(lw['ssd_d'], SSD_HEAD_DIM)[None, :]
    return _ssd_scan(xs, bc, dtf, alog, e_mat, segs, True,
                     extra=(y_fwd, pm, dsk, lw['ssd_norm_g'][None, :]))


S5_Q = 16
S5_SB = LANES // S5_GROUP
S5_NSB = S5_GROUPS // S5_SB
S5_SBW = S5_Q * LANES
S5_SW = S5_SB * S5_STATE
S5_NS = S5_GROUPS * S5_STATE
S5_CW = S5_Q * S5_GROUP
assert 4 * S5_STATE == S5_CW


def _s5_weights(lw):
    hp = lax.Precision.HIGHEST
    qn, g, p, c = S5_Q, S5_GROUPS, S5_STATE, S5_GROUP
    nsb, sb = S5_NSB, S5_SB
    cmul = lambda a, b: (a[0] * b[0] - a[1] * b[1], a[0] * b[1] + a[1] * b[0])
    a_re, a_im = lw['s5_a_re'], lw['s5_a_im']
    step = jnp.exp(lw['s5_log_step'])[..., None]
    mag = jnp.exp(a_re * step)
    lam_bar = (mag * jnp.cos(a_im * step), mag * jnp.sin(a_im * step))
    den = a_re * a_re + a_im * a_im
    coef = cmul((lam_bar[0] - 1.0, lam_bar[1]), (a_re / den, -a_im / den))
    b_bar = cmul((coef[0][..., None], coef[1][..., None]),
                 (lw['s5_b_re'][None], lw['s5_b_im'][None]))
    c_c = (lw['s5_c_re'], lw['s5_c_im'])
    pows = [(jnp.ones_like(mag), jnp.zeros_like(mag))]
    for _ in range(qn):
        pows.append(cmul(pows[-1], lam_bar))
    pw = (jnp.stack([t[0] for t in pows], axis=1), jnp.stack([t[1] for t in pows], axis=1))

    def lag_kernels(d):
        t1 = cmul((c_c[0][d][None], c_c[1][d][None]),
                  (pw[0][d, :qn][:, :, None, :], pw[1][d, :qn][:, :, None, :]))
        return (jnp.einsum('mgip,gpj->mgij', t1[0], b_bar[0][d], precision=hp)
                - jnp.einsum('mgip,gpj->mgij', t1[1], b_bar[1][d], precision=hp))

    s_idx = np.arange(qn)[:, None]
    t_idx = np.arange(qn)[None, :]
    df = t_idx - s_idx
    kf = lag_kernels(0)[np.clip(df, 0, qn - 1)] * jnp.asarray(df >= 0, F32)[:, :, None, None, None]
    kb = lag_kernels(1)[np.clip(-df, 0, qn - 1)] * jnp.asarray(df <= 0, F32)[:, :, None, None, None]
    tt = jnp.transpose(kf + kb, (2, 0, 4, 1, 3))
    by_block_row = lambda a: jnp.transpose(a.reshape(nsb, sb, qn, c, S5_CW), (0, 2, 1, 3, 4)).reshape(
        nsb, S5_SBW, S5_CW)
    tt = by_block_row(tt)

    def in_op(d, powers):
        w = cmul((pw[0][d, powers][:, :, :, None], pw[1][d, powers][:, :, :, None]),
                 (b_bar[0][d][None], b_bar[1][d][None]))
        return tuple(jnp.transpose(t, (1, 0, 3, 2)) for t in w)

    wf = in_op(0, qn - 1 - np.arange(qn))
    wb = in_op(1, np.arange(qn))
    wa = by_block_row(jnp.stack([wf[0], wf[1], wb[0], wb[1]], axis=3))

    def out_op(d, powers):
        return cmul((jnp.transpose(c_c[0][d], (0, 2, 1))[:, :, None, :],
                     jnp.transpose(c_c[1][d], (0, 2, 1))[:, :, None, :]),
                    (jnp.transpose(pw[0][d, powers], (1, 2, 0))[:, :, :, None],
                     jnp.transpose(pw[1][d, powers], (1, 2, 0))[:, :, :, None]))

    vf = out_op(0, 1 + np.arange(qn))
    vb = out_op(1, qn - np.arange(qn))
    v4 = jnp.stack([vf[0], -vf[1], vb[0], -vb[1]], axis=0)
    vc = jnp.transpose(v4.reshape(4, nsb, sb * p, S5_CW), (1, 0, 2, 3)).reshape(nsb, 4 * S5_SW, S5_CW)

    lam_q = (pw[0][:, qn].reshape(2, 1, S5_NS), pw[1][:, qn].reshape(2, 1, S5_NS))
    dsk = jnp.broadcast_to(lw['s5_d'].reshape(nsb, 1, 1, LANES), (nsb, 1, qn, LANES)).reshape(nsb, 1, S5_SBW)
    return dict(tt=tt.astype(BF16), wa=wa.astype(BF16), vc=vc.astype(BF16),
                lam_re=lam_q[0], lam_im=lam_q[1], dsk=dsk)


S5_TT = 4096


def _s5_blocks(u_ref, mt):
    return jnp.concatenate([u_ref[pl.ds(t, mt, stride=S5_Q), :] for t in range(S5_Q)], axis=1)


def _s5_spread_matrix(inner):
    src = np.arange(S5_CW)
    dst = np.arange(S5_SBW)
    same = (src[:, None] // inner == dst[None, :] // (S5_SB * inner)) & (src[:, None] % inner == dst[None, :] % inner)
    return jnp.asarray(same, BF16)


def _s5_expand(c_ref, spread_ref, out_ref, row_inner, col_inner):
    rows = S5_CW
    col_g = (lax.broadcasted_iota(jnp.int32, (rows, S5_SBW), 1) // col_inner) % S5_SB
    row_l = lax.broadcasted_iota(jnp.int32, (rows, S5_SBW), 0)
    for r0 in range(0, S5_SBW, rows):
        row_g = ((row_l + r0) // row_inner) % S5_SB
        full = jnp.dot(c_ref[r0:r0 + rows, :], spread_ref[...], preferred_element_type=F32)
        out_ref[r0:r0 + rows, :] = jnp.where(row_g == col_g, full, 0.0).astype(out_ref.dtype)


def _s5_in_kernel(u_ref, wc_ref, spread_ref, fre_ref, fim_ref, bre_ref, bim_ref, w_ref, *, mt):
    @pl.when(pl.program_id(1) == 0)
    def _():
        _s5_expand(wc_ref, spread_ref, w_ref, S5_GROUP, S5_STATE)

    r = jnp.dot(_s5_blocks(u_ref, mt).astype(BF16), w_ref[...], preferred_element_type=F32)
    fre_ref[...] = r[:, 0 * S5_SW:1 * S5_SW]
    fim_ref[...] = r[:, 1 * S5_SW:2 * S5_SW]
    bre_ref[...] = r[:, 2 * S5_SW:3 * S5_SW]
    bim_ref[...] = r[:, 3 * S5_SW:4 * S5_SW]


def _s5_in(pm, wa):
    n_tok = pm.shape[0]
    tt = min(S5_TT, n_tok)
    mt = tt // S5_Q
    ucb = COL_U_S5 // LANES
    st = pl.BlockSpec((mt, S5_SW), lambda k, i: (i, k))
    sds = jax.ShapeDtypeStruct((n_tok // S5_Q, S5_NS), F32)
    return pl.pallas_call(
        functools.partial(_s5_in_kernel, mt=mt), out_shape=[sds] * 4, grid=(S5_NSB, n_tok // tt),
        in_specs=[pl.BlockSpec((tt, LANES), lambda k, i: (i, ucb + k)),
                  pl.BlockSpec((None, S5_SBW, S5_CW), lambda k, i: (k, 0, 0)),
                  pl.BlockSpec((S5_CW, S5_SBW), lambda k, i: (0, 0))],
        out_specs=[st] * 4,
        scratch_shapes=[pltpu.VMEM((S5_SBW, 4 * S5_SW), BF16)],
        compiler_params=_params(("parallel", "arbitrary")),
    )(pm, wa, _s5_spread_matrix(S5_STATE))


def _s5_rec_kernel(flags_ref, fre_ref, fim_ref, bre_ref, bim_ref, lam_re_ref, lam_im_ref,
                   xre_ref, xim_ref, zre_ref, zim_ref, state_ref, *, tc):
    i = pl.program_id(0)
    n = pl.num_programs(0)

    @pl.when((flags_ref[i] & 1) > 0)
    def _():
        state_ref[0:2] = jnp.zeros((2, 1, S5_NS), F32)

    @pl.when((flags_ref[n - 1 - i] & 2) > 0)
    def _():
        state_ref[2:4] = jnp.zeros((2, 1, S5_NS), F32)

    flr, fli = lam_re_ref[0], lam_im_ref[0]
    blr, bli = lam_re_ref[1], lam_im_ref[1]

    def body(k, carry):
        xr, xi, zr, zi = carry
        cf = pl.ds(k, 1)
        cb = pl.ds(tc - 1 - k, 1)
        xre_ref[cf, :] = xr
        xim_ref[cf, :] = xi
        zre_ref[cb, :] = zr
        zim_ref[cb, :] = zi
        return (flr * xr - fli * xi + fre_ref[cf, :], flr * xi + fli * xr + fim_ref[cf, :],
                blr * zr - bli * zi + bre_ref[cb, :], blr * zi + bli * zr + bim_ref[cb, :])

    out = lax.fori_loop(0, tc, body, tuple(state_ref[j] for j in range(4)))
    for j in range(4):
        state_ref[j] = out[j]


def _s5_rec(fre, fim, bre, bim, lam_re, lam_im, segs):
    m = fre.shape[0]
    tc = min(128, min(l for _, l in segs) // S5_Q)
    _, flags = _tile_tables(segs, tc * S5_Q)
    n = m // tc
    fwd = pl.BlockSpec((tc, S5_NS), lambda i, f: (i, 0))
    bwd = pl.BlockSpec((tc, S5_NS), lambda i, f: (n - 1 - i, 0))
    lam = pl.BlockSpec((2, 1, S5_NS), lambda i, f: (0, 0, 0))
    grid_spec = pltpu.PrefetchScalarGridSpec(
        num_scalar_prefetch=1, grid=(n,), in_specs=[fwd, fwd, bwd, bwd, lam, lam],
        out_specs=[fwd, fwd, bwd, bwd],
        scratch_shapes=[pltpu.VMEM((4, 1, S5_NS), F32)])
    sds = jax.ShapeDtypeStruct((m, S5_NS), F32)
    return pl.pallas_call(
        functools.partial(_s5_rec_kernel, tc=tc), grid_spec=grid_spec, out_shape=[sds] * 4,
        compiler_params=_params(("arbitrary",)),
    )(jnp.asarray(flags), fre, fim, bre, bim, lam_re, lam_im)


def _s5_out_kernel(u_ref, ttc_ref, xre_ref, xim_ref, zre_ref, zim_ref, vcc_ref, dsk_ref, spread_ref, y_ref,
                   tt_ref, vc_ref, *, mt):
    @pl.when(pl.program_id(1) == 0)
    def _():
        _s5_expand(ttc_ref, spread_ref, tt_ref, S5_GROUP, S5_GROUP)
        _s5_expand(vcc_ref, spread_ref, vc_ref, S5_STATE, S5_GROUP)

    u = _s5_blocks(u_ref, mt)
    intra = jnp.dot(u.astype(BF16), tt_ref[...], preferred_element_type=F32)
    st = jnp.concatenate([xre_ref[...], xim_ref[...], zre_ref[...], zim_ref[...]], axis=1).astype(BF16)
    carry = jnp.dot(st, vc_ref[...], preferred_element_type=F32)
    y = intra + carry + dsk_ref[...] * u
    for t in range(S5_Q):
        y_ref[pl.ds(t, mt, stride=S5_Q), :] = y[:, t * LANES:(t + 1) * LANES]


def _s5_out(pm, w, xre, xim, zre, zim):
    n_tok = pm.shape[0]
    tt = min(S5_TT, n_tok)
    mt = tt // S5_Q
    ucb = COL_U_S5 // LANES
    st = pl.BlockSpec((mt, S5_SW), lambda k, i: (i, k))
    return pl.pallas_call(
        functools.partial(_s5_out_kernel, mt=mt),
        out_shape=jax.ShapeDtypeStruct((n_tok, S5_WIDTH), F32), grid=(S5_NSB, n_tok // tt),
        in_specs=[pl.BlockSpec((tt, LANES), lambda k, i: (i, ucb + k)),
                  pl.BlockSpec((None, S5_SBW, S5_CW), lambda k, i: (k, 0, 0)),
                  st, st, st, st,
                  pl.BlockSpec((None, 4 * S5_SW, S5_CW), lambda k, i: (k, 0, 0)),
                  pl.BlockSpec((None, 1, S5_SBW), lambda k, i: (k, 0, 0)),
                  pl.BlockSpec((S5_CW, S5_SBW), lambda k, i: (0, 0))],
        out_specs=pl.BlockSpec((tt, LANES), lambda k, i: (i, k)),
        scratch_shapes=[pltpu.VMEM((S5_SBW, S5_SBW), BF16), pltpu.VMEM((4 * S5_SW, S5_SBW), BF16)],
        compiler_params=_params(("parallel", "arbitrary")),
    )(pm, w['tt'], xre, xim, zre, zim, w['vc'], w['dsk'], _s5_spread_matrix(S5_GROUP))


def _s5_glu_kernel(y_ref, w_ref, b_ref, o_ref):
    g = jax.nn.gelu(y_ref[...])
    t = jnp.dot(g.astype(BF16), w_ref[...], preferred_element_type=F32) + b_ref[...]
    o_ref[...] = (g * jax.nn.sigmoid(t)).astype(o_ref.dtype)


def _s5_glu(y, glu_w, layer, glu_b):
    n_tok = y.shape[0]
    tr = 512
    return pl.pallas_call(
        _s5_glu_kernel, out_shape=jax.ShapeDtypeStruct((n_tok, S5_WIDTH), BF16), grid=(n_tok // tr,),
        in_specs=[pl.BlockSpec((tr, S5_WIDTH), lambda i: (i, 0)),
                  pl.BlockSpec((None, S5_WIDTH, S5_WIDTH), lambda i: (layer, 0, 0)),
                  pl.BlockSpec((1, S5_WIDTH), lambda i: (0, 0))],
        out_specs=pl.BlockSpec((tr, S5_WIDTH), lambda i: (i, 0)),
        compiler_params=_params(("parallel",)),
    )(y, glu_w, glu_b[None, :])


def _s5(pm, segs, lw, glu_w, layer):
    w = _s5_weights(lw)
    fre, fim, bre, bim = _s5_in(pm, w['wa'])
    xre, xim, zre, zim = _s5_rec(fre, fim, bre, bim, w['lam_re'], w['lam_im'], segs)
    y = _s5_out(pm, w, xre, xim, zre, zim)
    return _s5_glu(y, glu_w, layer, lw['s5_glu_b'])


def _matmul_weights(w_in, w_branch_ssd, w_branch_swa, w_branch_s5, w_branch_na, w_out, ffn_w1, ffn_w3, ffn_w2,
                    s5_glu_w):
    o = IN_OFFS
    w_main = jnp.concatenate([w_in[:, :, :o[2]], w_in[:, :, o[3]:]], axis=2).astype(BF16)
    w_dt_raw = w_in[:, :, o[2]:o[3]]
    zpad = jnp.zeros((DEPTH, D_MODEL, LANES - SSD_HEADS), F32)
    w_dt = jnp.concatenate([w_dt_raw[:, :, :SSD_HEADS], zpad, w_dt_raw[:, :, SSD_HEADS:], zpad],
                           axis=2).astype(BF16)
    w_branch = jnp.stack([w_branch_ssd, w_branch_swa, w_branch_s5, w_branch_na], axis=1).astype(BF16)
    return dict(main=w_main, dt=w_dt, branch=w_branch, out=w_out.astype(BF16),
                w1=ffn_w1.astype(BF16), w3=ffn_w3.astype(BF16), w2=ffn_w2.astype(BF16),
                glu=s5_glu_w.astype(BF16))


def _layer(xs, c8, lw, mw, ada_w, ada_b, layer, split_output):
    mod = _ada(c8, ada_w, ada_b, layer)
    mod4 = mod.reshape(8, 6, 1, D_MODEL)

    h = _norm_mod(xs, lw['norm1_g'], mod4, 1, 0)
    pm = _mm(h, mw['main'], layer, N_MIX, 1024, F32)
    dt_raw = _mm(h, mw['dt'], layer, 2 * LANES, 2 * LANES, F32)

    y_ssd = _ssd(pm, dt_raw, SEGS, lw)
    qs, ks, qn, kn, vn = _qk_prep(pm, SEGS, lw)
    y_swa = _swa(qs, ks, pm, lw['swa_sink'], SEGS)
    y_s5 = _s5(pm, SEGS, lw, mw['glu'], layer)
    y_na = _na(qn, kn, vn, lw['na_rpb'], GROUPS)
    merged = _merge(h, mw['main'], jnp.stack([y_ssd, y_swa, y_s5, y_na]), mw['branch'], layer)
    x = _mm_res(merged, mw['out'], layer, xs, mod4, 2, 512, D_MODEL)

    h2 = _norm_mod((x,), lw['norm2_g'], mod4, 4, 3)
    u = _ffn_up(h2, mw['w1'], mw['w3'], layer)
    down = functools.partial(_mm_res, u, mw['w2'], layer, (x,), mod4, 5, 512, D_FF // 2)
    if not split_output:
        return down()
    n_first = GROUPS[1][0] // TM
    return down(rows=(0, n_first)), down(rows=(n_first, N_TOK // TM - n_first))


_LAYER_KEYS = ('ada_w', 'ada_b', 'norm1_g', 'norm2_g', 'w_in',
               'ssd_conv_w', 'ssd_conv_b', 'ssd_dt_bias', 'ssd_a_log', 'ssd_d', 'ssd_norm_g',
               'swa_q_norm_g', 'swa_k_norm_g', 'swa_sink',
               's5_a_re', 's5_a_im', 's5_log_step', 's5_b_re', 's5_b_im', 's5_c_re', 's5_c_im',
               's5_d', 's5_glu_w', 's5_glu_b',
               'na_q_norm_g', 'na_k_norm_g', 'na_rpb',
               'w_branch_ssd', 'w_branch_swa', 'w_branch_s5', 'w_branch_na', 'w_out',
               'ffn_w1', 'ffn_w3', 'ffn_w2')


def kernel(x_prompt, x_sample, c_prompt, c_sample, ada_w, ada_b, norm1_g, norm2_g, w_in, ssd_conv_w, ssd_conv_b, ssd_dt_bias, ssd_a_log, ssd_d, ssd_norm_g, swa_q_norm_g, swa_k_norm_g, swa_sink, s5_a_re, s5_a_im, s5_log_step, s5_b_re, s5_b_im, s5_c_re, s5_c_im, s5_d, s5_glu_w, s5_glu_b, na_q_norm_g, na_k_norm_g, na_rpb, w_branch_ssd, w_branch_swa, w_branch_s5, w_branch_na, w_out, ffn_w1, ffn_w3, ffn_w2):
    stacked = dict(zip(_LAYER_KEYS, (ada_w, ada_b, norm1_g, norm2_g, w_in,
                                     ssd_conv_w, ssd_conv_b, ssd_dt_bias, ssd_a_log, ssd_d, ssd_norm_g,
                                     swa_q_norm_g, swa_k_norm_g, swa_sink,
                                     s5_a_re, s5_a_im, s5_log_step, s5_b_re, s5_b_im, s5_c_re, s5_c_im,
                                     s5_d, s5_glu_w, s5_glu_b,
                                     na_q_norm_g, na_k_norm_g, na_rpb,
                                     w_branch_ssd, w_branch_swa, w_branch_s5, w_branch_na, w_out,
                                     ffn_w1, ffn_w3, ffn_w2)))
    xs = (x_prompt.reshape(BATCH * SEQ, D_MODEL), x_sample.reshape(DEC_BATCH * DEC_SEQ, D_MODEL))
    c8 = jnp.concatenate([c_prompt, c_sample, jnp.zeros((8 - N_SEQS, D_MODEL), F32)], axis=0)
    mw = _matmul_weights(w_in, w_branch_ssd, w_branch_swa, w_branch_s5, w_branch_na, w_out, ffn_w1, ffn_w3, ffn_w2,
                         s5_glu_w)
    for i in range(DEPTH):
        lw = {k: v[i] for k, v in stacked.items() if v.size < D_MODEL * D_MODEL}
        last = i == DEPTH - 1
        out = _layer(xs, c8, lw, mw, ada_w, ada_b, i, split_output=last)
        xs = out if last else (out,)
    return (xs[0].reshape(BATCH, SEQ, D_MODEL), xs[1].reshape(DEC_BATCH, DEC_SEQ, D_MODEL))
```

```python
import functools

import jax
import jax.numpy as jnp
import numpy as np
from jax import lax
from jax.experimental import pallas as pl
from jax.experimental.pallas import tpu as pltpu

D_MODEL = 4096
BATCH = 2
SEQ = 4096
DEPTH = 2
DEC_BATCH = 4
DEC_SEQ = 2048
N_TOK = BATCH * SEQ + DEC_BATCH * DEC_SEQ
N_SEQS = BATCH + DEC_BATCH
GROUPS = ((0, BATCH, SEQ), (BATCH * SEQ, DEC_BATCH, DEC_SEQ))
SEGS = tuple((row0 + i * l, l) for row0, b, l in GROUPS for i in range(b))

EPS = 1e-6
HEAD_DIM = 128
N_BRANCH = 4
SSD_HEADS = 16
SSD_HEAD_DIM = 64
SSD_INNER = SSD_HEADS * SSD_HEAD_DIM
SSD_GROUPS = 2
SSD_STATE = 128
SSD_CONV = 5
SSD_CHUNK = 128
SSD_CONV_DIM = SSD_INNER + 2 * SSD_GROUPS * SSD_STATE
SWA_HEADS = 8
SWA_KV_HEADS = 2
SWA_WIDTH = SWA_HEADS * HEAD_DIM
SWA_KV_WIDTH = SWA_KV_HEADS * HEAD_DIM
SWA_WINDOW = 128
SWA_BLOCK = 128
ROPE_THETA = 10000.0
S5_WIDTH = 1024
S5_GROUP = 16
S5_GROUPS = S5_WIDTH // S5_GROUP
S5_STATE = 64
NA_HEADS = 8
NA_WIDTH = NA_HEADS * HEAD_DIM
GRID_W = 64
NA_KR = 8
NA_KW = 16
D_FF = ((8 * D_MODEL + 3 * 256 - 1) // (3 * 256)) * 256
IN_SIZES = (SSD_INNER, SSD_CONV_DIM, 2 * SSD_HEADS,
            SWA_WIDTH, SWA_KV_WIDTH, SWA_KV_WIDTH,
            S5_WIDTH,
            NA_WIDTH, NA_WIDTH, NA_WIDTH,
            N_BRANCH * D_MODEL)
IN_OFFS = tuple(int(v) for v in np.cumsum((0,) + IN_SIZES))

LANES = 128
TM = 1024
N_MIX = IN_OFFS[10] - IN_SIZES[2]
VMEM_LIMIT = 56 * 1024 * 1024

COL_Z = 0
COL_XBC = COL_Z + SSD_INNER
COL_Q_SWA = COL_XBC + SSD_CONV_DIM
COL_K_SWA = COL_Q_SWA + SWA_WIDTH
COL_V_SWA = COL_K_SWA + SWA_KV_WIDTH
COL_U_S5 = COL_V_SWA + SWA_KV_WIDTH
COL_Q_NA = COL_U_S5 + S5_WIDTH
COL_K_NA = COL_Q_NA + NA_WIDTH
COL_V_NA = COL_K_NA + NA_WIDTH
COL_SSD_END = COL_XBC + SSD_CONV_DIM

F32 = jnp.float32
BF16 = jnp.bfloat16


def _batch_of_tile(i, tile):
    n_p = (BATCH * SEQ) // tile
    return jnp.where(i < n_p, i // (SEQ // tile), BATCH + (i - n_p) // (DEC_SEQ // tile))


def _params(sem):
    return pltpu.CompilerParams(dimension_semantics=sem, vmem_limit_bytes=VMEM_LIMIT)


def _ada_kernel(c_ref, w_ref, b_ref, o_ref):
    c = c_ref[...]
    a = (c * jax.nn.sigmoid(c)).astype(BF16)
    o_ref[...] = jnp.dot(a, w_ref[...].astype(BF16), preferred_element_type=F32) + b_ref[...]


def _ada(c8, ada_w, ada_b, layer):
    n = ada_w.shape[2]
    tn = 512
    return pl.pallas_call(
        _ada_kernel,
        out_shape=jax.ShapeDtypeStruct((8, n), F32),
        grid=(n // tn,),
        in_specs=[pl.BlockSpec((8, D_MODEL), lambda j: (0, 0)),
                  pl.BlockSpec((None, D_MODEL, tn), lambda j: (layer, 0, j)),
                  pl.BlockSpec((1, tn), lambda j: (0, j))],
        out_specs=pl.BlockSpec((8, tn), lambda j: (0, j)),
        compiler_params=_params(("parallel",)),
    )(c8, ada_w, ada_b[layer].reshape(1, n))


def _norm_mod_kernel(*refs, n_first):
    *x_refs, g_ref, scale_ref, shift_ref, o_ref = refs

    def body(x_ref):
        x = x_ref[...]
        y = x * lax.rsqrt(jnp.mean(x * x, axis=-1, keepdims=True) + EPS)
        y = y * g_ref[...]
        o_ref[...] = (y * (1.0 + scale_ref[...]) + shift_ref[...]).astype(o_ref.dtype)

    if len(x_refs) == 1:
        body(x_refs[0])
        return
    i = pl.program_id(0)
    pl.when(i < n_first)(lambda: body(x_refs[0]))
    pl.when(i >= n_first)(lambda: body(x_refs[1]))


def _group_specs(block, n_first):
    def first(i, j=0, *_):
        return (jnp.minimum(i, n_first - 1), jnp.where(i < n_first, j, 0))

    def second(i, j=0, *_):
        return (jnp.maximum(i - n_first, 0), jnp.where(i >= n_first, j, 0))

    return [pl.BlockSpec(block, first), pl.BlockSpec(block, second)]


def _norm_mod(xs, g, mod4, scale_idx, shift_idx):
    tr = 256
    n_first = xs[0].shape[0] // tr
    x_specs = ([pl.BlockSpec((tr, D_MODEL), lambda i: (i, 0))] if len(xs) == 1
               else _group_specs((tr, D_MODEL), n_first))
    return pl.pallas_call(
        functools.partial(_norm_mod_kernel, n_first=n_first),
        out_shape=jax.ShapeDtypeStruct((N_TOK, D_MODEL), BF16),
        grid=(N_TOK // tr,),
        in_specs=x_specs + [
            pl.BlockSpec((1, D_MODEL), lambda i: (0, 0)),
            pl.BlockSpec((None, None, 1, D_MODEL), lambda i: (_batch_of_tile(i, tr), scale_idx, 0, 0)),
            pl.BlockSpec((None, None, 1, D_MODEL), lambda i: (_batch_of_tile(i, tr), shift_idx, 0, 0))],
        out_specs=pl.BlockSpec((tr, D_MODEL), lambda i: (i, 0)),
        compiler_params=_params(("parallel",)),
    )(*xs, g.reshape(1, D_MODEL), mod4, mod4)


def _mm_kernel(a_ref, b_ref, o_ref):
    o_ref[...] = jnp.dot(a_ref[...], b_ref[...], preferred_element_type=F32).astype(o_ref.dtype)


def _mm(a, b, layer, n, tn, out_dtype):
    m, k = a.shape
    return pl.pallas_call(
        _mm_kernel,
        out_shape=jax.ShapeDtypeStruct((m, n), out_dtype),
        grid=(m // TM, n // tn),
        in_specs=[pl.BlockSpec((TM, k), lambda i, j: (i, 0)),
                  pl.BlockSpec((None, k, tn), lambda i, j: (layer, 0, j))],
        out_specs=pl.BlockSpec((TM, tn), lambda i, j: (i, j)),
        compiler_params=_params(("parallel", "parallel")),
    )(a, b)


def _merge_kernel(h_ref, wg_ref, m_ref, wb_ref, o_ref, acc_ref):
    b = pl.program_id(2)
    logits = jnp.dot(h_ref[...], wg_ref[...], preferred_element_type=F32)
    y = jnp.dot(m_ref[...], wb_ref[...], preferred_element_type=F32)
    contrib = jax.nn.sigmoid(logits) * y

    @pl.when(b == 0)
    def _():
        acc_ref[...] = contrib

    @pl.when(b > 0)
    def _():
        acc_ref[...] += contrib

    @pl.when(b == N_BRANCH - 1)
    def _():
        o_ref[...] = acc_ref[...].astype(o_ref.dtype)


def _merge(h, w_main, mix, wb, layer):
    tn = 512
    w = mix.shape[2]
    gate0 = N_MIX // tn
    per_branch = D_MODEL // tn
    return pl.pallas_call(
        _merge_kernel,
        out_shape=jax.ShapeDtypeStruct((N_TOK, D_MODEL), BF16),
        grid=(N_TOK // TM, D_MODEL // tn, N_BRANCH),
        in_specs=[pl.BlockSpec((TM, D_MODEL), lambda i, j, b: (i, 0)),
                  pl.BlockSpec((None, D_MODEL, tn), lambda i, j, b: (layer, 0, gate0 + b * per_branch + j)),
                  pl.BlockSpec((None, TM, w), lambda i, j, b: (b, i, 0)),
                  pl.BlockSpec((None, None, w, tn), lambda i, j, b: (layer, b, 0, j))],
        out_specs=pl.BlockSpec((TM, tn), lambda i, j, b: (i, j)),
        scratch_shapes=[pltpu.VMEM((TM, tn), F32)],
        compiler_params=_params(("parallel", "parallel", "arbitrary")),
    )(h, w_main, mix, wb)


def _mm_res_kernel(a_ref, b_ref, *rest, nk, n_first):
    *x_refs, gate_ref, o_ref, acc_ref = rest
    d = jnp.dot(a_ref[...], b_ref[...], preferred_element_type=F32)

    def finish(total):
        def store(x_ref):
            o_ref[...] = x_ref[...] + gate_ref[...] * total

        if len(x_refs) == 1:
            store(x_refs[0])
            return
        i = pl.program_id(0)
        pl.when(i < n_first)(functools.partial(store, x_refs[0]))
        pl.when(i >= n_first)(functools.partial(store, x_refs[1]))

    if nk == 1:
        finish(d)
        return
    k = pl.program_id(2)

    @pl.when(k == 0)
    def _():
        acc_ref[...] = d

    @pl.when(k > 0)
    def _():
        acc_ref[...] += d

    @pl.when(k == nk - 1)
    def _():
        finish(acc_ref[...])


def _mm_res(a, b, layer, xs, mod4, gate_idx, tn, tk, rows=None):
    kdim = a.shape[1]
    n = b.shape[2]
    nk = kdim // tk
    off, m_tiles = rows if rows is not None else (0, a.shape[0] // TM)
    n_first = xs[0].shape[0] // TM
    x_specs = ([pl.BlockSpec((TM, tn), lambda i, j, k: (i + off, j))] if len(xs) == 1
               else _group_specs((TM, tn), n_first))
    return pl.pallas_call(
        functools.partial(_mm_res_kernel, nk=nk, n_first=n_first),
        out_shape=jax.ShapeDtypeStruct((m_tiles * TM, n), F32),
        grid=(m_tiles, n // tn, nk),
        in_specs=[pl.BlockSpec((TM, tk), lambda i, j, k: (i + off, k)),
                  pl.BlockSpec((None, tk, tn), lambda i, j, k: (layer, k, j))] + x_specs + [
                  pl.BlockSpec((None, None, 1, tn),
                               lambda i, j, k: (_batch_of_tile(i + off, TM), gate_idx, 0, j))],
        out_specs=pl.BlockSpec((TM, tn), lambda i, j, k: (i, j)),
        scratch_shapes=[pltpu.VMEM((TM, tn), F32)],
        compiler_params=_params(("parallel", "parallel", "arbitrary")),
    )(a, b, *xs, mod4)


def _ffn_up_kernel(h_ref, w1_ref, w3_ref, o_ref):
    h = h_ref[...]
    a = jnp.dot(h, w1_ref[...], preferred_element_type=F32)
    b = jnp.dot(h, w3_ref[...], preferred_element_type=F32)
    o_ref[...] = (a * jax.nn.sigmoid(a) * b).astype(o_ref.dtype)


def _ffn_up(h, w1, w3, layer):
    tn = 512
    n = w1.shape[2]
    return pl.pallas_call(
        _ffn_up_kernel,
        out_shape=jax.ShapeDtypeStruct((N_TOK, n), BF16),
        grid=(N_TOK // TM, pl.cdiv(n, tn)),
        in_specs=[pl.BlockSpec((TM, D_MODEL), lambda i, j: (i, 0)),
                  pl.BlockSpec((None, D_MODEL, tn), lambda i, j: (layer, 0, j)),
                  pl.BlockSpec((None, D_MODEL, tn), lambda i, j: (layer, 0, j))],
        out_specs=pl.BlockSpec((TM, tn), lambda i, j: (i, j)),
        compiler_params=_params(("parallel", "parallel")),
    )(h, w1, w3)


def _tile_tables(segs, tile):
    pos, flags = [], []
    for _, length in segs:
        n = length // tile
        for j in range(n):
            pos.append(j)
            flags.append((1 if j == 0 else 0) | (2 if j == n - 1 else 0))
    return np.asarray(pos, np.int32), np.asarray(flags, np.int32)


def _head_slice(h):
    return slice(h * HEAD_DIM, (h + 1) * HEAD_DIM)


def _qk_prep_kernel(pos_ref, qs_lo_ref, qs_hi_ref, ks_ref, qn_ref, kn_ref, vn_ref, cos_ref, sin_ref, g_ref,
                    oqs_ref, oks_ref, oqn_ref, okn_ref, ovn_ref):
    del pos_ref
    half_heads = SWA_HEADS // 2
    cos = cos_ref[...]
    sin = sin_ref[...]

    def hnorm(x, g):
        return x * lax.rsqrt(jnp.mean(x * x, axis=-1, keepdims=True) + EPS) * g

    def rope(x):
        return x * cos + pltpu.roll(x, HEAD_DIM // 2, 1) * sin

    for h in range(SWA_HEADS):
        src = qs_lo_ref if h < half_heads else qs_hi_ref
        oqs_ref[:, _head_slice(h)] = rope(hnorm(src[:, _head_slice(h % half_heads)], g_ref[0:1, :])).astype(BF16)
    for h in range(SWA_KV_HEADS):
        oks_ref[:, _head_slice(h)] = rope(hnorm(ks_ref[:, _head_slice(h)], g_ref[1:2, :])).astype(BF16)
    for h in range(NA_HEADS):
        oqn_ref[:, _head_slice(h)] = hnorm(qn_ref[:, _head_slice(h)], g_ref[2:3, :]).astype(BF16)
        okn_ref[:, _head_slice(h)] = hnorm(kn_ref[:, _head_slice(h)], g_ref[3:4, :]).astype(BF16)
    ovn_ref[...] = vn_ref[...].astype(BF16)


def _rope_tables(max_len):
    half = HEAD_DIM // 2
    inv_freq = ROPE_THETA ** (-jnp.arange(half, dtype=F32) / half)
    ang = jnp.arange(max_len, dtype=F32)[:, None] * inv_freq[None, :]
    cos, sin = jnp.cos(ang), jnp.sin(ang)
    return jnp.concatenate([cos, cos], axis=1), jnp.concatenate([-sin, sin], axis=1)


def _qk_prep(pm, segs, lw):
    n_tok = pm.shape[0]
    tr = 256
    pos, _ = _tile_tables(segs, tr)
    cos, sin = _rope_tables(max(l for _, l in segs))
    gains = jnp.concatenate([lw['swa_q_norm_g'][None], lw['swa_k_norm_g'][None],
                             lw['na_q_norm_g'][None], lw['na_k_norm_g'][None],
                             jnp.zeros((4, HEAD_DIM), F32)], axis=0)
    wide = lambda cb: pl.BlockSpec((tr, 1024), lambda i, p: (i, cb))
    narrow = lambda cb: pl.BlockSpec((tr, 256), lambda i, p: (i, cb))
    grid_spec = pltpu.PrefetchScalarGridSpec(
        num_scalar_prefetch=1, grid=(n_tok // tr,),
        in_specs=[pl.BlockSpec((tr, 512), lambda i, p: (i, COL_Q_SWA // 512)),
                  pl.BlockSpec((tr, 512), lambda i, p: (i, COL_Q_SWA // 512 + 1)),
                  narrow(COL_K_SWA // 256), wide(COL_Q_NA // 1024),
                  wide(COL_K_NA // 1024), wide(COL_V_NA // 1024),
                  pl.BlockSpec((tr, HEAD_DIM), lambda i, p: (p[i], 0)),
                  pl.BlockSpec((tr, HEAD_DIM), lambda i, p: (p[i], 0)),
                  pl.BlockSpec((8, HEAD_DIM), lambda i, p: (0, 0))],
        out_specs=[pl.BlockSpec((tr, 1024), lambda i, p: (i, 0)),
                   pl.BlockSpec((tr, 256), lambda i, p: (i, 0)),
                   pl.BlockSpec((tr, 1024), lambda i, p: (i, 0)),
                   pl.BlockSpec((tr, 1024), lambda i, p: (i, 0)),
                   pl.BlockSpec((tr, 1024), lambda i, p: (i, 0))])
    sds = lambda w: jax.ShapeDtypeStruct((n_tok, w), BF16)
    return pl.pallas_call(
        _qk_prep_kernel, grid_spec=grid_spec,
        out_shape=[sds(1024), sds(256), sds(1024), sds(1024), sds(1024)],
        compiler_params=_params(("parallel",)),
    )(jnp.asarray(pos), pm, pm, pm, pm, pm, pm, cos, sin, gains)


SWA_TQ = 512
SWA_GRP = SWA_HEADS // SWA_KV_HEADS


def _swa_kernel(flags_ref, sink_ref, q_ref, kc_ref, kp_ref, kn_ref, vc_ref, vp_ref, vn_ref, o_ref, *, tq):
    fl = flags_ref[pl.program_id(0)]
    lo = jnp.where((fl & 1) > 0, SWA_BLOCK, 0)
    hi = jnp.where((fl & 2) > 0, 2 * SWA_BLOCK, 3 * SWA_BLOCK)
    nqb = tq // SWA_BLOCK
    m_rows = SWA_GRP * SWA_BLOCK
    row = lax.broadcasted_iota(jnp.int32, (m_rows, 3 * SWA_BLOCK), 0) & (SWA_BLOCK - 1)
    col = lax.broadcasted_iota(jnp.int32, (m_rows, 3 * SWA_BLOCK), 1)
    band = (col >= row) & (col <= row + 2 * SWA_WINDOW)
    scale = HEAD_DIM ** -0.5
    for g in range(SWA_KV_HEADS):
        ks = _head_slice(g)
        k_ext = jnp.concatenate([kp_ref[:, ks], kc_ref[:, ks], kn_ref[:, ks]], axis=0)
        v_ext = jnp.concatenate([vp_ref[:, ks], vc_ref[:, ks], vn_ref[:, ks]], axis=0).astype(BF16)
        sk = jnp.concatenate([jnp.full((SWA_BLOCK, 1), sink_ref[g * SWA_GRP + h], F32)
                              for h in range(SWA_GRP)], axis=0)
        for qb in range(nqb):
            rows = slice(qb * SWA_BLOCK, (qb + 1) * SWA_BLOCK)
            q = jnp.concatenate([q_ref[rows, _head_slice(g * SWA_GRP + h)] for h in range(SWA_GRP)], axis=0)
            keys = k_ext[qb * SWA_BLOCK:(qb + 3) * SWA_BLOCK]
            vals = v_ext[qb * SWA_BLOCK:(qb + 3) * SWA_BLOCK]
            s = lax.dot_general(q, keys, (((1,), (1,)), ((), ())), preferred_element_type=F32) * scale
            mask = band
            if qb == 0:
                mask = mask & (col >= lo)
            if qb == nqb - 1:
                mask = mask & (col < hi)
            s = jnp.where(mask, s, -jnp.inf)
            m = jnp.maximum(jnp.max(s, axis=-1, keepdims=True), sk)
            p = jnp.exp(s - m)
            denom = jnp.sum(p, axis=-1, keepdims=True) + jnp.exp(sk - m)
            o = jnp.dot((p / denom).astype(BF16), vals, preferred_element_type=F32)
            for h in range(SWA_GRP):
                o_ref[rows, _head_slice(g * SWA_GRP + h)] = o[h * SWA_BLOCK:(h + 1) * SWA_BLOCK].astype(o_ref.dtype)


def _swa(qs, ks, pm, sink, segs):
    n_tok = qs.shape[0]
    tq = min(SWA_TQ, min(l for _, l in segs))
    _, flags = _tile_tables(segs, tq)
    nb = tq // SWA_BLOCK
    last_blk = n_tok // SWA_BLOCK - 1
    prev_map = lambda cb: (lambda i, f: (jnp.maximum(i * nb - 1, 0), cb))
    next_map = lambda cb: (lambda i, f: (jnp.minimum((i + 1) * nb, last_blk), cb))
    vcb = COL_V_SWA // SWA_KV_WIDTH
    grid_spec = pltpu.PrefetchScalarGridSpec(
        num_scalar_prefetch=1, grid=(n_tok // tq,),
        in_specs=[pl.BlockSpec(memory_space=pltpu.SMEM),
                  pl.BlockSpec((tq, SWA_WIDTH), lambda i, f: (i, 0)),
                  pl.BlockSpec((tq, SWA_KV_WIDTH), lambda i, f: (i, 0)),
                  pl.BlockSpec((SWA_BLOCK, SWA_KV_WIDTH), prev_map(0)),
                  pl.BlockSpec((SWA_BLOCK, SWA_KV_WIDTH), next_map(0)),
                  pl.BlockSpec((tq, SWA_KV_WIDTH), lambda i, f: (i, vcb)),
                  pl.BlockSpec((SWA_BLOCK, SWA_KV_WIDTH), prev_map(vcb)),
                  pl.BlockSpec((SWA_BLOCK, SWA_KV_WIDTH), next_map(vcb))],
        out_specs=pl.BlockSpec((tq, SWA_WIDTH), lambda i, f: (i, 0)))
    return pl.pallas_call(
        functools.partial(_swa_kernel, tq=tq), grid_spec=grid_spec,
        out_shape=jax.ShapeDtypeStruct((n_tok, SWA_WIDTH), BF16),
        compiler_params=_params(("parallel",)),
    )(jnp.asarray(flags), sink, qs, ks, ks, ks, pm, pm, pm)


NA_R = NA_KR // 2
NA_WIN = NA_R + NA_KR


def _na_bias_table(rpb):
    off = np.array([np.zeros(NA_R, int), np.arange(NA_R), np.full(NA_R, NA_R)])
    p_of = np.array([np.arange(NA_R), np.full(NA_R, NA_KR // 2), NA_KR // 2 + np.arange(NA_R)])
    j = np.arange(NA_WIN)[None, None, :] - off[:, :, None]
    row_ok = (j >= 0) & (j < NA_KR)
    dr = np.clip(j - p_of[:, :, None] + (NA_KR - 1), 0, 2 * NA_KR - 2)
    qc = np.arange(GRID_W)
    kc = np.arange(GRID_W)
    dc = np.clip(kc[None, :] - qc[:, None], -(NA_KW - 1), NA_KW - 1) + (NA_KW - 1)
    col_start = np.clip(qc - NA_KW // 2, 0, GRID_W - NA_KW)
    valid = (kc[None, :] >= col_start[:, None]) & (kc[None, :] < col_start[:, None] + NA_KW)
    hp = lax.Precision.HIGHEST
    sel_dc = jnp.asarray(np.eye(2 * NA_KW - 1, dtype=np.float32)[dc.reshape(-1)])
    sel_dr = jnp.asarray(np.eye(2 * NA_KR - 1, dtype=np.float32)[dr.reshape(-1)])
    cols = jnp.einsum('hab,xb->hax', rpb, sel_dc, precision=hp)
    bias = jnp.einsum('ya,hax->yhx', sel_dr, cols, precision=hp)
    bias = bias.reshape(3, NA_R, NA_WIN, NA_HEADS, GRID_W, GRID_W)
    ok = row_ok[:, :, :, None, None, None] & valid[None, None, None, None, :, :]
    bias = jnp.where(ok, bias, -jnp.inf)
    return jnp.transpose(bias, (0, 3, 1, 4, 2, 5)).reshape(3, NA_HEADS, NA_R * GRID_W, NA_WIN * GRID_W)


def _na_kernel(q_ref, k_ref, v_ref, bias_ref, o_ref, *, rows):
    first_row = jnp.clip(pl.program_id(1) * NA_R - NA_KR // 2, 0, rows - NA_WIN)
    keys = pl.ds(pl.multiple_of(first_row * GRID_W, GRID_W), NA_WIN * GRID_W)
    scale = HEAD_DIM ** -0.5
    for h in range(NA_HEADS):
        s = lax.dot_general(q_ref[:, _head_slice(h)], k_ref[keys, _head_slice(h)], (((1,), (1,)), ((), ())),
                            preferred_element_type=F32) * scale
        s = s + bias_ref[h]
        p = jnp.exp(s - jnp.max(s, axis=-1, keepdims=True))
        p = p / jnp.sum(p, axis=-1, keepdims=True)
        o = jnp.dot(p.astype(BF16), v_ref[keys, _head_slice(h)], preferred_element_type=F32)
        o_ref[:, _head_slice(h)] = o.astype(o_ref.dtype)


def _na_group(qn, kn, vn, bias, row0, b, l):
    rows = l // GRID_W
    nsteps = rows // NA_R
    tq = NA_R * GRID_W
    kind = lambda st: jnp.where(st == 0, 0, jnp.where(st == nsteps - 1, 2, 1))
    seq_spec = pl.BlockSpec((l, NA_WIDTH), lambda bi, st: (row0 // l + bi, 0))
    return pl.pallas_call(
        functools.partial(_na_kernel, rows=rows),
        out_shape=jax.ShapeDtypeStruct((b * l, NA_WIDTH), BF16),
        grid=(b, nsteps),
        in_specs=[pl.BlockSpec((tq, NA_WIDTH), lambda bi, st: (row0 // tq + bi * nsteps + st, 0)),
                  seq_spec, seq_spec,
                  pl.BlockSpec((None, NA_HEADS, tq, NA_WIN * GRID_W), lambda bi, st: (kind(st), 0, 0, 0))],
        out_specs=pl.BlockSpec((tq, NA_WIDTH), lambda bi, st: (bi * nsteps + st, 0)),
        compiler_params=_params(("parallel", "arbitrary")),
    )(qn, kn, vn, bias)


def _na(qn, kn, vn, rpb, groups):
    bias = _na_bias_table(rpb)
    return jnp.concatenate([_na_group(qn, kn, vn, bias, row0, b, l) for row0, b, l in groups], axis=0)


def _ssd_prep_kernel(flags_ref, xc_ref, xp_ref, xn_ref, dt_ref, w_ref, b_ref, dtb_ref,
                     xs_ref, bc_ref, dtf_ref, *, tr):
    fl = flags_ref[pl.program_id(0)]
    halo = 8
    xbc = slice(COL_XBC, COL_SSD_END)
    xp = jnp.where((fl & 1) > 0, 0.0, xp_ref[:, xbc])
    xn = jnp.where((fl & 2) > 0, 0.0, xn_ref[:, xbc])
    ext = jnp.concatenate([xp, xc_ref[:, xbc], xn], axis=0)
    n = tr + 2 * halo
    acc = None
    for k in range(SSD_CONV):
        sh = (SSD_CONV // 2 - k) % n
        xk = ext if sh == 0 else pltpu.roll(ext, sh, 0)
        term = xk[halo:halo + tr] * w_ref[k:k + 1, :]
        acc = term if acc is None else acc + term
    acc = acc + b_ref[...]
    y = acc * jax.nn.sigmoid(acc)
    xs_ref[...] = y[:, :SSD_INNER]
    bc_ref[...] = y[:, SSD_INNER:]
    t = dt_ref[...] + dtb_ref[...]
    dtf_ref[...] = jnp.maximum(t, 0.0) + jnp.log1p(jnp.exp(-jnp.abs(t)))


def _dt_pad(v):
    z = jnp.zeros((LANES - SSD_HEADS,), v.dtype)
    return jnp.concatenate([v[0], z, v[1], z])


def _ssd_prep(pm, dt_raw, segs, lw):
    n_tok = pm.shape[0]
    tr = 256
    _, flags = _tile_tables(segs, tr)
    nb8 = tr // 8
    last8 = n_tok // 8 - 1
    w8 = jnp.concatenate([lw['ssd_conv_w'], jnp.zeros((8 - SSD_CONV, SSD_CONV_DIM), F32)], axis=0)
    dtb = _dt_pad(lw['ssd_dt_bias'])[None, :]
    grid_spec = pltpu.PrefetchScalarGridSpec(
        num_scalar_prefetch=1, grid=(n_tok // tr,),
        in_specs=[pl.BlockSpec((tr, COL_SSD_END), lambda i, f: (i, 0)),
                  pl.BlockSpec((8, COL_SSD_END), lambda i, f: (jnp.maximum(i * nb8 - 1, 0), 0)),
                  pl.BlockSpec((8, COL_SSD_END), lambda i, f: (jnp.minimum((i + 1) * nb8, last8), 0)),
                  pl.BlockSpec((tr, 2 * LANES), lambda i, f: (i, 0)),
                  pl.BlockSpec((8, SSD_CONV_DIM), lambda i, f: (0, 0)),
                  pl.BlockSpec((1, SSD_CONV_DIM), lambda i, f: (0, 0)),
                  pl.BlockSpec((1, 2 * LANES), lambda i, f: (0, 0))],
        out_specs=[pl.BlockSpec((tr, SSD_INNER), lambda i, f: (i, 0)),
                   pl.BlockSpec((tr, 2 * SSD_GROUPS * SSD_STATE), lambda i, f: (i, 0)),
                   pl.BlockSpec((tr, 2 * LANES), lambda i, f: (i, 0))])
    return pl.pallas_call(
        functools.partial(_ssd_prep_kernel, tr=tr), grid_spec=grid_spec,
        out_shape=[jax.ShapeDtypeStruct((n_tok, SSD_INNER), F32),
                   jax.ShapeDtypeStruct((n_tok, 2 * SSD_GROUPS * SSD_STATE), F32),
                   jax.ShapeDtypeStruct((n_tok, 2 * LANES), F32)],
        compiler_params=_params(("parallel",)),
    )(jnp.asarray(flags), pm, pm, pm, dt_raw, w8, lw['ssd_conv_b'][None, :], dtb)


def _split3(x):
    hi = x.astype(BF16)
    r = x - hi.astype(F32)
    mid = r.astype(BF16)
    lo = (r - mid.astype(F32)).astype(BF16)
    return hi, mid, lo


def _dot_sel_l(sel, x):
    hi, mid, lo = _split3(x)
    d = lambda t: jnp.dot(sel, t, preferred_element_type=F32)
    return d(lo) + d(mid) + d(hi)


def _dot_sel_r(x, sel):
    hi, mid, lo = _split3(x)
    d = lambda t: jnp.dot(t, sel, preferred_element_type=F32)
    return d(lo) + d(mid) + d(hi)


def _ssd_scan_kernel(order_ref, reset_ref, xs_ref, bc_ref, dt_ref, alog_ref, e_ref, *rest, bwd):
    if bwd:
        yf_ref, z_ref, dsk_ref, g_ref, o_ref, state_ref = rest
    else:
        o_ref, state_ref = rest
    del order_ref
    q = SSD_CHUNK
    n_pairs = SSD_HEADS // 2
    gs = SSD_GROUPS * SSD_STATE

    @pl.when(reset_ref[pl.program_id(0)] > 0)
    def _():
        state_ref[...] = jnp.zeros_like(state_ref)

    dt = dt_ref[...]
    da = dt * (-jnp.exp(alog_ref[...]))
    r_i = lax.broadcasted_iota(jnp.int32, (q, q), 0)
    c_i = lax.broadcasted_iota(jnp.int32, (q, q), 1)
    incl = (r_i <= c_i) if bwd else (r_i >= c_i)
    tri = jnp.where(incl, 1.0, 0.0).astype(BF16)
    cs = _dot_sel_l(tri, da)
    cs_t = cs.T
    e = e_ref[...]
    dt_x = _dot_sel_r(dt, e)
    cs_x = _dot_sel_r(cs, e)
    total = cs_x[0:1, :] if bwd else cs_x[q - 1:q, :]
    xs = xs_ref[...]
    xdt = xs * dt_x
    w_state = (xdt * jnp.exp(total - cs_x)).astype(BF16)
    xdt_b = xdt.astype(BF16)
    ecs_x = jnp.exp(cs_x)
    etot = jnp.exp(total)
    bc = bc_ref[...]
    b_t = bc[:, :gs].T
    lane = lax.broadcasted_iota(jnp.int32, (q, LANES), 1)
    ys = []
    for g in range(SSD_GROUPS):
        b_g = bc[:, g * SSD_STATE:(g + 1) * SSD_STATE].astype(BF16)
        c_g = bc[:, gs + g * SSD_STATE:gs + (g + 1) * SSD_STATE].astype(BF16)
        bt_g = b_t[g * SSD_STATE:(g + 1) * SSD_STATE, :].astype(BF16)
        cb = lax.dot_general(c_g, b_g, (((1,), (1,)), ((), ())), preferred_element_type=F32)
        for j in range(n_pairs // SSD_GROUPS):
            pair = g * (n_pairs // SSD_GROUPS) + j
            lanes = slice(pair * LANES, (pair + 1) * LANES)
            halves = []
            for hh in range(2):
                h = 2 * pair + hh
                diff = jnp.broadcast_to(cs[:, h:h + 1], (q, q)) - jnp.broadcast_to(cs_t[h:h + 1, :], (q, q))
                decay = jnp.where(incl, jnp.exp(diff), 0.0)
                halves.append(jnp.dot((cb * decay).astype(BF16), xdt_b[:, lanes], preferred_element_type=F32))
            y_diag = jnp.where(lane < SSD_HEAD_DIM, halves[0], halves[1])
            s_prev = state_ref[pair]
            y_off = jnp.dot(c_g, s_prev.astype(BF16), preferred_element_type=F32) * ecs_x[:, lanes]
            contrib = jnp.dot(bt_g, w_state[:, lanes], preferred_element_type=F32)
            state_ref[pair] = s_prev * etot[:, lanes] + contrib
            ys.append(y_diag + y_off)
    y = jnp.concatenate(ys, axis=1)
    if not bwd:
        o_ref[...] = y
        return
    y = yf_ref[...] + y + dsk_ref[...] * xs
    z = z_ref[...]
    y = y * (z * jax.nn.sigmoid(z))
    y = y * lax.rsqrt(jnp.mean(y * y, axis=-1, keepdims=True) + EPS)
    o_ref[...] = (y * g_ref[...]).astype(o_ref.dtype)


def _ssd_scan(xs, bc, dtf, alog, e_mat, segs, bwd, extra=()):
    n_tok = xs.shape[0]
    q = SSD_CHUNK
    _, flags = _tile_tables(segs, q)
    n = n_tok // q
    order = np.arange(n, dtype=np.int32)[::-1].copy() if bwd else np.arange(n, dtype=np.int32)
    reset = ((flags[order] & (2 if bwd else 1)) > 0).astype(np.int32)
    d = 1 if bwd else 0
    row = lambda w, cb=0: pl.BlockSpec((q, w), lambda i, o, r: (o[i], cb))
    const = lambda shape: pl.BlockSpec(shape, lambda i, o, r: (0,) * len(shape))
    in_specs = [row(SSD_INNER), row(2 * SSD_GROUPS * SSD_STATE), row(LANES, d),
                pl.BlockSpec((None, 1, LANES), lambda i, o, r: (d, 0, 0)), const((LANES, SSD_INNER))]
    if bwd:
        in_specs += [row(SSD_INNER), row(SSD_INNER, COL_Z // SSD_INNER), const((1, SSD_INNER)),
                     const((1, SSD_INNER))]
    grid_spec = pltpu.PrefetchScalarGridSpec(
        num_scalar_prefetch=2, grid=(n,), in_specs=in_specs, out_specs=row(SSD_INNER),
        scratch_shapes=[pltpu.VMEM((SSD_HEADS // 2, SSD_STATE, LANES), F32)])
    return pl.pallas_call(
        functools.partial(_ssd_scan_kernel, bwd=bwd), grid_spec=grid_spec,
        out_shape=jax.ShapeDtypeStruct((n_tok, SSD_INNER), BF16 if bwd else F32),
        compiler_params=_params(("arbitrary",)),
    )(jnp.asarray(order), jnp.asarray(reset), xs, bc, dtf, alog, e_mat, *extra)


def _ssd(pm, dt_raw, segs, lw):
    xs, bc, dtf = _ssd_prep(pm, dt_raw, segs, lw)
    alog = _dt_pad(lw['ssd_a_log']).reshape(2, 1, LANES)
    e_np = np.zeros((LANES, SSD_INNER), np.float32)
    for h in range(SSD_HEADS):
        e_np[h, h * SSD_HEAD_DIM:(h + 1) * SSD_HEAD_DIM] = 1.0
    e_mat = jnp.asarray(e_np, BF16)
    y_fwd = _ssd_scan(xs, bc, dtf, alog, e_mat, segs, False)
    dsk = jnp.repeat(lw['ssd_d'], SSD_HEAD_DIM)[None, :]
    return _ssd_scan(xs, bc, dtf, alog, e_mat, segs, True,
                     extra=(y_fwd, pm, dsk, lw['ssd_norm_g'][None, :]))


S5_Q = 16
S5_SB = LANES // S5_GROUP
S5_NSB = S5_GROUPS // S5_SB
S5_SBW = S5_Q * LANES
S5_SW = S5_SB * S5_STATE
S5_NS = S5_GROUPS * S5_STATE
S5_CW = S5_Q * S5_GROUP
assert 4 * S5_STATE == S5_CW


def _s5_weights(lw):
    hp = lax.Precision.HIGHEST
    qn, g, p, c = S5_Q, S5_GROUPS, S5_STATE, S5_GROUP
    nsb, sb = S5_NSB, S5_SB
    cmul = lambda a, b: (a[0] * b[0] - a[1] * b[1], a[0] * b[1] + a[1] * b[0])
    a_re, a_im = lw['s5_a_re'], lw['s5_a_im']
    step = jnp.exp(lw['s5_log_step'])[..., None]
    mag = jnp.exp(a_re * step)
    lam_bar = (mag * jnp.cos(a_im * step), mag * jnp.sin(a_im * step))
    den = a_re * a_re + a_im * a_im
    coef = cmul((lam_bar[0] - 1.0, lam_bar[1]), (a_re / den, -a_im / den))
    b_bar = cmul((coef[0][..., None], coef[1][..., None]),
                 (lw['s5_b_re'][None], lw['s5_b_im'][None]))
    c_c = (lw['s5_c_re'], lw['s5_c_im'])
    pows = [(jnp.ones_like(mag), jnp.zeros_like(mag))]
    for _ in range(qn):
        pows.append(cmul(pows[-1], lam_bar))
    pw = (jnp.stack([t[0] for t in pows], axis=1), jnp.stack([t[1] for t in pows], axis=1))

    t1 = cmul((c_c[0][:, None], c_c[1][:, None]),
              (pw[0][:, :qn, :, None, :], pw[1][:, :qn, :, None, :]))
    lag = jnp.einsum('dmgiq,dgqj->dmgij', jnp.concatenate([t1[0], -t1[1]], axis=-1),
                     jnp.concatenate([b_bar[0], b_bar[1]], axis=2), precision=hp)

    s_idx = np.arange(qn)[:, None]
    t_idx = np.arange(qn)[None, :]
    df = t_idx - s_idx
    kf = lag[0][np.clip(df, 0, qn - 1)] * jnp.asarray(df >= 0, F32)[:, :, None, None, None]
    kb = lag[1][np.clip(-df, 0, qn - 1)] * jnp.asarray(df <= 0, F32)[:, :, None, None, None]
    tt = jnp.transpose(kf + kb, (2, 0, 4, 1, 3))
    by_block_row = lambda a: jnp.transpose(a.reshape(nsb, sb, qn, c, S5_CW), (0, 2, 1, 3, 4)).reshape(
        nsb, S5_SBW, S5_CW)
    tt = by_block_row(tt)

    def in_op(d, powers):
        w = cmul((pw[0][d, powers][:, :, :, None], pw[1][d, powers][:, :, :, None]),
                 (b_bar[0][d][None], b_bar[1][d][None]))
        return tuple(jnp.transpose(t, (1, 0, 3, 2)) for t in w)

    wf = in_op(0, qn - 1 - np.arange(qn))
    wb = in_op(1, np.arange(qn))
    wa = by_block_row(jnp.stack([wf[0], wf[1], wb[0], wb[1]], axis=3))

    def out_op(d, powers):
        return cmul((jnp.transpose(c_c[0][d], (0, 2, 1))[:, :, None, :],
                     jnp.transpose(c_c[1][d], (0, 2, 1))[:, :, None, :]),
                    (jnp.transpose(pw[0][d, powers], (1, 2, 0))[:, :, :, None],
                     jnp.transpose(pw[1][d, powers], (1, 2, 0))[:, :, :, None]))

    vf = out_op(0, 1 + np.arange(qn))
    vb = out_op(1, qn - np.arange(qn))
    v4 = jnp.stack([vf[0], -vf[1], vb[0], -vb[1]], axis=0)
    vc = jnp.transpose(v4.reshape(4, nsb, sb * p, S5_CW), (1, 0, 2, 3)).reshape(nsb, 4 * S5_SW, S5_CW)

    lam_q = (pw[0][:, qn].reshape(2, 1, S5_NS), pw[1][:, qn].reshape(2, 1, S5_NS))
    dsk = jnp.broadcast_to(lw['s5_d'].reshape(nsb, 1, 1, LANES), (nsb, 1, qn, LANES)).reshape(nsb, 1, S5_SBW)
    return dict(tt=tt.astype(BF16), wa=wa.astype(BF16), vc=vc.astype(BF16),
                lam_re=lam_q[0], lam_im=lam_q[1], dsk=dsk)


S5_TT = 4096


def _s5_blocks(u_ref, mt):
    return jnp.concatenate([u_ref[pl.ds(t, mt, stride=S5_Q), :] for t in range(S5_Q)], axis=1)


def _s5_spread_matrix(inner):
    src = np.arange(S5_CW)
    dst = np.arange(S5_SBW)
    same = (src[:, None] // inner == dst[None, :] // (S5_SB * inner)) & (src[:, None] % inner == dst[None, :] % inner)
    return jnp.asarray(same, BF16)


def _s5_expand(c_ref, spread_ref, out_ref, row_inner, col_inner):
    rows = S5_CW
    col_g = (lax.broadcasted_iota(jnp.int32, (rows, S5_SBW), 1) // col_inner) % S5_SB
    row_l = lax.broadcasted_iota(jnp.int32, (rows, S5_SBW), 0)
    for r0 in range(0, S5_SBW, rows):
        row_g = ((row_l + r0) // row_inner) % S5_SB
        full = jnp.dot(c_ref[r0:r0 + rows, :], spread_ref[...], preferred_element_type=F32)
        out_ref[r0:r0 + rows, :] = jnp.where(row_g == col_g, full, 0.0).astype(out_ref.dtype)


def _s5_in_kernel(u_ref, wc_ref, spread_ref, fre_ref, fim_ref, bre_ref, bim_ref, w_ref, *, mt):
    @pl.when(pl.program_id(1) == 0)
    def _():
        _s5_expand(wc_ref, spread_ref, w_ref, S5_GROUP, S5_STATE)

    r = jnp.dot(_s5_blocks(u_ref, mt).astype(BF16), w_ref[...], preferred_element_type=F32)
    fre_ref[...] = r[:, 0 * S5_SW:1 * S5_SW]
    fim_ref[...] = r[:, 1 * S5_SW:2 * S5_SW]
    bre_ref[...] = r[:, 2 * S5_SW:3 * S5_SW]
    bim_ref[...] = r[:, 3 * S5_SW:4 * S5_SW]


def _s5_in(pm, wa):
    n_tok = pm.shape[0]
    tt = min(S5_TT, n_tok)
    mt = tt // S5_Q
    ucb = COL_U_S5 // LANES
    st = pl.BlockSpec((mt, S5_SW), lambda k, i: (i, k))
    sds = jax.ShapeDtypeStruct((n_tok // S5_Q, S5_NS), F32)
    return pl.pallas_call(
        functools.partial(_s5_in_kernel, mt=mt), out_shape=[sds] * 4, grid=(S5_NSB, n_tok // tt),
        in_specs=[pl.BlockSpec((tt, LANES), lambda k, i: (i, ucb + k)),
                  pl.BlockSpec((None, S5_SBW, S5_CW), lambda k, i: (k, 0, 0)),
                  pl.BlockSpec((S5_CW, S5_SBW), lambda k, i: (0, 0))],
        out_specs=[st] * 4,
        scratch_shapes=[pltpu.VMEM((S5_SBW, 4 * S5_SW), BF16)],
        compiler_params=_params(("parallel", "arbitrary")),
    )(pm, wa, _s5_spread_matrix(S5_STATE))


def _s5_rec_kernel(flags_ref, fre_ref, fim_ref, bre_ref, bim_ref, lam_re_ref, lam_im_ref,
                   xre_ref, xim_ref, zre_ref, zim_ref, state_ref, *, tc):
    i = pl.program_id(0)
    n = pl.num_programs(0)

    @pl.when((flags_ref[i] & 1) > 0)
    def _():
        state_ref[0:2] = jnp.zeros((2, 1, S5_NS), F32)

    @pl.when((flags_ref[n - 1 - i] & 2) > 0)
    def _():
        state_ref[2:4] = jnp.zeros((2, 1, S5_NS), F32)

    flr, fli = lam_re_ref[0], lam_im_ref[0]
    blr, bli = lam_re_ref[1], lam_im_ref[1]

    def body(k, carry):
        xr, xi, zr, zi = carry
        cf = pl.ds(k, 1)
        cb = pl.ds(tc - 1 - k, 1)
        xre_ref[cf, :] = xr
        xim_ref[cf, :] = xi
        zre_ref[cb, :] = zr
        zim_ref[cb, :] = zi
        return (flr * xr - fli * xi + fre_ref[cf, :], flr * xi + fli * xr + fim_ref[cf, :],
                blr * zr - bli * zi + bre_ref[cb, :], blr * zi + bli * zr + bim_ref[cb, :])

    out = lax.fori_loop(0, tc, body, tuple(state_ref[j] for j in range(4)))
    for j in range(4):
        state_ref[j] = out[j]


def _s5_rec(fre, fim, bre, bim, lam_re, lam_im, segs):
    m = fre.shape[0]
    tc = min(128, min(l for _, l in segs) // S5_Q)
    _, flags = _tile_tables(segs, tc * S5_Q)
    n = m // tc
    fwd = pl.BlockSpec((tc, S5_NS), lambda i, f: (i, 0))
    bwd = pl.BlockSpec((tc, S5_NS), lambda i, f: (n - 1 - i, 0))
    lam = pl.BlockSpec((2, 1, S5_NS), lambda i, f: (0, 0, 0))
    grid_spec = pltpu.PrefetchScalarGridSpec(
        num_scalar_prefetch=1, grid=(n,), in_specs=[fwd, fwd, bwd, bwd, lam, lam],
        out_specs=[fwd, fwd, bwd, bwd],
        scratch_shapes=[pltpu.VMEM((4, 1, S5_NS), F32)])
    sds = jax.ShapeDtypeStruct((m, S5_NS), F32)
    return pl.pallas_call(
        functools.partial(_s5_rec_kernel, tc=tc), grid_spec=grid_spec, out_shape=[sds] * 4,
        compiler_params=_params(("arbitrary",)),
    )(jnp.asarray(flags), fre, fim, bre, bim, lam_re, lam_im)


def _s5_out_kernel(u_ref, ttc_ref, xre_ref, xim_ref, zre_ref, zim_ref, vcc_ref, dsk_ref, spread_ref, y_ref,
                   tt_ref, vc_ref, *, mt):
    @pl.when(pl.program_id(1) == 0)
    def _():
        _s5_expand(ttc_ref, spread_ref, tt_ref, S5_GROUP, S5_GROUP)
        _s5_expand(vcc_ref, spread_ref, vc_ref, S5_STATE, S5_GROUP)

    u = _s5_blocks(u_ref, mt)
    intra = jnp.dot(u.astype(BF16), tt_ref[...], preferred_element_type=F32)
    st = jnp.concatenate([xre_ref[...], xim_ref[...], zre_ref[...], zim_ref[...]], axis=1).astype(BF16)
    carry = jnp.dot(st, vc_ref[...], preferred_element_type=F32)
    y = intra + carry + dsk_ref[...] * u
    for t in range(S5_Q):
        y_ref[pl.ds(t, mt, stride=S5_Q), :] = y[:, t * LANES:(t + 1) * LANES]


def _s5_out(pm, w, xre, xim, zre, zim):
    n_tok = pm.shape[0]
    tt = min(S5_TT, n_tok)
    mt = tt // S5_Q
    ucb = COL_U_S5 // LANES
    st = pl.BlockSpec((mt, S5_SW), lambda k, i: (i, k))
    return pl.pallas_call(
        functools.partial(_s5_out_kernel, mt=mt),
        out_shape=jax.ShapeDtypeStruct((n_tok, S5_WIDTH), F32), grid=(S5_NSB, n_tok // tt),
        in_specs=[pl.BlockSpec((tt, LANES), lambda k, i: (i, ucb + k)),
                  pl.BlockSpec((None, S5_SBW, S5_CW), lambda k, i: (k, 0, 0)),
                  st, st, st, st,
                  pl.BlockSpec((None, 4 * S5_SW, S5_CW), lambda k, i: (k, 0, 0)),
                  pl.BlockSpec((None, 1, S5_SBW), lambda k, i: (k, 0, 0)),
                  pl.BlockSpec((S5_CW, S5_SBW), lambda k, i: (0, 0))],
        out_specs=pl.BlockSpec((tt, LANES), lambda k, i: (i, k)),
        scratch_shapes=[pltpu.VMEM((S5_SBW, S5_SBW), BF16), pltpu.VMEM((4 * S5_SW, S5_SBW), BF16)],
        compiler_params=_params(("parallel", "arbitrary")),
    )(pm, w['tt'], xre, xim, zre, zim, w['vc'], w['dsk'], _s5_spread_matrix(S5_GROUP))


def _s5_glu_kernel(y_ref, w_ref, b_ref, o_ref):
    g = jax.nn.gelu(y_ref[...])
    t = jnp.dot(g.astype(BF16), w_ref[...], preferred_element_type=F32) + b_ref[...]
    o_ref[...] = (g * jax.nn.sigmoid(t)).astype(o_ref.dtype)


def _s5_glu(y, glu_w, layer, glu_b):
    n_tok = y.shape[0]
    tr = 512
    return pl.pallas_call(
        _s5_glu_kernel, out_shape=jax.ShapeDtypeStruct((n_tok, S5_WIDTH), BF16), grid=(n_tok // tr,),
        in_specs=[pl.BlockSpec((tr, S5_WIDTH), lambda i: (i, 0)),
                  pl.BlockSpec((None, S5_WIDTH, S5_WIDTH), lambda i: (layer, 0, 0)),
                  pl.BlockSpec((1, S5_WIDTH), lambda i: (0, 0))],
        out_specs=pl.BlockSpec((tr, S5_WIDTH), lambda i: (i, 0)),
        compiler_params=_params(("parallel",)),
    )(y, glu_w, glu_b[None, :])


def _s5(pm, segs, lw, glu_w, layer):
    w = _s5_weights(lw)
    fre, fim, bre, bim = _s5_in(pm, w['wa'])
    xre, xim, zre, zim = _s5_rec(fre, fim, bre, bim, w['lam_re'], w['lam_im'], segs)
    y = _s5_out(pm, w, xre, xim, zre, zim)
    return _s5_glu(y, glu_w, layer, lw['s5_glu_b'])


def _repack_kernel(a_ref, b_ref, o_ref, *, n_direct, shift):
    j = pl.program_id(2)

    @pl.when(j < n_direct)
    def _():
        o_ref[...] = a_ref[...].astype(o_ref.dtype)

    @pl.when(j >= n_direct)
    def _():
        n_tiles = a_ref.shape[1] // LANES
        lane = lax.broadcasted_iota(jnp.int32, b_ref.shape, 1)
        rolled = [pltpu.roll(a_ref[:, t * LANES:(t + 1) * LANES], LANES - shift, 1) for t in range(n_tiles)]
        rolled.append(pltpu.roll(b_ref[...], LANES - shift, 1))
        for t in range(n_tiles):
            o_ref[:, t * LANES:(t + 1) * LANES] = jnp.where(
                lane < LANES - shift, rolled[t], rolled[t + 1]).astype(o_ref.dtype)


def _drop_dt_columns(w_in):
    depth, d, n_in = w_in.shape
    shift = IN_SIZES[2]
    cut = IN_OFFS[2]
    tr, tc = 2048, 512
    n_out = n_in - shift
    per_tc = tc // LANES
    return pl.pallas_call(
        functools.partial(_repack_kernel, n_direct=cut // tc, shift=shift),
        out_shape=jax.ShapeDtypeStruct((depth, d, n_out), BF16),
        grid=(depth, d // tr, n_out // tc),
        in_specs=[pl.BlockSpec((None, tr, tc), lambda l, i, j: (l, i, j)),
                  pl.BlockSpec((None, tr, LANES), lambda l, i, j: (l, i, (j + 1) * per_tc))],
        out_specs=pl.BlockSpec((None, tr, tc), lambda l, i, j: (l, i, j)),
        compiler_params=_params(("parallel", "parallel", "parallel")),
    )(w_in, w_in)


def _matmul_weights(w_in, w_branch_ssd, w_branch_swa, w_branch_s5, w_branch_na, w_out, ffn_w1, ffn_w3, ffn_w2,
                    s5_glu_w):
    o = IN_OFFS
    w_main = _drop_dt_columns(w_in)
    w_dt_raw = w_in[:, :, o[2]:o[3]]
    zpad = jnp.zeros((DEPTH, D_MODEL, LANES - SSD_HEADS), F32)
    w_dt = jnp.concatenate([w_dt_raw[:, :, :SSD_HEADS], zpad, w_dt_raw[:, :, SSD_HEADS:], zpad],
                           axis=2).astype(BF16)
    w_branch = jnp.stack([w_branch_ssd, w_branch_swa, w_branch_s5, w_branch_na], axis=1).astype(BF16)
    return dict(main=w_main, dt=w_dt, branch=w_branch, out=w_out.astype(BF16),
                w1=ffn_w1.astype(BF16), w3=ffn_w3.astype(BF16), w2=ffn_w2.astype(BF16),
                glu=s5_glu_w.astype(BF16))


def _layer(xs, c8, lw, mw, ada_w, ada_b, layer, split_output):
    mod = _ada(c8, ada_w, ada_b, layer)
    mod4 = mod.reshape(8, 6, 1, D_MODEL)

    h = _norm_mod(xs, lw['norm1_g'], mod4, 1, 0)
    pm = _mm(h, mw['main'], layer, N_MIX, 1024, F32)
    dt_raw = _mm(h, mw['dt'], layer, 2 * LANES, 2 * LANES, F32)

    y_ssd = _ssd(pm, dt_raw, SEGS, lw)
    qs, ks, qn, kn, vn = _qk_prep(pm, SEGS, lw)
    y_swa = _swa(qs, ks, pm, lw['swa_sink'], SEGS)
    y_s5 = _s5(pm, SEGS, lw, mw['glu'], layer)
    y_na = _na(qn, kn, vn, lw['na_rpb'], GROUPS)
    merged = _merge(h, mw['main'], jnp.stack([y_ssd, y_swa, y_s5, y_na]), mw['branch'], layer)
    x = _mm_res(merged, mw['out'], layer, xs, mod4, 2, 1024 // len(xs), D_MODEL)

    h2 = _norm_mod((x,), lw['norm2_g'], mod4, 4, 3)
    u = _ffn_up(h2, mw['w1'], mw['w3'], layer)
    down = functools.partial(_mm_res, u, mw['w2'], layer, (x,), mod4, 5, 512, D_FF // 2)
    if not split_output:
        return down()
    n_first = GROUPS[1][0] // TM
    return down(rows=(0, n_first)), down(rows=(n_first, N_TOK // TM - n_first))


_LAYER_KEYS = ('ada_w', 'ada_b', 'norm1_g', 'norm2_g', 'w_in',
               'ssd_conv_w', 'ssd_conv_b', 'ssd_dt_bias', 'ssd_a_log', 'ssd_d', 'ssd_norm_g',
               'swa_q_norm_g', 'swa_k_norm_g', 'swa_sink',
               's5_a_re', 's5_a_im', 's5_log_step', 's5_b_re', 's5_b_im', 's5_c_re', 's5_c_im',
               's5_d', 's5_glu_w', 's5_glu_b',
               'na_q_norm_g', 'na_k_norm_g', 'na_rpb',
               'w_branch_ssd', 'w_branch_swa', 'w_branch_s5', 'w_branch_na', 'w_out',
               'ffn_w1', 'ffn_w3', 'ffn_w2')


def kernel(x_prompt, x_sample, c_prompt, c_sample, ada_w, ada_b, norm1_g, norm2_g, w_in, ssd_conv_w, ssd_conv_b, ssd_dt_bias, ssd_a_log, ssd_d, ssd_norm_g, swa_q_norm_g, swa_k_norm_g, swa_sink, s5_a_re, s5_a_im, s5_log_step, s5_b_re, s5_b_im, s5_c_re, s5_c_im, s5_d, s5_glu_w, s5_glu_b, na_q_norm_g, na_k_norm_g, na_rpb, w_branch_ssd, w_branch_swa, w_branch_s5, w_branch_na, w_out, ffn_w1, ffn_w3, ffn_w2):
    stacked = dict(zip(_LAYER_KEYS, (ada_w, ada_b, norm1_g, norm2_g, w_in,
                                     ssd_conv_w, ssd_conv_b, ssd_dt_bias, ssd_a_log, ssd_d, ssd_norm_g,
                                     swa_q_norm_g, swa_k_norm_g, swa_sink,
                                     s5_a_re, s5_a_im, s5_log_step, s5_b_re, s5_b_im, s5_c_re, s5_c_im,
                                     s5_d, s5_glu_w, s5_glu_b,
                                     na_q_norm_g, na_k_norm_g, na_rpb,
                                     w_branch_ssd, w_branch_swa, w_branch_s5, w_branch_na, w_out,
                                     ffn_w1, ffn_w3, ffn_w2)))
    xs = (x_prompt.reshape(BATCH * SEQ, D_MODEL), x_sample.reshape(DEC_BATCH * DEC_SEQ, D_MODEL))
    c8 = jnp.concatenate([c_prompt, c_sample, jnp.zeros((8 - N_SEQS, D_MODEL), F32)], axis=0)
    mw = _matmul_weights(w_in, w_branch_ssd, w_branch_swa, w_branch_s5, w_branch_na, w_out, ffn_w1, ffn_w3, ffn_w2,
                         s5_glu_w)
    for i in range(DEPTH):
        lw = {k: v[i] for k, v in stacked.items() if v.size < D_MODEL * D_MODEL}
        last = i == DEPTH - 1
        out = _layer(xs, c8, lw, mw, ada_w, ada_b, i, split_output=last)
        xs = out if last else (out,)
    return (xs[0].reshape(BATCH, SEQ, D_MODEL), xs[1].reshape(DEC_BATCH, DEC_SEQ, D_MODEL))
```

```python
import functools

import jax
import jax.numpy as jnp
import numpy as np
from jax import lax
from jax.experimental import pallas as pl
from jax.experimental.pallas import tpu as pltpu

D_MODEL = 4096
BATCH = 2
SEQ = 4096
DEPTH = 2
DEC_BATCH = 4
DEC_SEQ = 2048
N_TOK = BATCH * SEQ + DEC_BATCH * DEC_SEQ
N_SEQS = BATCH + DEC_BATCH
GROUPS = ((0, BATCH, SEQ), (BATCH * SEQ, DEC_BATCH, DEC_SEQ))
SEGS = tuple((row0 + i * l, l) for row0, b, l in GROUPS for i in range(b))

EPS = 1e-6
HEAD_DIM = 128
N_BRANCH = 4
SSD_HEADS = 16
SSD_HEAD_DIM = 64
SSD_INNER = SSD_HEADS * SSD_HEAD_DIM
SSD_GROUPS = 2
SSD_STATE = 128
SSD_CONV = 5
SSD_CHUNK = 128
SSD_CONV_DIM = SSD_INNER + 2 * SSD_GROUPS * SSD_STATE
SWA_HEADS = 8
SWA_KV_HEADS = 2
SWA_WIDTH = SWA_HEADS * HEAD_DIM
SWA_KV_WIDTH = SWA_KV_HEADS * HEAD_DIM
SWA_WINDOW = 128
SWA_BLOCK = 128
ROPE_THETA = 10000.0
S5_WIDTH = 1024
S5_GROUP = 16
S5_GROUPS = S5_WIDTH // S5_GROUP
S5_STATE = 64
NA_HEADS = 8
NA_WIDTH = NA_HEADS * HEAD_DIM
GRID_W = 64
NA_KR = 8
NA_KW = 16
D_FF = ((8 * D_MODEL + 3 * 256 - 1) // (3 * 256)) * 256
IN_SIZES = (SSD_INNER, SSD_CONV_DIM, 2 * SSD_HEADS,
            SWA_WIDTH, SWA_KV_WIDTH, SWA_KV_WIDTH,
            S5_WIDTH,
            NA_WIDTH, NA_WIDTH, NA_WIDTH,
            N_BRANCH * D_MODEL)
IN_OFFS = tuple(int(v) for v in np.cumsum((0,) + IN_SIZES))

LANES = 128
TM = 1024
N_MIX = IN_OFFS[10] - IN_SIZES[2]
VMEM_LIMIT = 56 * 1024 * 1024

COL_Z = 0
COL_XBC = COL_Z + SSD_INNER
COL_Q_SWA = COL_XBC + SSD_CONV_DIM
COL_K_SWA = COL_Q_SWA + SWA_WIDTH
COL_V_SWA = COL_K_SWA + SWA_KV_WIDTH
COL_U_S5 = COL_V_SWA + SWA_KV_WIDTH
COL_Q_NA = COL_U_S5 + S5_WIDTH
COL_K_NA = COL_Q_NA + NA_WIDTH
COL_V_NA = COL_K_NA + NA_WIDTH
COL_SSD_END = COL_XBC + SSD_CONV_DIM

F32 = jnp.float32
BF16 = jnp.bfloat16


def _batch_of_tile(i, tile):
    n_p = (BATCH * SEQ) // tile
    return jnp.where(i < n_p, i // (SEQ // tile), BATCH + (i - n_p) // (DEC_SEQ // tile))


def _params(sem):
    return pltpu.CompilerParams(dimension_semantics=sem, vmem_limit_bytes=VMEM_LIMIT)


def _ada_kernel(c_ref, w_ref, b_ref, o_ref):
    c = c_ref[...]
    a = (c * jax.nn.sigmoid(c)).astype(BF16)
    o_ref[...] = jnp.dot(a, w_ref[...].astype(BF16), preferred_element_type=F32) + b_ref[...]


def _ada(c8, ada_w, ada_b, layer):
    n = ada_w.shape[2]
    tn = 512
    return pl.pallas_call(
        _ada_kernel,
        out_shape=jax.ShapeDtypeStruct((8, n), F32),
        grid=(n // tn,),
        in_specs=[pl.BlockSpec((8, D_MODEL), lambda j: (0, 0)),
                  pl.BlockSpec((None, D_MODEL, tn), lambda j: (layer, 0, j)),
                  pl.BlockSpec((1, tn), lambda j: (0, j))],
        out_specs=pl.BlockSpec((8, tn), lambda j: (0, j)),
        compiler_params=_params(("parallel",)),
    )(c8, ada_w, ada_b[layer].reshape(1, n))


def _norm_mod_kernel(*refs, n_first):
    *x_refs, g_ref, scale_ref, shift_ref, o_ref = refs

    def body(x_ref):
        x = x_ref[...]
        y = x * lax.rsqrt(jnp.mean(x * x, axis=-1, keepdims=True) + EPS)
        y = y * g_ref[...]
        o_ref[...] = (y * (1.0 + scale_ref[...]) + shift_ref[...]).astype(o_ref.dtype)

    if len(x_refs) == 1:
        body(x_refs[0])
        return
    i = pl.program_id(0)
    pl.when(i < n_first)(lambda: body(x_refs[0]))
    pl.when(i >= n_first)(lambda: body(x_refs[1]))


def _group_specs(block, n_first):
    def first(i, j=0, *_):
        return (jnp.minimum(i, n_first - 1), jnp.where(i < n_first, j, 0))

    def second(i, j=0, *_):
        return (jnp.maximum(i - n_first, 0), jnp.where(i >= n_first, j, 0))

    return [pl.BlockSpec(block, first), pl.BlockSpec(block, second)]


def _norm_mod(xs, g, mod4, scale_idx, shift_idx):
    tr = 256
    n_first = xs[0].shape[0] // tr
    x_specs = ([pl.BlockSpec((tr, D_MODEL), lambda i: (i, 0))] if len(xs) == 1
               else _group_specs((tr, D_MODEL), n_first))
    return pl.pallas_call(
        functools.partial(_norm_mod_kernel, n_first=n_first),
        out_shape=jax.ShapeDtypeStruct((N_TOK, D_MODEL), BF16),
        grid=(N_TOK // tr,),
        in_specs=x_specs + [
            pl.BlockSpec((1, D_MODEL), lambda i: (0, 0)),
            pl.BlockSpec((None, None, 1, D_MODEL), lambda i: (_batch_of_tile(i, tr), scale_idx, 0, 0)),
            pl.BlockSpec((None, None, 1, D_MODEL), lambda i: (_batch_of_tile(i, tr), shift_idx, 0, 0))],
        out_specs=pl.BlockSpec((tr, D_MODEL), lambda i: (i, 0)),
        compiler_params=_params(("parallel",)),
    )(*xs, g.reshape(1, D_MODEL), mod4, mod4)


def _mm_kernel(a_ref, b_ref, o_ref):
    o_ref[...] = jnp.dot(a_ref[...], b_ref[...], preferred_element_type=F32).astype(o_ref.dtype)


def _mm(a, b, layer, n, tn, out_dtype):
    m, k = a.shape
    return pl.pallas_call(
        _mm_kernel,
        out_shape=jax.ShapeDtypeStruct((m, n), out_dtype),
        grid=(m // TM, n // tn),
        in_specs=[pl.BlockSpec((TM, k), lambda i, j: (i, 0)),
                  pl.BlockSpec((None, k, tn), lambda i, j: (layer, 0, j))],
        out_specs=pl.BlockSpec((TM, tn), lambda i, j: (i, j)),
        compiler_params=_params(("parallel", "parallel")),
    )(a, b)


def _merge_kernel(h_ref, wg_ref, m_ref, wb_ref, o_ref, acc_ref):
    b = pl.program_id(2)
    logits = jnp.dot(h_ref[...], wg_ref[...], preferred_element_type=F32)
    y = jnp.dot(m_ref[...], wb_ref[...], preferred_element_type=F32)
    contrib = jax.nn.sigmoid(logits) * y

    @pl.when(b == 0)
    def _():
        acc_ref[...] = contrib

    @pl.when(b > 0)
    def _():
        acc_ref[...] += contrib

    @pl.when(b == N_BRANCH - 1)
    def _():
        o_ref[...] = acc_ref[...].astype(o_ref.dtype)


def _merge(h, w_main, mix, wb, layer):
    tn = 512
    w = mix.shape[2]
    gate0 = N_MIX // tn
    per_branch = D_MODEL // tn
    return pl.pallas_call(
        _merge_kernel,
        out_shape=jax.ShapeDtypeStruct((N_TOK, D_MODEL), BF16),
        grid=(N_TOK // TM, D_MODEL // tn, N_BRANCH),
        in_specs=[pl.BlockSpec((TM, D_MODEL), lambda i, j, b: (i, 0)),
                  pl.BlockSpec((None, D_MODEL, tn), lambda i, j, b: (layer, 0, gate0 + b * per_branch + j)),
                  pl.BlockSpec((None, TM, w), lambda i, j, b: (b, i, 0)),
                  pl.BlockSpec((None, None, w, tn), lambda i, j, b: (layer, b, 0, j))],
        out_specs=pl.BlockSpec((TM, tn), lambda i, j, b: (i, j)),
        scratch_shapes=[pltpu.VMEM((TM, tn), F32)],
        compiler_params=_params(("parallel", "parallel", "arbitrary")),
    )(h, w_main, mix, wb)


def _mm_res_kernel(a_ref, b_ref, *rest, nk, n_first):
    *x_refs, gate_ref, o_ref, acc_ref = rest
    d = jnp.dot(a_ref[...], b_ref[...], preferred_element_type=F32)

    def finish(total):
        def store(x_ref):
            o_ref[...] = x_ref[...] + gate_ref[...] * total

        if len(x_refs) == 1:
            store(x_refs[0])
            return
        i = pl.program_id(0)
        pl.when(i < n_first)(functools.partial(store, x_refs[0]))
        pl.when(i >= n_first)(functools.partial(store, x_refs[1]))

    if nk == 1:
        finish(d)
        return
    k = pl.program_id(2)

    @pl.when(k == 0)
    def _():
        acc_ref[...] = d

    @pl.when(k > 0)
    def _():
        acc_ref[...] += d

    @pl.when(k == nk - 1)
    def _():
        finish(acc_ref[...])


def _mm_res(a, b, layer, xs, mod4, gate_idx, tm, tn, tk, rows=None):
    kdim = a.shape[1]
    n = b.shape[2]
    nk = kdim // tk
    off, m_tiles = rows if rows is not None else (0, a.shape[0] // tm)
    n_first = xs[0].shape[0] // tm
    x_specs = ([pl.BlockSpec((tm, tn), lambda i, j, k: (i + off, j))] if len(xs) == 1
               else _group_specs((tm, tn), n_first))
    return pl.pallas_call(
        functools.partial(_mm_res_kernel, nk=nk, n_first=n_first),
        out_shape=jax.ShapeDtypeStruct((m_tiles * tm, n), F32),
        grid=(m_tiles, n // tn, nk),
        in_specs=[pl.BlockSpec((tm, tk), lambda i, j, k: (i + off, k)),
                  pl.BlockSpec((None, tk, tn), lambda i, j, k: (layer, k, j))] + x_specs + [
                  pl.BlockSpec((None, None, 1, tn),
                               lambda i, j, k: (_batch_of_tile(i + off, tm), gate_idx, 0, j))],
        out_specs=pl.BlockSpec((tm, tn), lambda i, j, k: (i, j)),
        scratch_shapes=[pltpu.VMEM((tm, tn), F32)],
        compiler_params=_params(("parallel", "parallel", "arbitrary")),
    )(a, b, *xs, mod4)


def _ffn_up_kernel(h_ref, w1_ref, w3_ref, o_ref):
    h = h_ref[...]
    a = jnp.dot(h, w1_ref[...], preferred_element_type=F32)
    b = jnp.dot(h, w3_ref[...], preferred_element_type=F32)
    o_ref[...] = (a * jax.nn.sigmoid(a) * b).astype(o_ref.dtype)


def _ffn_up(h, w1, w3, layer):
    tn = 512
    n = w1.shape[2]
    return pl.pallas_call(
        _ffn_up_kernel,
        out_shape=jax.ShapeDtypeStruct((N_TOK, n), BF16),
        grid=(N_TOK // TM, pl.cdiv(n, tn)),
        in_specs=[pl.BlockSpec((TM, D_MODEL), lambda i, j: (i, 0)),
                  pl.BlockSpec((None, D_MODEL, tn), lambda i, j: (layer, 0, j)),
                  pl.BlockSpec((None, D_MODEL, tn), lambda i, j: (layer, 0, j))],
        out_specs=pl.BlockSpec((TM, tn), lambda i, j: (i, j)),
        compiler_params=_params(("parallel", "parallel")),
    )(h, w1, w3)


def _tile_tables(segs, tile):
    pos, flags = [], []
    for _, length in segs:
        n = length // tile
        for j in range(n):
            pos.append(j)
            flags.append((1 if j == 0 else 0) | (2 if j == n - 1 else 0))
    return np.asarray(pos, np.int32), np.asarray(flags, np.int32)


def _head_slice(h):
    return slice(h * HEAD_DIM, (h + 1) * HEAD_DIM)


def _qk_prep_kernel(pos_ref, qs_lo_ref, qs_hi_ref, ks_ref, qn_ref, kn_ref, vn_ref, cos_ref, sin_ref, g_ref,
                    oqs_ref, oks_ref, oqn_ref, okn_ref, ovn_ref):
    del pos_ref
    half_heads = SWA_HEADS // 2
    cos = cos_ref[...]
    sin = sin_ref[...]

    def hnorm(x, g):
        return x * lax.rsqrt(jnp.mean(x * x, axis=-1, keepdims=True) + EPS) * g

    def rope(x):
        return x * cos + pltpu.roll(x, HEAD_DIM // 2, 1) * sin

    for h in range(SWA_HEADS):
        src = qs_lo_ref if h < half_heads else qs_hi_ref
        oqs_ref[:, _head_slice(h)] = rope(hnorm(src[:, _head_slice(h % half_heads)], g_ref[0:1, :])).astype(BF16)
    for h in range(SWA_KV_HEADS):
        oks_ref[:, _head_slice(h)] = rope(hnorm(ks_ref[:, _head_slice(h)], g_ref[1:2, :])).astype(BF16)
    for h in range(NA_HEADS):
        oqn_ref[:, _head_slice(h)] = hnorm(qn_ref[:, _head_slice(h)], g_ref[2:3, :]).astype(BF16)
        okn_ref[:, _head_slice(h)] = hnorm(kn_ref[:, _head_slice(h)], g_ref[3:4, :]).astype(BF16)
    ovn_ref[...] = vn_ref[...].astype(BF16)


def _rope_tables(max_len):
    half = HEAD_DIM // 2
    inv_freq = ROPE_THETA ** (-jnp.arange(half, dtype=F32) / half)
    ang = jnp.arange(max_len, dtype=F32)[:, None] * inv_freq[None, :]
    cos, sin = jnp.cos(ang), jnp.sin(ang)
    return jnp.concatenate([cos, cos], axis=1), jnp.concatenate([-sin, sin], axis=1)


def _qk_prep(pm, segs, lw):
    n_tok = pm.shape[0]
    tr = 256
    pos, _ = _tile_tables(segs, tr)
    cos, sin = _rope_tables(max(l for _, l in segs))
    gains = jnp.concatenate([lw['swa_q_norm_g'][None], lw['swa_k_norm_g'][None],
                             lw['na_q_norm_g'][None], lw['na_k_norm_g'][None],
                             jnp.zeros((4, HEAD_DIM), F32)], axis=0)
    wide = lambda cb: pl.BlockSpec((tr, 1024), lambda i, p: (i, cb))
    narrow = lambda cb: pl.BlockSpec((tr, 256), lambda i, p: (i, cb))
    grid_spec = pltpu.PrefetchScalarGridSpec(
        num_scalar_prefetch=1, grid=(n_tok // tr,),
        in_specs=[pl.BlockSpec((tr, 512), lambda i, p: (i, COL_Q_SWA // 512)),
                  pl.BlockSpec((tr, 512), lambda i, p: (i, COL_Q_SWA // 512 + 1)),
                  narrow(COL_K_SWA // 256), wide(COL_Q_NA // 1024),
                  wide(COL_K_NA // 1024), wide(COL_V_NA // 1024),
                  pl.BlockSpec((tr, HEAD_DIM), lambda i, p: (p[i], 0)),
                  pl.BlockSpec((tr, HEAD_DIM), lambda i, p: (p[i], 0)),
                  pl.BlockSpec((8, HEAD_DIM), lambda i, p: (0, 0))],
        out_specs=[pl.BlockSpec((tr, 1024), lambda i, p: (i, 0)),
                   pl.BlockSpec((tr, 256), lambda i, p: (i, 0)),
                   pl.BlockSpec((tr, 1024), lambda i, p: (i, 0)),
                   pl.BlockSpec((tr, 1024), lambda i, p: (i, 0)),
                   pl.BlockSpec((tr, 1024), lambda i, p: (i, 0))])
    sds = lambda w: jax.ShapeDtypeStruct((n_tok, w), BF16)
    return pl.pallas_call(
        _qk_prep_kernel, grid_spec=grid_spec,
        out_shape=[sds(1024), sds(256), sds(1024), sds(1024), sds(1024)],
        compiler_params=_params(("parallel",)),
    )(jnp.asarray(pos), pm, pm, pm, pm, pm, pm, cos, sin, gains)


SWA_TQ = 512
SWA_GRP = SWA_HEADS // SWA_KV_HEADS


def _swa_kernel(flags_ref, sink_ref, q_ref, kc_ref, kp_ref, kn_ref, vc_ref, vp_ref, vn_ref, o_ref, *, tq):
    fl = flags_ref[pl.program_id(0)]
    lo = jnp.where((fl & 1) > 0, SWA_BLOCK, 0)
    hi = jnp.where((fl & 2) > 0, 2 * SWA_BLOCK, 3 * SWA_BLOCK)
    nqb = tq // SWA_BLOCK
    m_rows = SWA_GRP * SWA_BLOCK
    row = lax.broadcasted_iota(jnp.int32, (m_rows, 3 * SWA_BLOCK), 0) & (SWA_BLOCK - 1)
    col = lax.broadcasted_iota(jnp.int32, (m_rows, 3 * SWA_BLOCK), 1)
    band = (col >= row) & (col <= row + 2 * SWA_WINDOW)
    scale = HEAD_DIM ** -0.5
    for g in range(SWA_KV_HEADS):
        ks = _head_slice(g)
        k_ext = jnp.concatenate([kp_ref[:, ks], kc_ref[:, ks], kn_ref[:, ks]], axis=0)
        v_ext = jnp.concatenate([vp_ref[:, ks], vc_ref[:, ks], vn_ref[:, ks]], axis=0).astype(BF16)
        sk = jnp.concatenate([jnp.full((SWA_BLOCK, 1), sink_ref[g * SWA_GRP + h], F32)
                              for h in range(SWA_GRP)], axis=0)
        for qb in range(nqb):
            rows = slice(qb * SWA_BLOCK, (qb + 1) * SWA_BLOCK)
            q = jnp.concatenate([q_ref[rows, _head_slice(g * SWA_GRP + h)] for h in range(SWA_GRP)], axis=0)
            keys = k_ext[qb * SWA_BLOCK:(qb + 3) * SWA_BLOCK]
            vals = v_ext[qb * SWA_BLOCK:(qb + 3) * SWA_BLOCK]
            s = lax.dot_general(q, keys, (((1,), (1,)), ((), ())), preferred_element_type=F32) * scale
            mask = band
            if qb == 0:
                mask = mask & (col >= lo)
            if qb == nqb - 1:
                mask = mask & (col < hi)
            s = jnp.where(mask, s, -jnp.inf)
            m = jnp.maximum(jnp.max(s, axis=-1, keepdims=True), sk)
            p = jnp.exp(s - m)
            denom = jnp.sum(p, axis=-1, keepdims=True) + jnp.exp(sk - m)
            o = jnp.dot((p / denom).astype(BF16), vals, preferred_element_type=F32)
            for h in range(SWA_GRP):
                o_ref[rows, _head_slice(g * SWA_GRP + h)] = o[h * SWA_BLOCK:(h + 1) * SWA_BLOCK].astype(o_ref.dtype)


def _swa(qs, ks, pm, sink, segs):
    n_tok = qs.shape[0]
    tq = min(SWA_TQ, min(l for _, l in segs))
    _, flags = _tile_tables(segs, tq)
    nb = tq // SWA_BLOCK
    last_blk = n_tok // SWA_BLOCK - 1
    prev_map = lambda cb: (lambda i, f: (jnp.maximum(i * nb - 1, 0), cb))
    next_map = lambda cb: (lambda i, f: (jnp.minimum((i + 1) * nb, last_blk), cb))
    vcb = COL_V_SWA // SWA_KV_WIDTH
    grid_spec = pltpu.PrefetchScalarGridSpec(
        num_scalar_prefetch=1, grid=(n_tok // tq,),
        in_specs=[pl.BlockSpec(memory_space=pltpu.SMEM),
                  pl.BlockSpec((tq, SWA_WIDTH), lambda i, f: (i, 0)),
                  pl.BlockSpec((tq, SWA_KV_WIDTH), lambda i, f: (i, 0)),
                  pl.BlockSpec((SWA_BLOCK, SWA_KV_WIDTH), prev_map(0)),
                  pl.BlockSpec((SWA_BLOCK, SWA_KV_WIDTH), next_map(0)),
                  pl.BlockSpec((tq, SWA_KV_WIDTH), lambda i, f: (i, vcb)),
                  pl.BlockSpec((SWA_BLOCK, SWA_KV_WIDTH), prev_map(vcb)),
                  pl.BlockSpec((SWA_BLOCK, SWA_KV_WIDTH), next_map(vcb))],
        out_specs=pl.BlockSpec((tq, SWA_WIDTH), lambda i, f: (i, 0)))
    return pl.pallas_call(
        functools.partial(_swa_kernel, tq=tq), grid_spec=grid_spec,
        out_shape=jax.ShapeDtypeStruct((n_tok, SWA_WIDTH), BF16),
        compiler_params=_params(("parallel",)),
    )(jnp.asarray(flags), sink, qs, ks, ks, ks, pm, pm, pm)


NA_R = NA_KR // 2
NA_WIN = NA_R + NA_KR


def _na_bias_table(rpb):
    off = np.array([np.zeros(NA_R, int), np.arange(NA_R), np.full(NA_R, NA_R)])
    p_of = np.array([np.arange(NA_R), np.full(NA_R, NA_KR // 2), NA_KR // 2 + np.arange(NA_R)])
    j = np.arange(NA_WIN)[None, None, :] - off[:, :, None]
    row_ok = (j >= 0) & (j < NA_KR)
    dr = np.clip(j - p_of[:, :, None] + (NA_KR - 1), 0, 2 * NA_KR - 2)
    qc = np.arange(GRID_W)
    kc = np.arange(GRID_W)
    dc = np.clip(kc[None, :] - qc[:, None], -(NA_KW - 1), NA_KW - 1) + (NA_KW - 1)
    col_start = np.clip(qc - NA_KW // 2, 0, GRID_W - NA_KW)
    valid = (kc[None, :] >= col_start[:, None]) & (kc[None, :] < col_start[:, None] + NA_KW)
    hp = lax.Precision.HIGHEST
    sel_dc = jnp.asarray(np.eye(2 * NA_KW - 1, dtype=np.float32)[dc.reshape(-1)])
    sel_dr = jnp.asarray(np.eye(2 * NA_KR - 1, dtype=np.float32)[dr.reshape(-1)])
    cols = jnp.einsum('hab,xb->hax', rpb, sel_dc, precision=hp)
    bias = jnp.einsum('ya,hax->yhx', sel_dr, cols, precision=hp)
    bias = bias.reshape(3, NA_R, NA_WIN, NA_HEADS, GRID_W, GRID_W)
    ok = row_ok[:, :, :, None, None, None] & valid[None, None, None, None, :, :]
    bias = jnp.where(ok, bias, -jnp.inf)
    return jnp.transpose(bias, (0, 3, 1, 4, 2, 5)).reshape(3, NA_HEADS, NA_R * GRID_W, NA_WIN * GRID_W)


def _na_kernel(q_ref, k_ref, v_ref, bias_ref, o_ref, *, rows):
    first_row = jnp.clip(pl.program_id(1) * NA_R - NA_KR // 2, 0, rows - NA_WIN)
    keys = pl.ds(pl.multiple_of(first_row * GRID_W, GRID_W), NA_WIN * GRID_W)
    scale = HEAD_DIM ** -0.5
    for h in range(NA_HEADS):
        s = lax.dot_general(q_ref[:, _head_slice(h)], k_ref[keys, _head_slice(h)], (((1,), (1,)), ((), ())),
                            preferred_element_type=F32) * scale
        s = s + bias_ref[h]
        p = jnp.exp(s - jnp.max(s, axis=-1, keepdims=True))
        p = p / jnp.sum(p, axis=-1, keepdims=True)
        o = jnp.dot(p.astype(BF16), v_ref[keys, _head_slice(h)], preferred_element_type=F32)
        o_ref[:, _head_slice(h)] = o.astype(o_ref.dtype)


def _na_group(qn, kn, vn, bias, row0, b, l):
    rows = l // GRID_W
    nsteps = rows // NA_R
    tq = NA_R * GRID_W
    kind = lambda st: jnp.where(st == 0, 0, jnp.where(st == nsteps - 1, 2, 1))
    seq_spec = pl.BlockSpec((l, NA_WIDTH), lambda bi, st: (row0 // l + bi, 0))
    return pl.pallas_call(
        functools.partial(_na_kernel, rows=rows),
        out_shape=jax.ShapeDtypeStruct((b * l, NA_WIDTH), BF16),
        grid=(b, nsteps),
        in_specs=[pl.BlockSpec((tq, NA_WIDTH), lambda bi, st: (row0 // tq + bi * nsteps + st, 0)),
                  seq_spec, seq_spec,
                  pl.BlockSpec((None, NA_HEADS, tq, NA_WIN * GRID_W), lambda bi, st: (kind(st), 0, 0, 0))],
        out_specs=pl.BlockSpec((tq, NA_WIDTH), lambda bi, st: (bi * nsteps + st, 0)),
        compiler_params=_params(("parallel", "arbitrary")),
    )(qn, kn, vn, bias)


def _na(qn, kn, vn, rpb, groups):
    bias = _na_bias_table(rpb)
    return jnp.concatenate([_na_group(qn, kn, vn, bias, row0, b, l) for row0, b, l in groups], axis=0)


def _ssd_prep_kernel(flags_ref, xc_ref, xp_ref, xn_ref, dt_ref, w_ref, b_ref, dtb_ref,
                     xs_ref, bc_ref, dtf_ref, *, tr):
    fl = flags_ref[pl.program_id(0)]
    halo = 8
    xbc = slice(COL_XBC, COL_SSD_END)
    xp = jnp.where((fl & 1) > 0, 0.0, xp_ref[:, xbc])
    xn = jnp.where((fl & 2) > 0, 0.0, xn_ref[:, xbc])
    ext = jnp.concatenate([xp, xc_ref[:, xbc], xn], axis=0)
    n = tr + 2 * halo
    acc = None
    for k in range(SSD_CONV):
        sh = (SSD_CONV // 2 - k) % n
        xk = ext if sh == 0 else pltpu.roll(ext, sh, 0)
        term = xk[halo:halo + tr] * w_ref[k:k + 1, :]
        acc = term if acc is None else acc + term
    acc = acc + b_ref[...]
    y = acc * jax.nn.sigmoid(acc)
    xs_ref[...] = y[:, :SSD_INNER]
    bc_ref[...] = y[:, SSD_INNER:]
    t = dt_ref[...] + dtb_ref[...]
    dtf_ref[...] = jnp.maximum(t, 0.0) + jnp.log1p(jnp.exp(-jnp.abs(t)))


def _dt_pad(v):
    z = jnp.zeros((LANES - SSD_HEADS,), v.dtype)
    return jnp.concatenate([v[0], z, v[1], z])


def _ssd_prep(pm, dt_raw, segs, lw):
    n_tok = pm.shape[0]
    tr = 256
    _, flags = _tile_tables(segs, tr)
    nb8 = tr // 8
    last8 = n_tok // 8 - 1
    w8 = jnp.concatenate([lw['ssd_conv_w'], jnp.zeros((8 - SSD_CONV, SSD_CONV_DIM), F32)], axis=0)
    dtb = _dt_pad(lw['ssd_dt_bias'])[None, :]
    grid_spec = pltpu.PrefetchScalarGridSpec(
        num_scalar_prefetch=1, grid=(n_tok // tr,),
        in_specs=[pl.BlockSpec((tr, COL_SSD_END), lambda i, f: (i, 0)),
                  pl.BlockSpec((8, COL_SSD_END), lambda i, f: (jnp.maximum(i * nb8 - 1, 0), 0)),
                  pl.BlockSpec((8, COL_SSD_END), lambda i, f: (jnp.minimum((i + 1) * nb8, last8), 0)),
                  pl.BlockSpec((tr, 2 * LANES), lambda i, f: (i, 0)),
                  pl.BlockSpec((8, SSD_CONV_DIM), lambda i, f: (0, 0)),
                  pl.BlockSpec((1, SSD_CONV_DIM), lambda i, f: (0, 0)),
                  pl.BlockSpec((1, 2 * LANES), lambda i, f: (0, 0))],
        out_specs=[pl.BlockSpec((tr, SSD_INNER), lambda i, f: (i, 0)),
                   pl.BlockSpec((tr, 2 * SSD_GROUPS * SSD_STATE), lambda i, f: (i, 0)),
                   pl.BlockSpec((tr, 2 * LANES), lambda i, f: (i, 0))])
    return pl.pallas_call(
        functools.partial(_ssd_prep_kernel, tr=tr), grid_spec=grid_spec,
        out_shape=[jax.ShapeDtypeStruct((n_tok, SSD_INNER), F32),
                   jax.ShapeDtypeStruct((n_tok, 2 * SSD_GROUPS * SSD_STATE), F32),
                   jax.ShapeDtypeStruct((n_tok, 2 * LANES), F32)],
        compiler_params=_params(("parallel",)),
    )(jnp.asarray(flags), pm, pm, pm, dt_raw, w8, lw['ssd_conv_b'][None, :], dtb)


def _split3(x):
    hi = x.astype(BF16)
    r = x - hi.astype(F32)
    mid = r.astype(BF16)
    lo = (r - mid.astype(F32)).astype(BF16)
    return hi, mid, lo


def _dot_sel_l(sel, x):
    hi, mid, lo = _split3(x)
    d = lambda t: jnp.dot(sel, t, preferred_element_type=F32)
    return d(lo) + d(mid) + d(hi)


def _dot_sel_r(x, sel):
    hi, mid, lo = _split3(x)
    d = lambda t: jnp.dot(t, sel, preferred_element_type=F32)
    return d(lo) + d(mid) + d(hi)


def _ssd_scan_kernel(order_ref, reset_ref, xs_ref, bc_ref, dt_ref, alog_ref, e_ref, *rest, bwd):
    if bwd:
        yf_ref, z_ref, dsk_ref, g_ref, o_ref, state_ref = rest
    else:
        o_ref, state_ref = rest
    del order_ref
    q = SSD_CHUNK
    n_pairs = SSD_HEADS // 2
    gs = SSD_GROUPS * SSD_STATE

    @pl.when(reset_ref[pl.program_id(0)] > 0)
    def _():
        state_ref[...] = jnp.zeros_like(state_ref)

    dt = dt_ref[...]
    da = dt * (-jnp.exp(alog_ref[...]))
    r_i = lax.broadcasted_iota(jnp.int32, (q, q), 0)
    c_i = lax.broadcasted_iota(jnp.int32, (q, q), 1)
    incl = (r_i <= c_i) if bwd else (r_i >= c_i)
    tri = jnp.where(incl, 1.0, 0.0).astype(BF16)
    cs = _dot_sel_l(tri, da)
    cs_t = cs.T
    e = e_ref[...]
    dt_x = _dot_sel_r(dt, e)
    cs_x = _dot_sel_r(cs, e)
    total = cs_x[0:1, :] if bwd else cs_x[q - 1:q, :]
    xs = xs_ref[...]
    xdt = xs * dt_x
    w_state = (xdt * jnp.exp(total - cs_x)).astype(BF16)
    xdt_b = xdt.astype(BF16)
    ecs_x = jnp.exp(cs_x)
    etot = jnp.exp(total)
    bc = bc_ref[...]
    b_t = bc[:, :gs].T
    lane = lax.broadcasted_iota(jnp.int32, (q, LANES), 1)
    ys = []
    for g in range(SSD_GROUPS):
        b_g = bc[:, g * SSD_STATE:(g + 1) * SSD_STATE].astype(BF16)
        c_g = bc[:, gs + g * SSD_STATE:gs + (g + 1) * SSD_STATE].astype(BF16)
        bt_g = b_t[g * SSD_STATE:(g + 1) * SSD_STATE, :].astype(BF16)
        cb = lax.dot_general(c_g, b_g, (((1,), (1,)), ((), ())), preferred_element_type=F32)
        for j in range(n_pairs // SSD_GROUPS):
            pair = g * (n_pairs // SSD_GROUPS) + j
            lanes = slice(pair * LANES, (pair + 1) * LANES)
            halves = []
            for hh in range(2):
                h = 2 * pair + hh
                diff = jnp.broadcast_to(cs[:, h:h + 1], (q, q)) - jnp.broadcast_to(cs_t[h:h + 1, :], (q, q))
                decay = jnp.where(incl, jnp.exp(diff), 0.0)
                halves.append(jnp.dot((cb * decay).astype(BF16), xdt_b[:, lanes], preferred_element_type=F32))
            y_diag = jnp.where(lane < SSD_HEAD_DIM, halves[0], halves[1])
            s_prev = state_ref[pair]
            y_off = jnp.dot(c_g, s_prev.astype(BF16), preferred_element_type=F32) * ecs_x[:, lanes]
            contrib = jnp.dot(bt_g, w_state[:, lanes], preferred_element_type=F32)
            state_ref[pair] = s_prev * etot[:, lanes] + contrib
            ys.append(y_diag + y_off)
    y = jnp.concatenate(ys, axis=1)
    if not bwd:
        o_ref[...] = y
        return
    y = yf_ref[...] + y + dsk_ref[...] * xs
    z = z_ref[...]
    y = y * (z * jax.nn.sigmoid(z))
    y = y * lax.rsqrt(jnp.mean(y * y, axis=-1, keepdims=True) + EPS)
    o_ref[...] = (y * g_ref[...]).astype(o_ref.dtype)


def _ssd_scan(xs, bc, dtf, alog, e_mat, segs, bwd, extra=()):
    n_tok = xs.shape[0]
    q = SSD_CHUNK
    _, flags = _tile_tables(segs, q)
    n = n_tok // q
    order = np.arange(n, dtype=np.int32)[::-1].copy() if bwd else np.arange(n, dtype=np.int32)
    reset = ((flags[order] & (2 if bwd else 1)) > 0).astype(np.int32)
    d = 1 if bwd else 0
    row = lambda w, cb=0: pl.BlockSpec((q, w), lambda i, o, r: (o[i], cb))
    const = lambda shape: pl.BlockSpec(shape, lambda i, o, r: (0,) * len(shape))
    in_specs = [row(SSD_INNER), row(2 * SSD_GROUPS * SSD_STATE), row(LANES, d),
                pl.BlockSpec((None, 1, LANES), lambda i, o, r: (d, 0, 0)), const((LANES, SSD_INNER))]
    if bwd:
        in_specs += [row(SSD_INNER), row(SSD_INNER, COL_Z // SSD_INNER), const((1, SSD_INNER)),
                     const((1, SSD_INNER))]
    grid_spec = pltpu.PrefetchScalarGridSpec(
        num_scalar_prefetch=2, grid=(n,), in_specs=in_specs, out_specs=row(SSD_INNER),
        scratch_shapes=[pltpu.VMEM((SSD_HEADS // 2, SSD_STATE, LANES), F32)])
    return pl.pallas_call(
        functools.partial(_ssd_scan_kernel, bwd=bwd), grid_spec=grid_spec,
        out_shape=jax.ShapeDtypeStruct((n_tok, SSD_INNER), BF16 if bwd else F32),
        compiler_params=_params(("arbitrary",)),
    )(jnp.asarray(order), jnp.asarray(reset), xs, bc, dtf, alog, e_mat, *extra)


def _ssd(pm, dt_raw, segs, lw):
    xs, bc, dtf = _ssd_prep(pm, dt_raw, segs, lw)
    alog = _dt_pad(lw['ssd_a_log']).reshape(2, 1, LANES)
    e_np = np.zeros((LANES, SSD_INNER), np.float32)
    for h in range(SSD_HEADS):
        e_np[h, h * SSD_HEAD_DIM:(h + 1) * SSD_HEAD_DIM] = 1.0
    e_mat = jnp.asarray(e_np, BF16)
    y_fwd = _ssd_scan(xs, bc, dtf, alog, e_mat, segs, False)
    dsk = jnp.repeat(lw['ssd_d'], SSD_HEAD_DIM)[None, :]
    return _ssd_scan(xs, bc, dtf, alog, e_mat, segs, True,
                     extra=(y_fwd, pm, dsk, lw['ssd_norm_g'][None, :]))


S5_Q = 16
S5_SB = LANES // S5_GROUP
S5_NSB = S5_GROUPS // S5_SB
S5_SBW = S5_Q * LANES
S5_SW = S5_SB * S5_STATE
S5_NS = S5_GROUPS * S5_STATE
S5_CW = S5_Q * S5_GROUP
assert 4 * S5_STATE == S5_CW


def _s5_weights(lw):
    hp = lax.Precision.HIGHEST
    qn, g, p, c = S5_Q, S5_GROUPS, S5_STATE, S5_GROUP
    nsb, sb = S5_NSB, S5_SB
    cmul = lambda a, b: (a[0] * b[0] - a[1] * b[1], a[0] * b[1] + a[1] * b[0])
    a_re, a_im = lw['s5_a_re'], lw['s5_a_im']
    step = jnp.exp(lw['s5_log_step'])[..., None]
    mag = jnp.exp(a_re * step)
    lam_bar = (mag * jnp.cos(a_im * step), mag * jnp.sin(a_im * step))
    den = a_re * a_re + a_im * a_im
    coef = cmul((lam_bar[0] - 1.0, lam_bar[1]), (a_re / den, -a_im / den))
    b_bar = cmul((coef[0][..., None], coef[1][..., None]),
                 (lw['s5_b_re'][None], lw['s5_b_im'][None]))
    c_c = (lw['s5_c_re'], lw['s5_c_im'])
    pows = [(jnp.ones_like(mag), jnp.zeros_like(mag))]
    for _ in range(qn):
        pows.append(cmul(pows[-1], lam_bar))
    pw = (jnp.stack([t[0] for t in pows], axis=1), jnp.stack([t[1] for t in pows], axis=1))

    t1 = cmul((c_c[0][:, None], c_c[1][:, None]),
              (pw[0][:, :qn, :, None, :], pw[1][:, :qn, :, None, :]))
    lag = jnp.einsum('dmgiq,dgqj->dmgij', jnp.concatenate([t1[0], -t1[1]], axis=-1),
                     jnp.concatenate([b_bar[0], b_bar[1]], axis=2), precision=hp)

    s_idx = np.arange(qn)[:, None]
    t_idx = np.arange(qn)[None, :]
    df = t_idx - s_idx
    kf = lag[0][np.clip(df, 0, qn - 1)] * jnp.asarray(df >= 0, F32)[:, :, None, None, None]
    kb = lag[1][np.clip(-df, 0, qn - 1)] * jnp.asarray(df <= 0, F32)[:, :, None, None, None]
    tt = jnp.transpose(kf + kb, (2, 0, 4, 1, 3))
    by_block_row = lambda a: jnp.transpose(a.reshape(nsb, sb, qn, c, S5_CW), (0, 2, 1, 3, 4)).reshape(
        nsb, S5_SBW, S5_CW)
    tt = by_block_row(tt)

    def in_op(d, powers):
        w = cmul((pw[0][d, powers][:, :, :, None], pw[1][d, powers][:, :, :, None]),
                 (b_bar[0][d][None], b_bar[1][d][None]))
        return tuple(jnp.transpose(t, (1, 0, 3, 2)) for t in w)

    wf = in_op(0, qn - 1 - np.arange(qn))
    wb = in_op(1, np.arange(qn))
    wa = by_block_row(jnp.stack([wf[0], wf[1], wb[0], wb[1]], axis=3))

    def out_op(d, powers):
        return cmul((jnp.transpose(c_c[0][d], (0, 2, 1))[:, :, None, :],
                     jnp.transpose(c_c[1][d], (0, 2, 1))[:, :, None, :]),
                    (jnp.transpose(pw[0][d, powers], (1, 2, 0))[:, :, :, None],
                     jnp.transpose(pw[1][d, powers], (1, 2, 0))[:, :, :, None]))

    vf = out_op(0, 1 + np.arange(qn))
    vb = out_op(1, qn - np.arange(qn))
    v4 = jnp.stack([vf[0], -vf[1], vb[0], -vb[1]], axis=0)
    vc = jnp.transpose(v4.reshape(4, nsb, sb * p, S5_CW), (1, 0, 2, 3)).reshape(nsb, 4 * S5_SW, S5_CW)

    lam_q = (pw[0][:, qn].reshape(2, 1, S5_NS), pw[1][:, qn].reshape(2, 1, S5_NS))
    dsk = jnp.broadcast_to(lw['s5_d'].reshape(nsb, 1, 1, LANES), (nsb, 1, qn, LANES)).reshape(nsb, 1, S5_SBW)
    return dict(tt=tt.astype(BF16), wa=wa.astype(BF16), vc=vc.astype(BF16),
                lam_re=lam_q[0], lam_im=lam_q[1], dsk=dsk)


S5_TT = 4096


def _s5_blocks(u_ref, mt):
    return jnp.concatenate([u_ref[pl.ds(t, mt, stride=S5_Q), :] for t in range(S5_Q)], axis=1)


def _s5_spread_matrix(inner):
    src = np.arange(S5_CW)
    dst = np.arange(S5_SBW)
    same = (src[:, None] // inner == dst[None, :] // (S5_SB * inner)) & (src[:, None] % inner == dst[None, :] % inner)
    return jnp.asarray(same, BF16)


def _s5_expand(c_ref, spread_ref, out_ref, row_inner, col_inner):
    rows = S5_CW
    col_g = (lax.broadcasted_iota(jnp.int32, (rows, S5_SBW), 1) // col_inner) % S5_SB
    row_l = lax.broadcasted_iota(jnp.int32, (rows, S5_SBW), 0)
    for r0 in range(0, S5_SBW, rows):
        row_g = ((row_l + r0) // row_inner) % S5_SB
        full = jnp.dot(c_ref[r0:r0 + rows, :], spread_ref[...], preferred_element_type=F32)
        out_ref[r0:r0 + rows, :] = jnp.where(row_g == col_g, full, 0.0).astype(out_ref.dtype)


def _s5_in_kernel(u_ref, wc_ref, spread_ref, fre_ref, fim_ref, bre_ref, bim_ref, w_ref, *, mt):
    @pl.when(pl.program_id(1) == 0)
    def _():
        _s5_expand(wc_ref, spread_ref, w_ref, S5_GROUP, S5_STATE)

    r = jnp.dot(_s5_blocks(u_ref, mt).astype(BF16), w_ref[...], preferred_element_type=F32)
    fre_ref[...] = r[:, 0 * S5_SW:1 * S5_SW]
    fim_ref[...] = r[:, 1 * S5_SW:2 * S5_SW]
    bre_ref[...] = r[:, 2 * S5_SW:3 * S5_SW]
    bim_ref[...] = r[:, 3 * S5_SW:4 * S5_SW]


def _s5_in(pm, wa):
    n_tok = pm.shape[0]
    tt = min(S5_TT, n_tok)
    mt = tt // S5_Q
    ucb = COL_U_S5 // LANES
    st = pl.BlockSpec((mt, S5_SW), lambda k, i: (i, k))
    sds = jax.ShapeDtypeStruct((n_tok // S5_Q, S5_NS), F32)
    return pl.pallas_call(
        functools.partial(_s5_in_kernel, mt=mt), out_shape=[sds] * 4, grid=(S5_NSB, n_tok // tt),
        in_specs=[pl.BlockSpec((tt, LANES), lambda k, i: (i, ucb + k)),
                  pl.BlockSpec((None, S5_SBW, S5_CW), lambda k, i: (k, 0, 0)),
                  pl.BlockSpec((S5_CW, S5_SBW), lambda k, i: (0, 0))],
        out_specs=[st] * 4,
        scratch_shapes=[pltpu.VMEM((S5_SBW, 4 * S5_SW), BF16)],
        compiler_params=_params(("parallel", "arbitrary")),
    )(pm, wa, _s5_spread_matrix(S5_STATE))


def _s5_rec_kernel(flags_ref, fre_ref, fim_ref, bre_ref, bim_ref, lam_re_ref, lam_im_ref,
                   xre_ref, xim_ref, zre_ref, zim_ref, state_ref, *, tc):
    i = pl.program_id(0)
    n = pl.num_programs(0)

    @pl.when((flags_ref[i] & 1) > 0)
    def _():
        state_ref[0:2] = jnp.zeros((2, 1, S5_NS), F32)

    @pl.when((flags_ref[n - 1 - i] & 2) > 0)
    def _():
        state_ref[2:4] = jnp.zeros((2, 1, S5_NS), F32)

    flr, fli = lam_re_ref[0], lam_im_ref[0]
    blr, bli = lam_re_ref[1], lam_im_ref[1]

    def body(k, carry):
        xr, xi, zr, zi = carry
        cf = pl.ds(k, 1)
        cb = pl.ds(tc - 1 - k, 1)
        xre_ref[cf, :] = xr
        xim_ref[cf, :] = xi
        zre_ref[cb, :] = zr
        zim_ref[cb, :] = zi
        return (flr * xr - fli * xi + fre_ref[cf, :], flr * xi + fli * xr + fim_ref[cf, :],
                blr * zr - bli * zi + bre_ref[cb, :], blr * zi + bli * zr + bim_ref[cb, :])

    out = lax.fori_loop(0, tc, body, tuple(state_ref[j] for j in range(4)))
    for j in range(4):
        state_ref[j] = out[j]


def _s5_rec(fre, fim, bre, bim, lam_re, lam_im, segs):
    m = fre.shape[0]
    tc = min(128, min(l for _, l in segs) // S5_Q)
    _, flags = _tile_tables(segs, tc * S5_Q)
    n = m // tc
    fwd = pl.BlockSpec((tc, S5_NS), lambda i, f: (i, 0))
    bwd = pl.BlockSpec((tc, S5_NS), lambda i, f: (n - 1 - i, 0))
    lam = pl.BlockSpec((2, 1, S5_NS), lambda i, f: (0, 0, 0))
    grid_spec = pltpu.PrefetchScalarGridSpec(
        num_scalar_prefetch=1, grid=(n,), in_specs=[fwd, fwd, bwd, bwd, lam, lam],
        out_specs=[fwd, fwd, bwd, bwd],
        scratch_shapes=[pltpu.VMEM((4, 1, S5_NS), F32)])
    sds = jax.ShapeDtypeStruct((m, S5_NS), F32)
    return pl.pallas_call(
        functools.partial(_s5_rec_kernel, tc=tc), grid_spec=grid_spec, out_shape=[sds] * 4,
        compiler_params=_params(("arbitrary",)),
    )(jnp.asarray(flags), fre, fim, bre, bim, lam_re, lam_im)


def _s5_out_kernel(u_ref, ttc_ref, xre_ref, xim_ref, zre_ref, zim_ref, vcc_ref, dsk_ref, spread_ref, y_ref,
                   tt_ref, vc_ref, *, mt):
    @pl.when(pl.program_id(1) == 0)
    def _():
        _s5_expand(ttc_ref, spread_ref, tt_ref, S5_GROUP, S5_GROUP)
        _s5_expand(vcc_ref, spread_ref, vc_ref, S5_STATE, S5_GROUP)

    u = _s5_blocks(u_ref, mt)
    intra = jnp.dot(u.astype(BF16), tt_ref[...], preferred_element_type=F32)
    st = jnp.concatenate([xre_ref[...], xim_ref[...], zre_ref[...], zim_ref[...]], axis=1).astype(BF16)
    carry = jnp.dot(st, vc_ref[...], preferred_element_type=F32)
    y = intra + carry + dsk_ref[...] * u
    for t in range(S5_Q):
        y_ref[pl.ds(t, mt, stride=S5_Q), :] = y[:, t * LANES:(t + 1) * LANES]


def _s5_out(pm, w, xre, xim, zre, zim):
    n_tok = pm.shape[0]
    tt = min(S5_TT, n_tok)
    mt = tt // S5_Q
    ucb = COL_U_S5 // LANES
    st = pl.BlockSpec((mt, S5_SW), lambda k, i: (i, k))
    return pl.pallas_call(
        functools.partial(_s5_out_kernel, mt=mt),
        out_shape=jax.ShapeDtypeStruct((n_tok, S5_WIDTH), F32), grid=(S5_NSB, n_tok // tt),
        in_specs=[pl.BlockSpec((tt, LANES), lambda k, i: (i, ucb + k)),
                  pl.BlockSpec((None, S5_SBW, S5_CW), lambda k, i: (k, 0, 0)),
                  st, st, st, st,
                  pl.BlockSpec((None, 4 * S5_SW, S5_CW), lambda k, i: (k, 0, 0)),
                  pl.BlockSpec((None, 1, S5_SBW), lambda k, i: (k, 0, 0)),
                  pl.BlockSpec((S5_CW, S5_SBW), lambda k, i: (0, 0))],
        out_specs=pl.BlockSpec((tt, LANES), lambda k, i: (i, k)),
        scratch_shapes=[pltpu.VMEM((S5_SBW, S5_SBW), BF16), pltpu.VMEM((4 * S5_SW, S5_SBW), BF16)],
        compiler_params=_params(("parallel", "arbitrary")),
    )(pm, w['tt'], xre, xim, zre, zim, w['vc'], w['dsk'], _s5_spread_matrix(S5_GROUP))


def _s5_glu_kernel(y_ref, w_ref, b_ref, o_ref):
    g = jax.nn.gelu(y_ref[...])
    t = jnp.dot(g.astype(BF16), w_ref[...], preferred_element_type=F32) + b_ref[...]
    o_ref[...] = (g * jax.nn.sigmoid(t)).astype(o_ref.dtype)


def _s5_glu(y, glu_w, layer, glu_b):
    n_tok = y.shape[0]
    tr = 512
    return pl.pallas_call(
        _s5_glu_kernel, out_shape=jax.ShapeDtypeStruct((n_tok, S5_WIDTH), BF16), grid=(n_tok // tr,),
        in_specs=[pl.BlockSpec((tr, S5_WIDTH), lambda i: (i, 0)),
                  pl.BlockSpec((None, S5_WIDTH, S5_WIDTH), lambda i: (layer, 0, 0)),
                  pl.BlockSpec((1, S5_WIDTH), lambda i: (0, 0))],
        out_specs=pl.BlockSpec((tr, S5_WIDTH), lambda i: (i, 0)),
        compiler_params=_params(("parallel",)),
    )(y, glu_w, glu_b[None, :])


def _s5(pm, segs, lw, glu_w, layer):
    w = _s5_weights(lw)
    fre, fim, bre, bim = _s5_in(pm, w['wa'])
    xre, xim, zre, zim = _s5_rec(fre, fim, bre, bim, w['lam_re'], w['lam_im'], segs)
    y = _s5_out(pm, w, xre, xim, zre, zim)
    return _s5_glu(y, glu_w, layer, lw['s5_glu_b'])


def _matmul_weights(w_in, w_branch_ssd, w_branch_swa, w_branch_s5, w_branch_na, w_out, ffn_w1, ffn_w3, ffn_w2,
                    s5_glu_w):
    o = IN_OFFS
    w_main = jnp.concatenate([w_in[:, :, :o[2]], w_in[:, :, o[3]:]], axis=2).astype(BF16)
    w_dt_raw = w_in[:, :, o[2]:o[3]]
    zpad = jnp.zeros((DEPTH, D_MODEL, LANES - SSD_HEADS), F32)
    w_dt = jnp.concatenate([w_dt_raw[:, :, :SSD_HEADS], zpad, w_dt_raw[:, :, SSD_HEADS:], zpad],
                           axis=2).astype(BF16)
    w_branch = jnp.stack([w_branch_ssd, w_branch_swa, w_branch_s5, w_branch_na], axis=1).astype(BF16)
    return dict(main=w_main, dt=w_dt, branch=w_branch, out=w_out.astype(BF16),
                w1=ffn_w1.astype(BF16), w3=ffn_w3.astype(BF16), w2=ffn_w2.astype(BF16),
                glu=s5_glu_w.astype(BF16))


def _layer(xs, c8, lw, mw, ada_w, ada_b, layer, split_output):
    mod = _ada(c8, ada_w, ada_b, layer)
    mod4 = mod.reshape(8, 6, 1, D_MODEL)

    h = _norm_mod(xs, lw['norm1_g'], mod4, 1, 0)
    pm = _mm(h, mw['main'], layer, N_MIX, 1024, F32)
    dt_raw = _mm(h, mw['dt'], layer, 2 * LANES, 2 * LANES, F32)

    y_ssd = _ssd(pm, dt_raw, SEGS, lw)
    qs, ks, qn, kn, vn = _qk_prep(pm, SEGS, lw)
    y_swa = _swa(qs, ks, pm, lw['swa_sink'], SEGS)
    y_s5 = _s5(pm, SEGS, lw, mw['glu'], layer)
    y_na = _na(qn, kn, vn, lw['na_rpb'], GROUPS)
    merged = _merge(h, mw['main'], jnp.stack([y_ssd, y_swa, y_s5, y_na]), mw['branch'], layer)
    x = _mm_res(merged, mw['out'], layer, xs, mod4, 2, TM, 1024 // len(xs), D_MODEL)

    h2 = _norm_mod((x,), lw['norm2_g'], mod4, 4, 3)
    u = _ffn_up(h2, mw['w1'], mw['w3'], layer)
    tm = TM // 2
    down = functools.partial(_mm_res, u, mw['w2'], layer, (x,), mod4, 5, tm, 512, D_FF)
    if not split_output:
        return down()
    n_first = GROUPS[1][0] // tm
    return down(rows=(0, n_first)), down(rows=(n_first, N_TOK // tm - n_first))


_LAYER_KEYS = ('ada_w', 'ada_b', 'norm1_g', 'norm2_g', 'w_in',
               'ssd_conv_w', 'ssd_conv_b', 'ssd_dt_bias', 'ssd_a_log', 'ssd_d', 'ssd_norm_g',
               'swa_q_norm_g', 'swa_k_norm_g', 'swa_sink',
               's5_a_re', 's5_a_im', 's5_log_step', 's5_b_re', 's5_b_im', 's5_c_re', 's5_c_im',
               's5_d', 's5_glu_w', 's5_glu_b',
               'na_q_norm_g', 'na_k_norm_g', 'na_rpb',
               'w_branch_ssd', 'w_branch_swa', 'w_branch_s5', 'w_branch_na', 'w_out',
               'ffn_w1', 'ffn_w3', 'ffn_w2')


def kernel(x_prompt, x_sample, c_prompt, c_sample, ada_w, ada_b, norm1_g, norm2_g, w_in, ssd_conv_w, ssd_conv_b, ssd_dt_bias, ssd_a_log, ssd_d, ssd_norm_g, swa_q_norm_g, swa_k_norm_g, swa_sink, s5_a_re, s5_a_im, s5_log_step, s5_b_re, s5_b_im, s5_c_re, s5_c_im, s5_d, s5_glu_w, s5_glu_b, na_q_norm_g, na_k_norm_g, na_rpb, w_branch_ssd, w_branch_swa, w_branch_s5, w_branch_na, w_out, ffn_w1, ffn_w3, ffn_w2):
    stacked = dict(zip(_LAYER_KEYS, (ada_w, ada_b, norm1_g, norm2_g, w_in,
                                     ssd_conv_w, ssd_conv_b, ssd_dt_bias, ssd_a_log, ssd_d, ssd_norm_g,
                                     swa_q_norm_g, swa_k_norm_g, swa_sink,
                                     s5_a_re, s5_a_im, s5_log_step, s5_b_re, s5_b_im, s5_c_re, s5_c_im,
                                     s5_d, s5_glu_w, s5_glu_b,
                                     na_q_norm_g, na_k_norm_g, na_rpb,
                                     w_branch_ssd, w_branch_swa, w_branch_s5, w_branch_na, w_out,
                                     ffn_w1, ffn_w3, ffn_w2)))
    xs = (x_prompt.reshape(BATCH * SEQ, D_MODEL), x_sample.reshape(DEC_BATCH * DEC_SEQ, D_MODEL))
    c8 = jnp.concatenate([c_prompt, c_sample, jnp.zeros((8 - N_SEQS, D_MODEL), F32)], axis=0)
    mw = _matmul_weights(w_in, w_branch_ssd, w_branch_swa, w_branch_s5, w_branch_na, w_out, ffn_w1, ffn_w3, ffn_w2,
                         s5_glu_w)
    for i in range(DEPTH):
        lw = {k: v[i] for k, v in stacked.items() if v.size < D_MODEL * D_MODEL}
        last = i == DEPTH - 1
        out = _layer(xs, c8, lw, mw, ada_w, ada_b, i, split_output=last)
        xs = out if last else (out,)
    return (xs[0].reshape(BATCH, SEQ, D_MODEL), xs[1].reshape(DEC_BATCH, DEC_SEQ, D_MODEL))
```

```python
import functools

import jax
import jax.numpy as jnp
import numpy as np
from jax import lax
from jax.experimental import pallas as pl
from jax.experimental.pallas import tpu as pltpu

D_MODEL = 4096
BATCH = 2
SEQ = 4096
DEPTH = 2
DEC_BATCH = 4
DEC_SEQ = 2048
N_TOK = BATCH * SEQ + DEC_BATCH * DEC_SEQ
N_SEQS = BATCH + DEC_BATCH
GROUPS = ((0, BATCH, SEQ), (BATCH * SEQ, DEC_BATCH, DEC_SEQ))
SEGS = tuple((row0 + i * l, l) for row0, b, l in GROUPS for i in range(b))

EPS = 1e-6
HEAD_DIM = 128
N_BRANCH = 4
SSD_HEADS = 16
SSD_HEAD_DIM = 64
SSD_INNER = SSD_HEADS * SSD_HEAD_DIM
SSD_GROUPS = 2
SSD_STATE = 128
SSD_CONV = 5
SSD_CHUNK = 128
SSD_CONV_DIM = SSD_INNER + 2 * SSD_GROUPS * SSD_STATE
SWA_HEADS = 8
SWA_KV_HEADS = 2
SWA_WIDTH = SWA_HEADS * HEAD_DIM
SWA_KV_WIDTH = SWA_KV_HEADS * HEAD_DIM
SWA_WINDOW = 128
SWA_BLOCK = 128
ROPE_THETA = 10000.0
S5_WIDTH = 1024
S5_GROUP = 16
S5_GROUPS = S5_WIDTH // S5_GROUP
S5_STATE = 64
NA_HEADS = 8
NA_WIDTH = NA_HEADS * HEAD_DIM
GRID_W = 64
NA_KR = 8
NA_KW = 16
D_FF = ((8 * D_MODEL + 3 * 256 - 1) // (3 * 256)) * 256
IN_SIZES = (SSD_INNER, SSD_CONV_DIM, 2 * SSD_HEADS,
            SWA_WIDTH, SWA_KV_WIDTH, SWA_KV_WIDTH,
            S5_WIDTH,
            NA_WIDTH, NA_WIDTH, NA_WIDTH,
            N_BRANCH * D_MODEL)
IN_OFFS = tuple(int(v) for v in np.cumsum((0,) + IN_SIZES))

LANES = 128
TM = 1024
N_MIX = IN_OFFS[10] - IN_SIZES[2]
VMEM_LIMIT = 56 * 1024 * 1024

COL_Z = 0
COL_XBC = COL_Z + SSD_INNER
COL_Q_SWA = COL_XBC + SSD_CONV_DIM
COL_K_SWA = COL_Q_SWA + SWA_WIDTH
COL_V_SWA = COL_K_SWA + SWA_KV_WIDTH
COL_U_S5 = COL_V_SWA + SWA_KV_WIDTH
COL_Q_NA = COL_U_S5 + S5_WIDTH
COL_K_NA = COL_Q_NA + NA_WIDTH
COL_V_NA = COL_K_NA + NA_WIDTH
COL_SSD_END = COL_XBC + SSD_CONV_DIM

F32 = jnp.float32
BF16 = jnp.bfloat16


def _batch_of_tile(i, tile):
    n_p = (BATCH * SEQ) // tile
    return jnp.where(i < n_p, i // (SEQ // tile), BATCH + (i - n_p) // (DEC_SEQ // tile))


def _params(sem):
    return pltpu.CompilerParams(dimension_semantics=sem, vmem_limit_bytes=VMEM_LIMIT)


def _ada_kernel(c_ref, w_ref, b_ref, o_ref):
    c = c_ref[...]
    a = (c * jax.nn.sigmoid(c)).astype(BF16)
    o_ref[...] = jnp.dot(a, w_ref[...].astype(BF16), preferred_element_type=F32) + b_ref[...]


def _ada(c8, ada_w, ada_b, layer):
    n = ada_w.shape[2]
    tn = 512
    return pl.pallas_call(
        _ada_kernel,
        out_shape=jax.ShapeDtypeStruct((8, n), F32),
        grid=(n // tn,),
        in_specs=[pl.BlockSpec((8, D_MODEL), lambda j: (0, 0)),
                  pl.BlockSpec((None, D_MODEL, tn), lambda j: (layer, 0, j)),
                  pl.BlockSpec((1, tn), lambda j: (0, j))],
        out_specs=pl.BlockSpec((8, tn), lambda j: (0, j)),
        compiler_params=_params(("parallel",)),
    )(c8, ada_w, ada_b[layer].reshape(1, n))


def _norm_mod_kernel(*refs, n_first):
    *x_refs, g_ref, scale_ref, shift_ref, o_ref = refs

    def body(x_ref):
        x = x_ref[...]
        y = x * lax.rsqrt(jnp.mean(x * x, axis=-1, keepdims=True) + EPS)
        y = y * g_ref[...]
        o_ref[...] = (y * (1.0 + scale_ref[...]) + shift_ref[...]).astype(o_ref.dtype)

    if len(x_refs) == 1:
        body(x_refs[0])
        return
    i = pl.program_id(0)
    pl.when(i < n_first)(lambda: body(x_refs[0]))
    pl.when(i >= n_first)(lambda: body(x_refs[1]))


def _group_specs(block, n_first):
    def first(i, j=0, *_):
        return (jnp.minimum(i, n_first - 1), jnp.where(i < n_first, j, 0))

    def second(i, j=0, *_):
        return (jnp.maximum(i - n_first, 0), jnp.where(i >= n_first, j, 0))

    return [pl.BlockSpec(block, first), pl.BlockSpec(block, second)]


def _norm_mod(xs, g, mod4, scale_idx, shift_idx):
    tr = 512
    n_first = xs[0].shape[0] // tr
    x_specs = ([pl.BlockSpec((tr, D_MODEL), lambda i: (i, 0))] if len(xs) == 1
               else _group_specs((tr, D_MODEL), n_first))
    return pl.pallas_call(
        functools.partial(_norm_mod_kernel, n_first=n_first),
        out_shape=jax.ShapeDtypeStruct((N_TOK, D_MODEL), BF16),
        grid=(N_TOK // tr,),
        in_specs=x_specs + [
            pl.BlockSpec((1, D_MODEL), lambda i: (0, 0)),
            pl.BlockSpec((None, None, 1, D_MODEL), lambda i: (_batch_of_tile(i, tr), scale_idx, 0, 0)),
            pl.BlockSpec((None, None, 1, D_MODEL), lambda i: (_batch_of_tile(i, tr), shift_idx, 0, 0))],
        out_specs=pl.BlockSpec((tr, D_MODEL), lambda i: (i, 0)),
        compiler_params=_params(("parallel",)),
    )(*xs, g.reshape(1, D_MODEL), mod4, mod4)


def _mm_kernel(a_ref, b_ref, o_ref):
    o_ref[...] = jnp.dot(a_ref[...], b_ref[...], preferred_element_type=F32).astype(o_ref.dtype)


def _mm(a, b, layer, n, tn, out_dtype):
    m, k = a.shape
    return pl.pallas_call(
        _mm_kernel,
        out_shape=jax.ShapeDtypeStruct((m, n), out_dtype),
        grid=(m // TM, n // tn),
        in_specs=[pl.BlockSpec((TM, k), lambda i, j: (i, 0)),
                  pl.BlockSpec((None, k, tn), lambda i, j: (layer, 0, j))],
        out_specs=pl.BlockSpec((TM, tn), lambda i, j: (i, j)),
        compiler_params=_params(("parallel", "parallel")),
    )(a, b)


def _merge_kernel(h_ref, wg_ref, m_ref, wb_ref, o_ref, acc_ref):
    b = pl.program_id(2)
    logits = jnp.dot(h_ref[...], wg_ref[...], preferred_element_type=F32)
    y = jnp.dot(m_ref[...], wb_ref[...], preferred_element_type=F32)
    contrib = jax.nn.sigmoid(logits) * y

    @pl.when(b == 0)
    def _():
        acc_ref[...] = contrib

    @pl.when(b > 0)
    def _():
        acc_ref[...] += contrib

    @pl.when(b == N_BRANCH - 1)
    def _():
        o_ref[...] = acc_ref[...].astype(o_ref.dtype)


def _merge(h, w_main, mix, wb, layer):
    tn = 512
    w = mix.shape[2]
    gate0 = N_MIX // tn
    per_branch = D_MODEL // tn
    return pl.pallas_call(
        _merge_kernel,
        out_shape=jax.ShapeDtypeStruct((N_TOK, D_MODEL), BF16),
        grid=(N_TOK // TM, D_MODEL // tn, N_BRANCH),
        in_specs=[pl.BlockSpec((TM, D_MODEL), lambda i, j, b: (i, 0)),
                  pl.BlockSpec((None, D_MODEL, tn), lambda i, j, b: (layer, 0, gate0 + b * per_branch + j)),
                  pl.BlockSpec((None, TM, w), lambda i, j, b: (b, i, 0)),
                  pl.BlockSpec((None, None, w, tn), lambda i, j, b: (layer, b, 0, j))],
        out_specs=pl.BlockSpec((TM, tn), lambda i, j, b: (i, j)),
        scratch_shapes=[pltpu.VMEM((TM, tn), F32)],
        compiler_params=_params(("parallel", "parallel", "arbitrary")),
    )(h, w_main, mix, wb)


def _mm_res_kernel(a_ref, b_ref, *rest, nk, n_first):
    *x_refs, gate_ref, o_ref, acc_ref = rest
    d = jnp.dot(a_ref[...], b_ref[...], preferred_element_type=F32)

    def finish(total):
        def store(x_ref):
            o_ref[...] = x_ref[...] + gate_ref[...] * total

        if len(x_refs) == 1:
            store(x_refs[0])
            return
        i = pl.program_id(0)
        pl.when(i < n_first)(functools.partial(store, x_refs[0]))
        pl.when(i >= n_first)(functools.partial(store, x_refs[1]))

    if nk == 1:
        finish(d)
        return
    k = pl.program_id(2)

    @pl.when(k == 0)
    def _():
        acc_ref[...] = d

    @pl.when(k > 0)
    def _():
        acc_ref[...] += d

    @pl.when(k == nk - 1)
    def _():
        finish(acc_ref[...])


def _mm_res(a, b, layer, xs, mod4, gate_idx, tm, tn, tk, rows=None):
    kdim = a.shape[1]
    n = b.shape[2]
    nk = kdim // tk
    off, m_tiles = rows if rows is not None else (0, a.shape[0] // tm)
    n_first = xs[0].shape[0] // tm
    x_specs = ([pl.BlockSpec((tm, tn), lambda i, j, k: (i + off, j))] if len(xs) == 1
               else _group_specs((tm, tn), n_first))
    return pl.pallas_call(
        functools.partial(_mm_res_kernel, nk=nk, n_first=n_first),
        out_shape=jax.ShapeDtypeStruct((m_tiles * tm, n), F32),
        grid=(m_tiles, n // tn, nk),
        in_specs=[pl.BlockSpec((tm, tk), lambda i, j, k: (i + off, k)),
                  pl.BlockSpec((None, tk, tn), lambda i, j, k: (layer, k, j))] + x_specs + [
                  pl.BlockSpec((None, None, 1, tn),
                               lambda i, j, k: (_batch_of_tile(i + off, tm), gate_idx, 0, j))],
        out_specs=pl.BlockSpec((tm, tn), lambda i, j, k: (i, j)),
        scratch_shapes=[pltpu.VMEM((tm, tn), F32)],
        compiler_params=_params(("parallel", "parallel", "arbitrary")),
    )(a, b, *xs, mod4)


def _ffn_up_kernel(h_ref, w1_ref, w3_ref, o_ref):
    h = h_ref[...]
    a = jnp.dot(h, w1_ref[...], preferred_element_type=F32)
    b = jnp.dot(h, w3_ref[...], preferred_element_type=F32)
    o_ref[...] = (a * jax.nn.sigmoid(a) * b).astype(o_ref.dtype)


def _ffn_up(h, w1, w3, layer):
    tn = 512
    n = w1.shape[2]
    return pl.pallas_call(
        _ffn_up_kernel,
        out_shape=jax.ShapeDtypeStruct((N_TOK, n), BF16),
        grid=(N_TOK // TM, pl.cdiv(n, tn)),
        in_specs=[pl.BlockSpec((TM, D_MODEL), lambda i, j: (i, 0)),
                  pl.BlockSpec((None, D_MODEL, tn), lambda i, j: (layer, 0, j)),
                  pl.BlockSpec((None, D_MODEL, tn), lambda i, j: (layer, 0, j))],
        out_specs=pl.BlockSpec((TM, tn), lambda i, j: (i, j)),
        compiler_params=_params(("parallel", "parallel")),
    )(h, w1, w3)


def _tile_tables(segs, tile):
    pos, flags = [], []
    for _, length in segs:
        n = length // tile
        for j in range(n):
            pos.append(j)
            flags.append((1 if j == 0 else 0) | (2 if j == n - 1 else 0))
    return np.asarray(pos, np.int32), np.asarray(flags, np.int32)


def _head_slice(h):
    return slice(h * HEAD_DIM, (h + 1) * HEAD_DIM)


def _qk_prep_kernel(pos_ref, qs_lo_ref, qs_hi_ref, ks_ref, qn_ref, kn_ref, vn_ref, cos_ref, sin_ref, g_ref,
                    oqs_ref, oks_ref, oqn_ref, okn_ref, ovn_ref):
    del pos_ref
    half_heads = SWA_HEADS // 2
    cos = cos_ref[...]
    sin = sin_ref[...]

    def hnorm(x, g):
        return x * lax.rsqrt(jnp.mean(x * x, axis=-1, keepdims=True) + EPS) * g

    def rope(x):
        return x * cos + pltpu.roll(x, HEAD_DIM // 2, 1) * sin

    for h in range(SWA_HEADS):
        src = qs_lo_ref if h < half_heads else qs_hi_ref
        oqs_ref[:, _head_slice(h)] = rope(hnorm(src[:, _head_slice(h % half_heads)], g_ref[0:1, :])).astype(BF16)
    for h in range(SWA_KV_HEADS):
        oks_ref[:, _head_slice(h)] = rope(hnorm(ks_ref[:, _head_slice(h)], g_ref[1:2, :])).astype(BF16)
    for h in range(NA_HEADS):
        oqn_ref[:, _head_slice(h)] = hnorm(qn_ref[:, _head_slice(h)], g_ref[2:3, :]).astype(BF16)
        okn_ref[:, _head_slice(h)] = hnorm(kn_ref[:, _head_slice(h)], g_ref[3:4, :]).astype(BF16)
    ovn_ref[...] = vn_ref[...].astype(BF16)


def _rope_tables(max_len):
    half = HEAD_DIM // 2
    inv_freq = ROPE_THETA ** (-jnp.arange(half, dtype=F32) / half)
    ang = jnp.arange(max_len, dtype=F32)[:, None] * inv_freq[None, :]
    cos, sin = jnp.cos(ang), jnp.sin(ang)
    return jnp.concatenate([cos, cos], axis=1), jnp.concatenate([-sin, sin], axis=1)


def _qk_prep(pm, segs, lw):
    n_tok = pm.shape[0]
    tr = 512
    pos, _ = _tile_tables(segs, tr)
    cos, sin = _rope_tables(max(l for _, l in segs))
    gains = jnp.concatenate([lw['swa_q_norm_g'][None], lw['swa_k_norm_g'][None],
                             lw['na_q_norm_g'][None], lw['na_k_norm_g'][None],
                             jnp.zeros((4, HEAD_DIM), F32)], axis=0)
    wide = lambda cb: pl.BlockSpec((tr, 1024), lambda i, p: (i, cb))
    narrow = lambda cb: pl.BlockSpec((tr, 256), lambda i, p: (i, cb))
    grid_spec = pltpu.PrefetchScalarGridSpec(
        num_scalar_prefetch=1, grid=(n_tok // tr,),
        in_specs=[pl.BlockSpec((tr, 512), lambda i, p: (i, COL_Q_SWA // 512)),
                  pl.BlockSpec((tr, 512), lambda i, p: (i, COL_Q_SWA // 512 + 1)),
                  narrow(COL_K_SWA // 256), wide(COL_Q_NA // 1024),
                  wide(COL_K_NA // 1024), wide(COL_V_NA // 1024),
                  pl.BlockSpec((tr, HEAD_DIM), lambda i, p: (p[i], 0)),
                  pl.BlockSpec((tr, HEAD_DIM), lambda i, p: (p[i], 0)),
                  pl.BlockSpec((8, HEAD_DIM), lambda i, p: (0, 0))],
        out_specs=[pl.BlockSpec((tr, 1024), lambda i, p: (i, 0)),
                   pl.BlockSpec((tr, 256), lambda i, p: (i, 0)),
                   pl.BlockSpec((tr, 1024), lambda i, p: (i, 0)),
                   pl.BlockSpec((tr, 1024), lambda i, p: (i, 0)),
                   pl.BlockSpec((tr, 1024), lambda i, p: (i, 0))])
    sds = lambda w: jax.ShapeDtypeStruct((n_tok, w), BF16)
    return pl.pallas_call(
        _qk_prep_kernel, grid_spec=grid_spec,
        out_shape=[sds(1024), sds(256), sds(1024), sds(1024), sds(1024)],
        compiler_params=_params(("parallel",)),
    )(jnp.asarray(pos), pm, pm, pm, pm, pm, pm, cos, sin, gains)


SWA_TQ = 512
SWA_GRP = SWA_HEADS // SWA_KV_HEADS


def _swa_kernel(flags_ref, sink_ref, q_ref, kc_ref, kp_ref, kn_ref, vc_ref, vp_ref, vn_ref, o_ref, *, tq):
    fl = flags_ref[pl.program_id(0)]
    lo = jnp.where((fl & 1) > 0, SWA_BLOCK, 0)
    hi = jnp.where((fl & 2) > 0, 2 * SWA_BLOCK, 3 * SWA_BLOCK)
    nqb = tq // SWA_BLOCK
    m_rows = SWA_GRP * SWA_BLOCK
    row = lax.broadcasted_iota(jnp.int32, (m_rows, 3 * SWA_BLOCK), 0) & (SWA_BLOCK - 1)
    col = lax.broadcasted_iota(jnp.int32, (m_rows, 3 * SWA_BLOCK), 1)
    band = (col >= row) & (col <= row + 2 * SWA_WINDOW)
    scale = HEAD_DIM ** -0.5
    for g in range(SWA_KV_HEADS):
        ks = _head_slice(g)
        k_ext = jnp.concatenate([kp_ref[:, ks], kc_ref[:, ks], kn_ref[:, ks]], axis=0)
        v_ext = jnp.concatenate([vp_ref[:, ks], vc_ref[:, ks], vn_ref[:, ks]], axis=0).astype(BF16)
        sk = jnp.concatenate([jnp.full((SWA_BLOCK, 1), sink_ref[g * SWA_GRP + h], F32)
                              for h in range(SWA_GRP)], axis=0)
        for qb in range(nqb):
            rows = slice(qb * SWA_BLOCK, (qb + 1) * SWA_BLOCK)
            q = jnp.concatenate([q_ref[rows, _head_slice(g * SWA_GRP + h)] for h in range(SWA_GRP)], axis=0)
            keys = k_ext[qb * SWA_BLOCK:(qb + 3) * SWA_BLOCK]
            vals = v_ext[qb * SWA_BLOCK:(qb + 3) * SWA_BLOCK]
            s = lax.dot_general(q, keys, (((1,), (1,)), ((), ())), preferred_element_type=F32) * scale
            mask = band
            if qb == 0:
                mask = mask & (col >= lo)
            if qb == nqb - 1:
                mask = mask & (col < hi)
            s = jnp.where(mask, s, -jnp.inf)
            m = jnp.maximum(jnp.max(s, axis=-1, keepdims=True), sk)
            p = jnp.exp(s - m)
            denom = jnp.sum(p, axis=-1, keepdims=True) + jnp.exp(sk - m)
            o = jnp.dot((p / denom).astype(BF16), vals, preferred_element_type=F32)
            for h in range(SWA_GRP):
                o_ref[rows, _head_slice(g * SWA_GRP + h)] = o[h * SWA_BLOCK:(h + 1) * SWA_BLOCK].astype(o_ref.dtype)


def _swa(qs, ks, pm, sink, segs):
    n_tok = qs.shape[0]
    tq = min(SWA_TQ, min(l for _, l in segs))
    _, flags = _tile_tables(segs, tq)
    nb = tq // SWA_BLOCK
    last_blk = n_tok // SWA_BLOCK - 1
    prev_map = lambda cb: (lambda i, f: (jnp.maximum(i * nb - 1, 0), cb))
    next_map = lambda cb: (lambda i, f: (jnp.minimum((i + 1) * nb, last_blk), cb))
    vcb = COL_V_SWA // SWA_KV_WIDTH
    grid_spec = pltpu.PrefetchScalarGridSpec(
        num_scalar_prefetch=1, grid=(n_tok // tq,),
        in_specs=[pl.BlockSpec(memory_space=pltpu.SMEM),
                  pl.BlockSpec((tq, SWA_WIDTH), lambda i, f: (i, 0)),
                  pl.BlockSpec((tq, SWA_KV_WIDTH), lambda i, f: (i, 0)),
                  pl.BlockSpec((SWA_BLOCK, SWA_KV_WIDTH), prev_map(0)),
                  pl.BlockSpec((SWA_BLOCK, SWA_KV_WIDTH), next_map(0)),
                  pl.BlockSpec((tq, SWA_KV_WIDTH), lambda i, f: (i, vcb)),
                  pl.BlockSpec((SWA_BLOCK, SWA_KV_WIDTH), prev_map(vcb)),
                  pl.BlockSpec((SWA_BLOCK, SWA_KV_WIDTH), next_map(vcb))],
        out_specs=pl.BlockSpec((tq, SWA_WIDTH), lambda i, f: (i, 0)))
    return pl.pallas_call(
        functools.partial(_swa_kernel, tq=tq), grid_spec=grid_spec,
        out_shape=jax.ShapeDtypeStruct((n_tok, SWA_WIDTH), BF16),
        compiler_params=_params(("parallel",)),
    )(jnp.asarray(flags), sink, qs, ks, ks, ks, pm, pm, pm)


NA_R = NA_KR // 2
NA_WIN = NA_R + NA_KR


def _na_bias_table(rpb):
    off = np.array([np.zeros(NA_R, int), np.arange(NA_R), np.full(NA_R, NA_R)])
    p_of = np.array([np.arange(NA_R), np.full(NA_R, NA_KR // 2), NA_KR // 2 + np.arange(NA_R)])
    j = np.arange(NA_WIN)[None, None, :] - off[:, :, None]
    row_ok = (j >= 0) & (j < NA_KR)
    dr = np.clip(j - p_of[:, :, None] + (NA_KR - 1), 0, 2 * NA_KR - 2)
    qc = np.arange(GRID_W)
    kc = np.arange(GRID_W)
    dc = np.clip(kc[None, :] - qc[:, None], -(NA_KW - 1), NA_KW - 1) + (NA_KW - 1)
    col_start = np.clip(qc - NA_KW // 2, 0, GRID_W - NA_KW)
    valid = (kc[None, :] >= col_start[:, None]) & (kc[None, :] < col_start[:, None] + NA_KW)
    hp = lax.Precision.HIGHEST
    sel_dc = jnp.asarray(np.eye(2 * NA_KW - 1, dtype=np.float32)[dc.reshape(-1)])
    sel_dr = jnp.asarray(np.eye(2 * NA_KR - 1, dtype=np.float32)[dr.reshape(-1)])
    cols = jnp.einsum('hab,xb->hax', rpb, sel_dc, precision=hp)
    bias = jnp.einsum('ya,hax->yhx', sel_dr, cols, precision=hp)
    bias = bias.reshape(3, NA_R, NA_WIN, NA_HEADS, GRID_W, GRID_W)
    ok = row_ok[:, :, :, None, None, None] & valid[None, None, None, None, :, :]
    bias = jnp.where(ok, bias, -jnp.inf)
    return jnp.transpose(bias, (0, 3, 1, 4, 2, 5)).reshape(3, NA_HEADS, NA_R * GRID_W, NA_WIN * GRID_W)


def _na_kernel(q_ref, k_ref, v_ref, bias_ref, o_ref, *, rows):
    first_row = jnp.clip(pl.program_id(1) * NA_R - NA_KR // 2, 0, rows - NA_WIN)
    keys = pl.ds(pl.multiple_of(first_row * GRID_W, GRID_W), NA_WIN * GRID_W)
    scale = HEAD_DIM ** -0.5
    for h in range(NA_HEADS):
        s = lax.dot_general(q_ref[:, _head_slice(h)], k_ref[keys, _head_slice(h)], (((1,), (1,)), ((), ())),
                            preferred_element_type=F32) * scale
        s = s + bias_ref[h]
        p = jnp.exp(s - jnp.max(s, axis=-1, keepdims=True))
        p = p / jnp.sum(p, axis=-1, keepdims=True)
        o = jnp.dot(p.astype(BF16), v_ref[keys, _head_slice(h)], preferred_element_type=F32)
        o_ref[:, _head_slice(h)] = o.astype(o_ref.dtype)


def _na_group(qn, kn, vn, bias, row0, b, l):
    rows = l // GRID_W
    nsteps = rows // NA_R
    tq = NA_R * GRID_W
    kind = lambda st: jnp.where(st == 0, 0, jnp.where(st == nsteps - 1, 2, 1))
    seq_spec = pl.BlockSpec((l, NA_WIDTH), lambda bi, st: (row0 // l + bi, 0))
    return pl.pallas_call(
        functools.partial(_na_kernel, rows=rows),
        out_shape=jax.ShapeDtypeStruct((b * l, NA_WIDTH), BF16),
        grid=(b, nsteps),
        in_specs=[pl.BlockSpec((tq, NA_WIDTH), lambda bi, st: (row0 // tq + bi * nsteps + st, 0)),
                  seq_spec, seq_spec,
                  pl.BlockSpec((None, NA_HEADS, tq, NA_WIN * GRID_W), lambda bi, st: (kind(st), 0, 0, 0))],
        out_specs=pl.BlockSpec((tq, NA_WIDTH), lambda bi, st: (bi * nsteps + st, 0)),
        compiler_params=_params(("parallel", "arbitrary")),
    )(qn, kn, vn, bias)


def _na(qn, kn, vn, rpb, groups):
    bias = _na_bias_table(rpb)
    return jnp.concatenate([_na_group(qn, kn, vn, bias, row0, b, l) for row0, b, l in groups], axis=0)


def _ssd_prep_kernel(flags_ref, xc_ref, xp_ref, xn_ref, dt_ref, w_ref, b_ref, dtb_ref,
                     xs_ref, bc_ref, dtf_ref, *, tr):
    fl = flags_ref[pl.program_id(0)]
    halo = 8
    xbc = slice(COL_XBC, COL_SSD_END)
    xp = jnp.where((fl & 1) > 0, 0.0, xp_ref[:, xbc])
    xn = jnp.where((fl & 2) > 0, 0.0, xn_ref[:, xbc])
    ext = jnp.concatenate([xp, xc_ref[:, xbc], xn], axis=0)
    n = tr + 2 * halo
    acc = None
    for k in range(SSD_CONV):
        sh = (SSD_CONV // 2 - k) % n
        xk = ext if sh == 0 else pltpu.roll(ext, sh, 0)
        term = xk[halo:halo + tr] * w_ref[k:k + 1, :]
        acc = term if acc is None else acc + term
    acc = acc + b_ref[...]
    y = acc * jax.nn.sigmoid(acc)
    xs_ref[...] = y[:, :SSD_INNER]
    bc_ref[...] = y[:, SSD_INNER:]
    t = dt_ref[...] + dtb_ref[...]
    dtf_ref[...] = jnp.maximum(t, 0.0) + jnp.log1p(jnp.exp(-jnp.abs(t)))


def _dt_pad(v):
    z = jnp.zeros((LANES - SSD_HEADS,), v.dtype)
    return jnp.concatenate([v[0], z, v[1], z])


def _ssd_prep(pm, dt_raw, segs, lw):
    n_tok = pm.shape[0]
    tr = 512
    _, flags = _tile_tables(segs, tr)
    nb8 = tr // 8
    last8 = n_tok // 8 - 1
    w8 = jnp.concatenate([lw['ssd_conv_w'], jnp.zeros((8 - SSD_CONV, SSD_CONV_DIM), F32)], axis=0)
    dtb = _dt_pad(lw['ssd_dt_bias'])[None, :]
    grid_spec = pltpu.PrefetchScalarGridSpec(
        num_scalar_prefetch=1, grid=(n_tok // tr,),
        in_specs=[pl.BlockSpec((tr, COL_SSD_END), lambda i, f: (i, 0)),
                  pl.BlockSpec((8, COL_SSD_END), lambda i, f: (jnp.maximum(i * nb8 - 1, 0), 0)),
                  pl.BlockSpec((8, COL_SSD_END), lambda i, f: (jnp.minimum((i + 1) * nb8, last8), 0)),
                  pl.BlockSpec((tr, 2 * LANES), lambda i, f: (i, 0)),
                  pl.BlockSpec((8, SSD_CONV_DIM), lambda i, f: (0, 0)),
                  pl.BlockSpec((1, SSD_CONV_DIM), lambda i, f: (0, 0)),
                  pl.BlockSpec((1, 2 * LANES), lambda i, f: (0, 0))],
        out_specs=[pl.BlockSpec((tr, SSD_INNER), lambda i, f: (i, 0)),
                   pl.BlockSpec((tr, 2 * SSD_GROUPS * SSD_STATE), lambda i, f: (i, 0)),
                   pl.BlockSpec((tr, 2 * LANES), lambda i, f: (i, 0))])
    return pl.pallas_call(
        functools.partial(_ssd_prep_kernel, tr=tr), grid_spec=grid_spec,
        out_shape=[jax.ShapeDtypeStruct((n_tok, SSD_INNER), F32),
                   jax.ShapeDtypeStruct((n_tok, 2 * SSD_GROUPS * SSD_STATE), F32),
                   jax.ShapeDtypeStruct((n_tok, 2 * LANES), F32)],
        compiler_params=_params(("parallel",)),
    )(jnp.asarray(flags), pm, pm, pm, dt_raw, w8, lw['ssd_conv_b'][None, :], dtb)


def _split3(x):
    hi = x.astype(BF16)
    r = x - hi.astype(F32)
    mid = r.astype(BF16)
    lo = (r - mid.astype(F32)).astype(BF16)
    return hi, mid, lo


def _dot_sel_l(sel, x):
    hi, mid, lo = _split3(x)
    d = lambda t: jnp.dot(sel, t, preferred_element_type=F32)
    return d(lo) + d(mid) + d(hi)


def _dot_sel_r(x, sel):
    hi, mid, lo = _split3(x)
    d = lambda t: jnp.dot(t, sel, preferred_element_type=F32)
    return d(lo) + d(mid) + d(hi)


def _ssd_scan_kernel(order_ref, reset_ref, xs_ref, bc_ref, dt_ref, alog_ref, e_ref, *rest, bwd):
    if bwd:
        yf_ref, z_ref, dsk_ref, g_ref, o_ref, state_ref = rest
    else:
        o_ref, state_ref = rest
    del order_ref
    q = SSD_CHUNK
    n_pairs = SSD_HEADS // 2
    gs = SSD_GROUPS * SSD_STATE

    @pl.when(reset_ref[pl.program_id(0)] > 0)
    def _():
        state_ref[...] = jnp.zeros_like(state_ref)

    dt = dt_ref[...]
    da = dt * (-jnp.exp(alog_ref[...]))
    r_i = lax.broadcasted_iota(jnp.int32, (q, q), 0)
    c_i = lax.broadcasted_iota(jnp.int32, (q, q), 1)
    incl = (r_i <= c_i) if bwd else (r_i >= c_i)
    tri = jnp.where(incl, 1.0, 0.0).astype(BF16)
    cs = _dot_sel_l(tri, da)
    cs_t = cs.T
    e = e_ref[...]
    dt_x = _dot_sel_r(dt, e)
    cs_x = _dot_sel_r(cs, e)
    total = cs_x[0:1, :] if bwd else cs_x[q - 1:q, :]
    xs = xs_ref[...]
    xdt = xs * dt_x
    w_state = (xdt * jnp.exp(total - cs_x)).astype(BF16)
    xdt_b = xdt.astype(BF16)
    ecs_x = jnp.exp(cs_x)
    etot = jnp.exp(total)
    bc = bc_ref[...]
    b_t = bc[:, :gs].T
    lane = lax.broadcasted_iota(jnp.int32, (q, LANES), 1)
    ys = []
    for g in range(SSD_GROUPS):
        b_g = bc[:, g * SSD_STATE:(g + 1) * SSD_STATE].astype(BF16)
        c_g = bc[:, gs + g * SSD_STATE:gs + (g + 1) * SSD_STATE].astype(BF16)
        bt_g = b_t[g * SSD_STATE:(g + 1) * SSD_STATE, :].astype(BF16)
        cb = lax.dot_general(c_g, b_g, (((1,), (1,)), ((), ())), preferred_element_type=F32)
        for j in range(n_pairs // SSD_GROUPS):
            pair = g * (n_pairs // SSD_GROUPS) + j
            lanes = slice(pair * LANES, (pair + 1) * LANES)
            halves = []
            for hh in range(2):
                h = 2 * pair + hh
                diff = jnp.broadcast_to(cs[:, h:h + 1], (q, q)) - jnp.broadcast_to(cs_t[h:h + 1, :], (q, q))
                decay = jnp.where(incl, jnp.exp(diff), 0.0)
                halves.append(jnp.dot((cb * decay).astype(BF16), xdt_b[:, lanes], preferred_element_type=F32))
            y_diag = jnp.where(lane < SSD_HEAD_DIM, halves[0], halves[1])
            s_prev = state_ref[pair]
            y_off = jnp.dot(c_g, s_prev.astype(BF16), preferred_element_type=F32) * ecs_x[:, lanes]
            contrib = jnp.dot(bt_g, w_state[:, lanes], preferred_element_type=F32)
            state_ref[pair] = s_prev * etot[:, lanes] + contrib
            ys.append(y_diag + y_off)
    y = jnp.concatenate(ys, axis=1)
    if not bwd:
        o_ref[...] = y
        return
    y = yf_ref[...] + y + dsk_ref[...] * xs
    z = z_ref[...]
    y = y * (z * jax.nn.sigmoid(z))
    y = y * lax.rsqrt(jnp.mean(y * y, axis=-1, keepdims=True) + EPS)
    o_ref[...] = (y * g_ref[...]).astype(o_ref.dtype)


def _ssd_scan(xs, bc, dtf, alog, e_mat, segs, bwd, extra=()):
    n_tok = xs.shape[0]
    q = SSD_CHUNK
    _, flags = _tile_tables(segs, q)
    n = n_tok // q
    order = np.arange(n, dtype=np.int32)[::-1].copy() if bwd else np.arange(n, dtype=np.int32)
    reset = ((flags[order] & (2 if bwd else 1)) > 0).astype(np.int32)
    d = 1 if bwd else 0
    row = lambda w, cb=0: pl.BlockSpec((q, w), lambda i, o, r: (o[i], cb))
    const = lambda shape: pl.BlockSpec(shape, lambda i, o, r: (0,) * len(shape))
    in_specs = [row(SSD_INNER), row(2 * SSD_GROUPS * SSD_STATE), row(LANES, d),
                pl.BlockSpec((None, 1, LANES), lambda i, o, r: (d, 0, 0)), const((LANES, SSD_INNER))]
    if bwd:
        in_specs += [row(SSD_INNER), row(SSD_INNER, COL_Z // SSD_INNER), const((1, SSD_INNER)),
                     const((1, SSD_INNER))]
    grid_spec = pltpu.PrefetchScalarGridSpec(
        num_scalar_prefetch=2, grid=(n,), in_specs=in_specs, out_specs=row(SSD_INNER),
        scratch_shapes=[pltpu.VMEM((SSD_HEADS // 2, SSD_STATE, LANES), F32)])
    return pl.pallas_call(
        functools.partial(_ssd_scan_kernel, bwd=bwd), grid_spec=grid_spec,
        out_shape=jax.ShapeDtypeStruct((n_tok, SSD_INNER), BF16 if bwd else F32),
        compiler_params=_params(("arbitrary",)),
    )(jnp.asarray(order), jnp.asarray(reset), xs, bc, dtf, alog, e_mat, *extra)


def _ssd(pm, dt_raw, segs, lw):
    xs, bc, dtf = _ssd_prep(pm, dt_raw, segs, lw)
    alog = _dt_pad(lw['ssd_a_log']).reshape(2, 1, LANES)
    e_np = np.zeros((LANES, SSD_INNER), np.float32)
    for h in range(SSD_HEADS):
        e_np[h, h * SSD_HEAD_DIM:(h + 1) * SSD_HEAD_DIM] = 1.0
    e_mat = jnp.asarray(e_np, BF16)
    y_fwd = _ssd_scan(xs, bc, dtf, alog, e_mat, segs, False)
    dsk = jnp.repeat(lw['ssd_d'], SSD_HEAD_DIM)[None, :]
    return _ssd_scan(xs, bc, dtf, alog, e_mat, segs, True,
                     extra=(y_fwd, pm, dsk, lw['ssd_norm_g'][None, :]))


S5_Q = 16
S5_SB = LANES // S5_GROUP
S5_NSB = S5_GROUPS // S5_SB
S5_SBW = S5_Q * LANES
S5_SW = S5_SB * S5_STATE
S5_NS = S5_GROUPS * S5_STATE
S5_CW = S5_Q * S5_GROUP
assert 4 * S5_STATE == S5_CW


def _s5_weights(lw):
    hp = lax.Precision.HIGHEST
    qn, g, p, c = S5_Q, S5_GROUPS, S5_STATE, S5_GROUP
    nsb, sb = S5_NSB, S5_SB
    cmul = lambda a, b: (a[0] * b[0] - a[1] * b[1], a[0] * b[1] + a[1] * b[0])
    a_re, a_im = lw['s5_a_re'], lw['s5_a_im']
    step = jnp.exp(lw['s5_log_step'])[..., None]
    mag = jnp.exp(a_re * step)
    lam_bar = (mag * jnp.cos(a_im * step), mag * jnp.sin(a_im * step))
    den = a_re * a_re + a_im * a_im
    coef = cmul((lam_bar[0] - 1.0, lam_bar[1]), (a_re / den, -a_im / den))
    b_bar = cmul((coef[0][..., None], coef[1][..., None]),
                 (lw['s5_b_re'][None], lw['s5_b_im'][None]))
    c_c = (lw['s5_c_re'], lw['s5_c_im'])
    pows = [(jnp.ones_like(mag), jnp.zeros_like(mag))]
    for _ in range(qn):
        pows.append(cmul(pows[-1], lam_bar))
    pw = (jnp.stack([t[0] for t in pows], axis=1), jnp.stack([t[1] for t in pows], axis=1))

    t1 = cmul((c_c[0][:, None], c_c[1][:, None]),
              (pw[0][:, :qn, :, None, :], pw[1][:, :qn, :, None, :]))
    lag = jnp.einsum('dmgiq,dgqj->dmgij', jnp.concatenate([t1[0], -t1[1]], axis=-1),
                     jnp.concatenate([b_bar[0], b_bar[1]], axis=2), precision=hp)

    s_idx = np.arange(qn)[:, None]
    t_idx = np.arange(qn)[None, :]
    df = t_idx - s_idx
    kf = lag[0][np.clip(df, 0, qn - 1)] * jnp.asarray(df >= 0, F32)[:, :, None, None, None]
    kb = lag[1][np.clip(-df, 0, qn - 1)] * jnp.asarray(df <= 0, F32)[:, :, None, None, None]
    tt = jnp.transpose(kf + kb, (2, 0, 4, 1, 3))
    by_block_row = lambda a: jnp.transpose(a.reshape(nsb, sb, qn, c, S5_CW), (0, 2, 1, 3, 4)).reshape(
        nsb, S5_SBW, S5_CW)
    tt = by_block_row(tt)

    def in_op(d, powers):
        w = cmul((pw[0][d, powers][:, :, :, None], pw[1][d, powers][:, :, :, None]),
                 (b_bar[0][d][None], b_bar[1][d][None]))
        return tuple(jnp.transpose(t, (1, 0, 3, 2)) for t in w)

    wf = in_op(0, qn - 1 - np.arange(qn))
    wb = in_op(1, np.arange(qn))
    wa = by_block_row(jnp.stack([wf[0], wf[1], wb[0], wb[1]], axis=3))

    def out_op(d, powers):
        return cmul((jnp.transpose(c_c[0][d], (0, 2, 1))[:, :, None, :],
                     jnp.transpose(c_c[1][d], (0, 2, 1))[:, :, None, :]),
                    (jnp.transpose(pw[0][d, powers], (1, 2, 0))[:, :, :, None],
                     jnp.transpose(pw[1][d, powers], (1, 2, 0))[:, :, :, None]))

    vf = out_op(0, 1 + np.arange(qn))
    vb = out_op(1, qn - np.arange(qn))
    v4 = jnp.stack([vf[0], -vf[1], vb[0], -vb[1]], axis=0)
    vc = jnp.transpose(v4.reshape(4, nsb, sb * p, S5_CW), (1, 0, 2, 3)).reshape(nsb, 4 * S5_SW, S5_CW)

    lam_q = (pw[0][:, qn].reshape(2, 1, S5_NS), pw[1][:, qn].reshape(2, 1, S5_NS))
    dsk = jnp.broadcast_to(lw['s5_d'].reshape(nsb, 1, 1, LANES), (nsb, 1, qn, LANES)).reshape(nsb, 1, S5_SBW)
    return dict(tt=tt.astype(BF16), wa=wa.astype(BF16), vc=vc.astype(BF16),
                lam_re=lam_q[0], lam_im=lam_q[1], dsk=dsk)


S5_TT = 4096


def _s5_blocks(u_ref, mt):
    return jnp.concatenate([u_ref[pl.ds(t, mt, stride=S5_Q), :] for t in range(S5_Q)], axis=1)


def _s5_spread_matrix(inner):
    src = np.arange(S5_CW)
    dst = np.arange(S5_SBW)
    same = (src[:, None] // inner == dst[None, :] // (S5_SB * inner)) & (src[:, None] % inner == dst[None, :] % inner)
    return jnp.asarray(same, BF16)


def _s5_expand(c_ref, spread_ref, out_ref, row_inner, col_inner):
    rows = S5_CW
    col_g = (lax.broadcasted_iota(jnp.int32, (rows, S5_SBW), 1) // col_inner) % S5_SB
    row_l = lax.broadcasted_iota(jnp.int32, (rows, S5_SBW), 0)
    for r0 in range(0, S5_SBW, rows):
        row_g = ((row_l + r0) // row_inner) % S5_SB
        full = jnp.dot(c_ref[r0:r0 + rows, :], spread_ref[...], preferred_element_type=F32)
        out_ref[r0:r0 + rows, :] = jnp.where(row_g == col_g, full, 0.0).astype(out_ref.dtype)


def _s5_in_kernel(u_ref, wc_ref, spread_ref, fre_ref, fim_ref, bre_ref, bim_ref, w_ref, *, mt):
    @pl.when(pl.program_id(1) == 0)
    def _():
        _s5_expand(wc_ref, spread_ref, w_ref, S5_GROUP, S5_STATE)

    r = jnp.dot(_s5_blocks(u_ref, mt).astype(BF16), w_ref[...], preferred_element_type=F32)
    fre_ref[...] = r[:, 0 * S5_SW:1 * S5_SW]
    fim_ref[...] = r[:, 1 * S5_SW:2 * S5_SW]
    bre_ref[...] = r[:, 2 * S5_SW:3 * S5_SW]
    bim_ref[...] = r[:, 3 * S5_SW:4 * S5_SW]


def _s5_in(pm, wa):
    n_tok = pm.shape[0]
    tt = min(S5_TT, n_tok)
    mt = tt // S5_Q
    ucb = COL_U_S5 // LANES
    st = pl.BlockSpec((mt, S5_SW), lambda k, i: (i, k))
    sds = jax.ShapeDtypeStruct((n_tok // S5_Q, S5_NS), F32)
    return pl.pallas_call(
        functools.partial(_s5_in_kernel, mt=mt), out_shape=[sds] * 4, grid=(S5_NSB, n_tok // tt),
        in_specs=[pl.BlockSpec((tt, LANES), lambda k, i: (i, ucb + k)),
                  pl.BlockSpec((None, S5_SBW, S5_CW), lambda k, i: (k, 0, 0)),
                  pl.BlockSpec((S5_CW, S5_SBW), lambda k, i: (0, 0))],
        out_specs=[st] * 4,
        scratch_shapes=[pltpu.VMEM((S5_SBW, 4 * S5_SW), BF16)],
        compiler_params=_params(("parallel", "arbitrary")),
    )(pm, wa, _s5_spread_matrix(S5_STATE))


def _s5_rec_kernel(flags_ref, fre_ref, fim_ref, bre_ref, bim_ref, lam_re_ref, lam_im_ref,
                   xre_ref, xim_ref, zre_ref, zim_ref, state_ref, *, tc):
    i = pl.program_id(0)
    n = pl.num_programs(0)

    @pl.when((flags_ref[i] & 1) > 0)
    def _():
        state_ref[0:2] = jnp.zeros((2, 1, S5_NS), F32)

    @pl.when((flags_ref[n - 1 - i] & 2) > 0)
    def _():
        state_ref[2:4] = jnp.zeros((2, 1, S5_NS), F32)

    flr, fli = lam_re_ref[0], lam_im_ref[0]
    blr, bli = lam_re_ref[1], lam_im_ref[1]

    def body(k, carry):
        xr, xi, zr, zi = carry
        cf = pl.ds(k, 1)
        cb = pl.ds(tc - 1 - k, 1)
        xre_ref[cf, :] = xr
        xim_ref[cf, :] = xi
        zre_ref[cb, :] = zr
        zim_ref[cb, :] = zi
        return (flr * xr - fli * xi + fre_ref[cf, :], flr * xi + fli * xr + fim_ref[cf, :],
                blr * zr - bli * zi + bre_ref[cb, :], blr * zi + bli * zr + bim_ref[cb, :])

    out = lax.fori_loop(0, tc, body, tuple(state_ref[j] for j in range(4)))
    for j in range(4):
        state_ref[j] = out[j]


def _s5_rec(fre, fim, bre, bim, lam_re, lam_im, segs):
    m = fre.shape[0]
    tc = min(128, min(l for _, l in segs) // S5_Q)
    _, flags = _tile_tables(segs, tc * S5_Q)
    n = m // tc
    fwd = pl.BlockSpec((tc, S5_NS), lambda i, f: (i, 0))
    bwd = pl.BlockSpec((tc, S5_NS), lambda i, f: (n - 1 - i, 0))
    lam = pl.BlockSpec((2, 1, S5_NS), lambda i, f: (0, 0, 0))
    grid_spec = pltpu.PrefetchScalarGridSpec(
        num_scalar_prefetch=1, grid=(n,), in_specs=[fwd, fwd, bwd, bwd, lam, lam],
        out_specs=[fwd, fwd, bwd, bwd],
        scratch_shapes=[pltpu.VMEM((4, 1, S5_NS), F32)])
    sds = jax.ShapeDtypeStruct((m, S5_NS), F32)
    return pl.pallas_call(
        functools.partial(_s5_rec_kernel, tc=tc), grid_spec=grid_spec, out_shape=[sds] * 4,
        compiler_params=_params(("arbitrary",)),
    )(jnp.asarray(flags), fre, fim, bre, bim, lam_re, lam_im)


def _s5_out_kernel(u_ref, ttc_ref, xre_ref, xim_ref, zre_ref, zim_ref, vcc_ref, dsk_ref, spread_ref, y_ref,
                   tt_ref, vc_ref, *, mt):
    @pl.when(pl.program_id(1) == 0)
    def _():
        _s5_expand(ttc_ref, spread_ref, tt_ref, S5_GROUP, S5_GROUP)
        _s5_expand(vcc_ref, spread_ref, vc_ref, S5_STATE, S5_GROUP)

    u = _s5_blocks(u_ref, mt)
    intra = jnp.dot(u.astype(BF16), tt_ref[...], preferred_element_type=F32)
    st = jnp.concatenate([xre_ref[...], xim_ref[...], zre_ref[...], zim_ref[...]], axis=1).astype(BF16)
    carry = jnp.dot(st, vc_ref[...], preferred_element_type=F32)
    y = intra + carry + dsk_ref[...] * u
    for t in range(S5_Q):
        y_ref[pl.ds(t, mt, stride=S5_Q), :] = y[:, t * LANES:(t + 1) * LANES]


def _s5_out(pm, w, xre, xim, zre, zim):
    n_tok = pm.shape[0]
    tt = min(S5_TT, n_tok)
    mt = tt // S5_Q
    ucb = COL_U_S5 // LANES
    st = pl.BlockSpec((mt, S5_SW), lambda k, i: (i, k))
    return pl.pallas_call(
        functools.partial(_s5_out_kernel, mt=mt),
        out_shape=jax.ShapeDtypeStruct((n_tok, S5_WIDTH), F32), grid=(S5_NSB, n_tok // tt),
        in_specs=[pl.BlockSpec((tt, LANES), lambda k, i: (i, ucb + k)),
                  pl.BlockSpec((None, S5_SBW, S5_CW), lambda k, i: (k, 0, 0)),
                  st, st, st, st,
                  pl.BlockSpec((None, 4 * S5_SW, S5_CW), lambda k, i: (k, 0, 0)),
                  pl.BlockSpec((None, 1, S5_SBW), lambda k, i: (k, 0, 0)),
                  pl.BlockSpec((S5_CW, S5_SBW), lambda k, i: (0, 0))],
        out_specs=pl.BlockSpec((tt, LANES), lambda k, i: (i, k)),
        scratch_shapes=[pltpu.VMEM((S5_SBW, S5_SBW), BF16), pltpu.VMEM((4 * S5_SW, S5_SBW), BF16)],
        compiler_params=_params(("parallel", "arbitrary")),
    )(pm, w['tt'], xre, xim, zre, zim, w['vc'], w['dsk'], _s5_spread_matrix(S5_GROUP))


def _s5_glu_kernel(y_ref, w_ref, b_ref, o_ref):
    g = jax.nn.gelu(y_ref[...])
    t = jnp.dot(g.astype(BF16), w_ref[...], preferred_element_type=F32) + b_ref[...]
    o_ref[...] = (g * jax.nn.sigmoid(t)).astype(o_ref.dtype)


def _s5_glu(y, glu_w, layer, glu_b):
    n_tok = y.shape[0]
    tr = 512
    return pl.pallas_call(
        _s5_glu_kernel, out_shape=jax.ShapeDtypeStruct((n_tok, S5_WIDTH), BF16), grid=(n_tok // tr,),
        in_specs=[pl.BlockSpec((tr, S5_WIDTH), lambda i: (i, 0)),
                  pl.BlockSpec((None, S5_WIDTH, S5_WIDTH), lambda i: (layer, 0, 0)),
                  pl.BlockSpec((1, S5_WIDTH), lambda i: (0, 0))],
        out_specs=pl.BlockSpec((tr, S5_WIDTH), lambda i: (i, 0)),
        compiler_params=_params(("parallel",)),
    )(y, glu_w, glu_b[None, :])


def _s5(pm, segs, lw, glu_w, layer):
    w = _s5_weights(lw)
    fre, fim, bre, bim = _s5_in(pm, w['wa'])
    xre, xim, zre, zim = _s5_rec(fre, fim, bre, bim, w['lam_re'], w['lam_im'], segs)
    y = _s5_out(pm, w, xre, xim, zre, zim)
    return _s5_glu(y, glu_w, layer, lw['s5_glu_b'])


def _matmul_weights(w_in, w_branch_ssd, w_branch_swa, w_branch_s5, w_branch_na, w_out, ffn_w1, ffn_w3, ffn_w2,
                    s5_glu_w):
    o = IN_OFFS
    w_main = jnp.concatenate([w_in[:, :, :o[2]], w_in[:, :, o[3]:]], axis=2).astype(BF16)
    w_dt_raw = w_in[:, :, o[2]:o[3]]
    zpad = jnp.zeros((DEPTH, D_MODEL, LANES - SSD_HEADS), F32)
    w_dt = jnp.concatenate([w_dt_raw[:, :, :SSD_HEADS], zpad, w_dt_raw[:, :, SSD_HEADS:], zpad],
                           axis=2).astype(BF16)
    w_branch = jnp.stack([w_branch_ssd, w_branch_swa, w_branch_s5, w_branch_na], axis=1).astype(BF16)
    return dict(main=w_main, dt=w_dt, branch=w_branch, out=w_out.astype(BF16),
                w1=ffn_w1.astype(BF16), w3=ffn_w3.astype(BF16), w2=ffn_w2.astype(BF16),
                glu=s5_glu_w.astype(BF16))


def _layer(xs, c8, lw, mw, ada_w, ada_b, layer, split_output):
    mod = _ada(c8, ada_w, ada_b, layer)
    mod4 = mod.reshape(8, 6, 1, D_MODEL)

    h = _norm_mod(xs, lw['norm1_g'], mod4, 1, 0)
    pm = _mm(h, mw['main'], layer, N_MIX, 1024, F32)
    dt_raw = _mm(h, mw['dt'], layer, 2 * LANES, 2 * LANES, F32)

    y_ssd = _ssd(pm, dt_raw, SEGS, lw)
    qs, ks, qn, kn, vn = _qk_prep(pm, SEGS, lw)
    y_swa = _swa(qs, ks, pm, lw['swa_sink'], SEGS)
    y_s5 = _s5(pm, SEGS, lw, mw['glu'], layer)
    y_na = _na(qn, kn, vn, lw['na_rpb'], GROUPS)
    merged = _merge(h, mw['main'], jnp.stack([y_ssd, y_swa, y_s5, y_na]), mw['branch'], layer)
    x = _mm_res(merged, mw['out'], layer, xs, mod4, 2, TM, 1024 // len(xs), D_MODEL)

    h2 = _norm_mod((x,), lw['norm2_g'], mod4, 4, 3)
    u = _ffn_up(h2, mw['w1'], mw['w3'], layer)
    tm = TM // 2
    down = functools.partial(_mm_res, u, mw['w2'], layer, (x,), mod4, 5, tm, 512, D_FF)
    if not split_output:
        return down()
    n_first = GROUPS[1][0] // tm
    return down(rows=(0, n_first)), down(rows=(n_first, N_TOK // tm - n_first))


_LAYER_KEYS = ('ada_w', 'ada_b', 'norm1_g', 'norm2_g', 'w_in',
               'ssd_conv_w', 'ssd_conv_b', 'ssd_dt_bias', 'ssd_a_log', 'ssd_d', 'ssd_norm_g',
               'swa_q_norm_g', 'swa_k_norm_g', 'swa_sink',
               's5_a_re', 's5_a_im', 's5_log_step', 's5_b_re', 's5_b_im', 's5_c_re', 's5_c_im',
               's5_d', 's5_glu_w', 's5_glu_b',
               'na_q_norm_g', 'na_k_norm_g', 'na_rpb',
               'w_branch_ssd', 'w_branch_swa', 'w_branch_s5', 'w_branch_na', 'w_out',
               'ffn_w1', 'ffn_w3', 'ffn_w2')


def kernel(x_prompt, x_sample, c_prompt, c_sample, ada_w, ada_b, norm1_g, norm2_g, w_in, ssd_conv_w, ssd_conv_b, ssd_dt_bias, ssd_a_log, ssd_d, ssd_norm_g, swa_q_norm_g, swa_k_norm_g, swa_sink, s5_a_re, s5_a_im, s5_log_step, s5_b_re, s5_b_im, s5_c_re, s5_c_im, s5_d, s5_glu_w, s5_glu_b, na_q_norm_g, na_k_norm_g, na_rpb, w_branch_ssd, w_branch_swa, w_branch_s5, w_branch_na, w_out, ffn_w1, ffn_w3, ffn_w2):
    stacked = dict(zip(_LAYER_KEYS, (ada_w, ada_b, norm1_g, norm2_g, w_in,
                                     ssd_conv_w, ssd_conv_b, ssd_dt_bias, ssd_a_log, ssd_d, ssd_norm_g,
                                     swa_q_norm_g, swa_k_norm_g, swa_sink,
                                     s5_a_re, s5_a_im, s5_log_step, s5_b_re, s5_b_im, s5_c_re, s5_c_im,
                                     s5_d, s5_glu_w, s5_glu_b,
                                     na_q_norm_g, na_k_norm_g, na_rpb,
                                     w_branch_ssd, w_branch_swa, w_branch_s5, w_branch_na, w_out,
                                     ffn_w1, ffn_w3, ffn_w2)))
    xs = (x_prompt.reshape(BATCH * SEQ, D_MODEL), x_sample.reshape(DEC_BATCH * DEC_SEQ, D_MODEL))
    c8 = jnp.concatenate([c_prompt, c_sample, jnp.zeros((8 - N_SEQS, D_MODEL), F32)], axis=0)
    mw = _matmul_weights(w_in, w_branch_ssd, w_branch_swa, w_branch_s5, w_branch_na, w_out, ffn_w1, ffn_w3, ffn_w2,
                         s5_glu_w)
    for i in range(DEPTH):
        lw = {k: v[i] for k, v in stacked.items() if v.size < D_MODEL * D_MODEL}
        last = i == DEPTH - 1
        out = _layer(xs, c8, lw, mw, ada_w, ada_b, i, split_output=last)
        xs = out if last else (out,)
    return (xs[0].reshape(BATCH, SEQ, D_MODEL), xs[1].reshape(DEC_BATCH, DEC_SEQ, D_MODEL))
```

```python
import functools

import jax
import jax.numpy as jnp
import numpy as np
from jax import lax
from jax.experimental import pallas as pl
from jax.experimental.pallas import tpu as pltpu

D_MODEL = 4096
BATCH = 2
SEQ = 4096
DEPTH = 2
DEC_BATCH = 4
DEC_SEQ = 2048
N_TOK = BATCH * SEQ + DEC_BATCH * DEC_SEQ
N_SEQS = BATCH + DEC_BATCH
GROUPS = ((0, BATCH, SEQ), (BATCH * SEQ, DEC_BATCH, DEC_SEQ))
SEGS = tuple((row0 + i * l, l) for row0, b, l in GROUPS for i in range(b))

EPS = 1e-6
HEAD_DIM = 128
N_BRANCH = 4
SSD_HEADS = 16
SSD_HEAD_DIM = 64
SSD_INNER = SSD_HEADS * SSD_HEAD_DIM
SSD_GROUPS = 2
SSD_STATE = 128
SSD_CONV = 5
SSD_CHUNK = 128
SSD_CONV_DIM = SSD_INNER + 2 * SSD_GROUPS * SSD_STATE
SWA_HEADS = 8
SWA_KV_HEADS = 2
SWA_WIDTH = SWA_HEADS * HEAD_DIM
SWA_KV_WIDTH = SWA_KV_HEADS * HEAD_DIM
SWA_WINDOW = 128
SWA_BLOCK = 128
ROPE_THETA = 10000.0
S5_WIDTH = 1024
S5_GROUP = 16
S5_GROUPS = S5_WIDTH // S5_GROUP
S5_STATE = 64
NA_HEADS = 8
NA_WIDTH = NA_HEADS * HEAD_DIM
GRID_W = 64
NA_KR = 8
NA_KW = 16
D_FF = ((8 * D_MODEL + 3 * 256 - 1) // (3 * 256)) * 256
IN_SIZES = (SSD_INNER, SSD_CONV_DIM, 2 * SSD_HEADS,
            SWA_WIDTH, SWA_KV_WIDTH, SWA_KV_WIDTH,
            S5_WIDTH,
            NA_WIDTH, NA_WIDTH, NA_WIDTH,
            N_BRANCH * D_MODEL)
IN_OFFS = tuple(int(v) for v in np.cumsum((0,) + IN_SIZES))

LANES = 128
TM = 1024
N_MIX = IN_OFFS[10] - IN_SIZES[2]
VMEM_LIMIT = 56 * 1024 * 1024

COL_Z = 0
COL_XBC = COL_Z + SSD_INNER
COL_Q_SWA = COL_XBC + SSD_CONV_DIM
COL_K_SWA = COL_Q_SWA + SWA_WIDTH
COL_V_SWA = COL_K_SWA + SWA_KV_WIDTH
COL_U_S5 = COL_V_SWA + SWA_KV_WIDTH
COL_Q_NA = COL_U_S5 + S5_WIDTH
COL_K_NA = COL_Q_NA + NA_WIDTH
COL_V_NA = COL_K_NA + NA_WIDTH
COL_SSD_END = COL_XBC + SSD_CONV_DIM

F32 = jnp.float32
BF16 = jnp.bfloat16


def _batch_of_tile(i, tile):
    n_p = (BATCH * SEQ) // tile
    return jnp.where(i < n_p, i // (SEQ // tile), BATCH + (i - n_p) // (DEC_SEQ // tile))


def _params(sem):
    return pltpu.CompilerParams(dimension_semantics=sem, vmem_limit_bytes=VMEM_LIMIT)


def _ada_kernel(c_ref, w_ref, b_ref, o_ref):
    c = c_ref[...]
    a = (c * jax.nn.sigmoid(c)).astype(BF16)
    o_ref[...] = jnp.dot(a, w_ref[...].astype(BF16), preferred_element_type=F32) + b_ref[...]


def _ada(c8, ada_w, ada_b, layer):
    n = ada_w.shape[2]
    tn = 512
    return pl.pallas_call(
        _ada_kernel,
        out_shape=jax.ShapeDtypeStruct((8, n), F32),
        grid=(n // tn,),
        in_specs=[pl.BlockSpec((8, D_MODEL), lambda j: (0, 0)),
                  pl.BlockSpec((None, D_MODEL, tn), lambda j: (layer, 0, j)),
                  pl.BlockSpec((1, tn), lambda j: (0, j))],
        out_specs=pl.BlockSpec((8, tn), lambda j: (0, j)),
        compiler_params=_params(("parallel",)),
    )(c8, ada_w, ada_b[layer].reshape(1, n))


def _norm_mod_kernel(*refs, n_first):
    *x_refs, g_ref, scale_ref, shift_ref, o_ref = refs

    def body(x_ref):
        x = x_ref[...]
        y = x * lax.rsqrt(jnp.mean(x * x, axis=-1, keepdims=True) + EPS)
        y = y * g_ref[...]
        o_ref[...] = (y * (1.0 + scale_ref[...]) + shift_ref[...]).astype(o_ref.dtype)

    if len(x_refs) == 1:
        body(x_refs[0])
        return
    i = pl.program_id(0)
    pl.when(i < n_first)(lambda: body(x_refs[0]))
    pl.when(i >= n_first)(lambda: body(x_refs[1]))


def _group_specs(block, n_first):
    def first(i, j=0, *_):
        return (jnp.minimum(i, n_first - 1), jnp.where(i < n_first, j, 0))

    def second(i, j=0, *_):
        return (jnp.maximum(i - n_first, 0), jnp.where(i >= n_first, j, 0))

    return [pl.BlockSpec(block, first), pl.BlockSpec(block, second)]


def _norm_mod(xs, g, mod4, scale_idx, shift_idx):
    tr = 512
    n_first = xs[0].shape[0] // tr
    x_specs = ([pl.BlockSpec((tr, D_MODEL), lambda i: (i, 0))] if len(xs) == 1
               else _group_specs((tr, D_MODEL), n_first))
    return pl.pallas_call(
        functools.partial(_norm_mod_kernel, n_first=n_first),
        out_shape=jax.ShapeDtypeStruct((N_TOK, D_MODEL), BF16),
        grid=(N_TOK // tr,),
        in_specs=x_specs + [
            pl.BlockSpec((1, D_MODEL), lambda i: (0, 0)),
            pl.BlockSpec((None, None, 1, D_MODEL), lambda i: (_batch_of_tile(i, tr), scale_idx, 0, 0)),
            pl.BlockSpec((None, None, 1, D_MODEL), lambda i: (_batch_of_tile(i, tr), shift_idx, 0, 0))],
        out_specs=pl.BlockSpec((tr, D_MODEL), lambda i: (i, 0)),
        compiler_params=_params(("parallel",)),
    )(*xs, g.reshape(1, D_MODEL), mod4, mod4)


def _mm_kernel(a_ref, b_ref, o_ref):
    o_ref[...] = jnp.dot(a_ref[...], b_ref[...], preferred_element_type=F32).astype(o_ref.dtype)


def _mm(a, b, layer, n, tn, out_dtype):
    m, k = a.shape
    return pl.pallas_call(
        _mm_kernel,
        out_shape=jax.ShapeDtypeStruct((m, n), out_dtype),
        grid=(m // TM, n // tn),
        in_specs=[pl.BlockSpec((TM, k), lambda i, j: (i, 0)),
                  pl.BlockSpec((None, k, tn), lambda i, j: (layer, 0, j))],
        out_specs=pl.BlockSpec((TM, tn), lambda i, j: (i, j)),
        compiler_params=_params(("parallel", "parallel")),
    )(a, b)


def _merge_kernel(h_ref, wg_ref, m_ref, wb_ref, o_ref, acc_ref):
    b = pl.program_id(2)
    logits = jnp.dot(h_ref[...], wg_ref[...], preferred_element_type=F32)
    y = jnp.dot(m_ref[...], wb_ref[...], preferred_element_type=F32)
    contrib = jax.nn.sigmoid(logits) * y

    @pl.when(b == 0)
    def _():
        acc_ref[...] = contrib

    @pl.when(b > 0)
    def _():
        acc_ref[...] += contrib

    @pl.when(b == N_BRANCH - 1)
    def _():
        o_ref[...] = acc_ref[...].astype(o_ref.dtype)


def _merge(h, w_main, mix, wb, layer):
    tn = 512
    w = mix.shape[2]
    gate0 = N_MIX // tn
    per_branch = D_MODEL // tn
    return pl.pallas_call(
        _merge_kernel,
        out_shape=jax.ShapeDtypeStruct((N_TOK, D_MODEL), BF16),
        grid=(N_TOK // TM, D_MODEL // tn, N_BRANCH),
        in_specs=[pl.BlockSpec((TM, D_MODEL), lambda i, j, b: (i, 0)),
                  pl.BlockSpec((None, D_MODEL, tn), lambda i, j, b: (layer, 0, gate0 + b * per_branch + j)),
                  pl.BlockSpec((None, TM, w), lambda i, j, b: (b, i, 0)),
                  pl.BlockSpec((None, None, w, tn), lambda i, j, b: (layer, b, 0, j))],
        out_specs=pl.BlockSpec((TM, tn), lambda i, j, b: (i, j)),
        scratch_shapes=[pltpu.VMEM((TM, tn), F32)],
        compiler_params=_params(("parallel", "parallel", "arbitrary")),
    )(h, w_main, mix, wb)


def _mm_res_kernel(a_ref, b_ref, *rest, nk, n_first):
    *x_refs, gate_ref, o_ref, acc_ref = rest
    d = jnp.dot(a_ref[...], b_ref[...], preferred_element_type=F32)

    def finish(total):
        def store(x_ref):
            o_ref[...] = x_ref[...] + gate_ref[...] * total

        if len(x_refs) == 1:
            store(x_refs[0])
            return
        i = pl.program_id(0)
        pl.when(i < n_first)(functools.partial(store, x_refs[0]))
        pl.when(i >= n_first)(functools.partial(store, x_refs[1]))

    if nk == 1:
        finish(d)
        return
    k = pl.program_id(2)

    @pl.when(k == 0)
    def _():
        acc_ref[...] = d

    @pl.when(k > 0)
    def _():
        acc_ref[...] += d

    @pl.when(k == nk - 1)
    def _():
        finish(acc_ref[...])


def _mm_res(a, b, layer, xs, mod4, gate_idx, tm, tn, tk, rows=None):
    kdim = a.shape[1]
    n = b.shape[2]
    nk = kdim // tk
    off, m_tiles = rows if rows is not None else (0, a.shape[0] // tm)
    n_first = xs[0].shape[0] // tm
    x_specs = ([pl.BlockSpec((tm, tn), lambda i, j, k: (i + off, j))] if len(xs) == 1
               else _group_specs((tm, tn), n_first))
    return pl.pallas_call(
        functools.partial(_mm_res_kernel, nk=nk, n_first=n_first),
        out_shape=jax.ShapeDtypeStruct((m_tiles * tm, n), F32),
        grid=(m_tiles, n // tn, nk),
        in_specs=[pl.BlockSpec((tm, tk), lambda i, j, k: (i + off, k)),
                  pl.BlockSpec((None, tk, tn), lambda i, j, k: (layer, k, j))] + x_specs + [
                  pl.BlockSpec((None, None, 1, tn),
                               lambda i, j, k: (_batch_of_tile(i + off, tm), gate_idx, 0, j))],
        out_specs=pl.BlockSpec((tm, tn), lambda i, j, k: (i, j)),
        scratch_shapes=[pltpu.VMEM((tm, tn), F32)],
        compiler_params=_params(("parallel", "parallel", "arbitrary")),
    )(a, b, *xs, mod4)


def _ffn_up_kernel(h_ref, w1_ref, w3_ref, o_ref):
    h = h_ref[...]
    a = jnp.dot(h, w1_ref[...], preferred_element_type=F32)
    b = jnp.dot(h, w3_ref[...], preferred_element_type=F32)
    o_ref[...] = (a * jax.nn.sigmoid(a) * b).astype(o_ref.dtype)


def _ffn_up(h, w1, w3, layer):
    tn = 512
    n = w1.shape[2]
    return pl.pallas_call(
        _ffn_up_kernel,
        out_shape=jax.ShapeDtypeStruct((N_TOK, n), BF16),
        grid=(N_TOK // TM, pl.cdiv(n, tn)),
        in_specs=[pl.BlockSpec((TM, D_MODEL), lambda i, j: (i, 0)),
                  pl.BlockSpec((None, D_MODEL, tn), lambda i, j: (layer, 0, j)),
                  pl.BlockSpec((None, D_MODEL, tn), lambda i, j: (layer, 0, j))],
        out_specs=pl.BlockSpec((TM, tn), lambda i, j: (i, j)),
        compiler_params=_params(("parallel", "parallel")),
    )(h, w1, w3)


def _tile_tables(segs, tile):
    pos, flags = [], []
    for _, length in segs:
        n = length // tile
        for j in range(n):
            pos.append(j)
            flags.append((1 if j == 0 else 0) | (2 if j == n - 1 else 0))
    return np.asarray(pos, np.int32), np.asarray(flags, np.int32)


def _head_slice(h):
    return slice(h * HEAD_DIM, (h + 1) * HEAD_DIM)


def _qk_prep_kernel(pos_ref, qs_lo_ref, qs_hi_ref, ks_ref, qn_ref, kn_ref, vn_ref, cos_ref, sin_ref, g_ref,
                    oqs_ref, oks_ref, oqn_ref, okn_ref, ovn_ref):
    del pos_ref
    half_heads = SWA_HEADS // 2
    cos = cos_ref[...]
    sin = sin_ref[...]

    def hnorm(x, g):
        return x * lax.rsqrt(jnp.mean(x * x, axis=-1, keepdims=True) + EPS) * g

    def rope(x):
        return x * cos + pltpu.roll(x, HEAD_DIM // 2, 1) * sin

    for h in range(SWA_HEADS):
        src = qs_lo_ref if h < half_heads else qs_hi_ref
        oqs_ref[:, _head_slice(h)] = rope(hnorm(src[:, _head_slice(h % half_heads)], g_ref[0:1, :])).astype(BF16)
    for h in range(SWA_KV_HEADS):
        oks_ref[:, _head_slice(h)] = rope(hnorm(ks_ref[:, _head_slice(h)], g_ref[1:2, :])).astype(BF16)
    for h in range(NA_HEADS):
        oqn_ref[:, _head_slice(h)] = hnorm(qn_ref[:, _head_slice(h)], g_ref[2:3, :]).astype(BF16)
        okn_ref[:, _head_slice(h)] = hnorm(kn_ref[:, _head_slice(h)], g_ref[3:4, :]).astype(BF16)
    ovn_ref[...] = vn_ref[...].astype(BF16)


def _rope_tables(max_len):
    half = HEAD_DIM // 2
    inv_freq = ROPE_THETA ** (-jnp.arange(half, dtype=F32) / half)
    ang = jnp.arange(max_len, dtype=F32)[:, None] * inv_freq[None, :]
    cos, sin = jnp.cos(ang), jnp.sin(ang)
    return jnp.concatenate([cos, cos], axis=1), jnp.concatenate([-sin, sin], axis=1)


def _qk_prep(pm, segs, lw):
    n_tok = pm.shape[0]
    tr = 512
    pos, _ = _tile_tables(segs, tr)
    cos, sin = _rope_tables(max(l for _, l in segs))
    gains = jnp.concatenate([lw['swa_q_norm_g'][None], lw['swa_k_norm_g'][None],
                             lw['na_q_norm_g'][None], lw['na_k_norm_g'][None],
                             jnp.zeros((4, HEAD_DIM), F32)], axis=0)
    wide = lambda cb: pl.BlockSpec((tr, 1024), lambda i, p: (i, cb))
    narrow = lambda cb: pl.BlockSpec((tr, 256), lambda i, p: (i, cb))
    grid_spec = pltpu.PrefetchScalarGridSpec(
        num_scalar_prefetch=1, grid=(n_tok // tr,),
        in_specs=[pl.BlockSpec((tr, 512), lambda i, p: (i, COL_Q_SWA // 512)),
                  pl.BlockSpec((tr, 512), lambda i, p: (i, COL_Q_SWA // 512 + 1)),
                  narrow(COL_K_SWA // 256), wide(COL_Q_NA // 1024),
                  wide(COL_K_NA // 1024), wide(COL_V_NA // 1024),
                  pl.BlockSpec((tr, HEAD_DIM), lambda i, p: (p[i], 0)),
                  pl.BlockSpec((tr, HEAD_DIM), lambda i, p: (p[i], 0)),
                  pl.BlockSpec((8, HEAD_DIM), lambda i, p: (0, 0))],
        out_specs=[pl.BlockSpec((tr, 1024), lambda i, p: (i, 0)),
                   pl.BlockSpec((tr, 256), lambda i, p: (i, 0)),
                   pl.BlockSpec((tr, 1024), lambda i, p: (i, 0)),
                   pl.BlockSpec((tr, 1024), lambda i, p: (i, 0)),
                   pl.BlockSpec((tr, 1024), lambda i, p: (i, 0))])
    sds = lambda w: jax.ShapeDtypeStruct((n_tok, w), BF16)
    return pl.pallas_call(
        _qk_prep_kernel, grid_spec=grid_spec,
        out_shape=[sds(1024), sds(256), sds(1024), sds(1024), sds(1024)],
        compiler_params=_params(("parallel",)),
    )(jnp.asarray(pos), pm, pm, pm, pm, pm, pm, cos, sin, gains)


SWA_TQ = 512
SWA_GRP = SWA_HEADS // SWA_KV_HEADS


def _swa_kernel(flags_ref, sink_ref, q_ref, kc_ref, kp_ref, kn_ref, vc_ref, vp_ref, vn_ref, mix_ref, o_ref, *, tq):
    del mix_ref
    fl = flags_ref[pl.program_id(0)]
    lo = jnp.where((fl & 1) > 0, SWA_BLOCK, 0)
    hi = jnp.where((fl & 2) > 0, 2 * SWA_BLOCK, 3 * SWA_BLOCK)
    nqb = tq // SWA_BLOCK
    m_rows = SWA_GRP * SWA_BLOCK
    row = lax.broadcasted_iota(jnp.int32, (m_rows, 3 * SWA_BLOCK), 0) & (SWA_BLOCK - 1)
    col = lax.broadcasted_iota(jnp.int32, (m_rows, 3 * SWA_BLOCK), 1)
    band = (col >= row) & (col <= row + 2 * SWA_WINDOW)
    scale = HEAD_DIM ** -0.5
    for g in range(SWA_KV_HEADS):
        ks = _head_slice(g)
        k_ext = jnp.concatenate([kp_ref[:, ks], kc_ref[:, ks], kn_ref[:, ks]], axis=0)
        v_ext = jnp.concatenate([vp_ref[:, ks], vc_ref[:, ks], vn_ref[:, ks]], axis=0).astype(BF16)
        sk = jnp.concatenate([jnp.full((SWA_BLOCK, 1), sink_ref[g * SWA_GRP + h], F32)
                              for h in range(SWA_GRP)], axis=0)
        for qb in range(nqb):
            rows = slice(qb * SWA_BLOCK, (qb + 1) * SWA_BLOCK)
            q = jnp.concatenate([q_ref[rows, _head_slice(g * SWA_GRP + h)] for h in range(SWA_GRP)], axis=0)
            keys = k_ext[qb * SWA_BLOCK:(qb + 3) * SWA_BLOCK]
            vals = v_ext[qb * SWA_BLOCK:(qb + 3) * SWA_BLOCK]
            s = lax.dot_general(q, keys, (((1,), (1,)), ((), ())), preferred_element_type=F32) * scale
            mask = band
            if qb == 0:
                mask = mask & (col >= lo)
            if qb == nqb - 1:
                mask = mask & (col < hi)
            s = jnp.where(mask, s, -jnp.inf)
            m = jnp.maximum(jnp.max(s, axis=-1, keepdims=True), sk)
            p = jnp.exp(s - m)
            denom = jnp.sum(p, axis=-1, keepdims=True) + jnp.exp(sk - m)
            o = jnp.dot((p / denom).astype(BF16), vals, preferred_element_type=F32)
            for h in range(SWA_GRP):
                o_ref[rows, _head_slice(g * SWA_GRP + h)] = o[h * SWA_BLOCK:(h + 1) * SWA_BLOCK].astype(o_ref.dtype)


MIX_SSD, MIX_SWA, MIX_S5, MIX_NA = range(N_BRANCH)
_WHOLE = pl.BlockSpec(memory_space=pl.ANY)


def _swa(qs, ks, pm, sink, segs, mix):
    n_tok = qs.shape[0]
    tq = min(SWA_TQ, min(l for _, l in segs))
    _, flags = _tile_tables(segs, tq)
    nb = tq // SWA_BLOCK
    last_blk = n_tok // SWA_BLOCK - 1
    prev_map = lambda cb: (lambda i, f: (jnp.maximum(i * nb - 1, 0), cb))
    next_map = lambda cb: (lambda i, f: (jnp.minimum((i + 1) * nb, last_blk), cb))
    vcb = COL_V_SWA // SWA_KV_WIDTH
    grid_spec = pltpu.PrefetchScalarGridSpec(
        num_scalar_prefetch=1, grid=(n_tok // tq,),
        in_specs=[pl.BlockSpec(memory_space=pltpu.SMEM),
                  pl.BlockSpec((tq, SWA_WIDTH), lambda i, f: (i, 0)),
                  pl.BlockSpec((tq, SWA_KV_WIDTH), lambda i, f: (i, 0)),
                  pl.BlockSpec((SWA_BLOCK, SWA_KV_WIDTH), prev_map(0)),
                  pl.BlockSpec((SWA_BLOCK, SWA_KV_WIDTH), next_map(0)),
                  pl.BlockSpec((tq, SWA_KV_WIDTH), lambda i, f: (i, vcb)),
                  pl.BlockSpec((SWA_BLOCK, SWA_KV_WIDTH), prev_map(vcb)),
                  pl.BlockSpec((SWA_BLOCK, SWA_KV_WIDTH), next_map(vcb)),
                  _WHOLE],
        out_specs=pl.BlockSpec((None, tq, SWA_WIDTH), lambda i, f: (MIX_SWA, i, 0)))
    return pl.pallas_call(
        functools.partial(_swa_kernel, tq=tq), grid_spec=grid_spec,
        out_shape=jax.ShapeDtypeStruct(mix.shape, mix.dtype), input_output_aliases={9: 0},
        compiler_params=_params(("parallel",)),
    )(jnp.asarray(flags), sink, qs, ks, ks, ks, pm, pm, pm, mix)


NA_R = NA_KR // 2
NA_WIN = NA_R + NA_KR


def _na_bias_table(rpb):
    off = np.array([np.zeros(NA_R, int), np.arange(NA_R), np.full(NA_R, NA_R)])
    p_of = np.array([np.arange(NA_R), np.full(NA_R, NA_KR // 2), NA_KR // 2 + np.arange(NA_R)])
    j = np.arange(NA_WIN)[None, None, :] - off[:, :, None]
    row_ok = (j >= 0) & (j < NA_KR)
    dr = np.clip(j - p_of[:, :, None] + (NA_KR - 1), 0, 2 * NA_KR - 2)
    qc = np.arange(GRID_W)
    kc = np.arange(GRID_W)
    dc = np.clip(kc[None, :] - qc[:, None], -(NA_KW - 1), NA_KW - 1) + (NA_KW - 1)
    col_start = np.clip(qc - NA_KW // 2, 0, GRID_W - NA_KW)
    valid = (kc[None, :] >= col_start[:, None]) & (kc[None, :] < col_start[:, None] + NA_KW)
    hp = lax.Precision.HIGHEST
    sel_dc = jnp.asarray(np.eye(2 * NA_KW - 1, dtype=np.float32)[dc.reshape(-1)])
    sel_dr = jnp.asarray(np.eye(2 * NA_KR - 1, dtype=np.float32)[dr.reshape(-1)])
    cols = jnp.einsum('hab,xb->hax', rpb, sel_dc, precision=hp)
    bias = jnp.einsum('ya,hax->yhx', sel_dr, cols, precision=hp)
    bias = bias.reshape(3, NA_R, NA_WIN, NA_HEADS, GRID_W, GRID_W)
    ok = row_ok[:, :, :, None, None, None] & valid[None, None, None, None, :, :]
    bias = jnp.where(ok, bias, -jnp.inf)
    return jnp.transpose(bias, (0, 3, 1, 4, 2, 5)).reshape(3, NA_HEADS, NA_R * GRID_W, NA_WIN * GRID_W)


def _na_kernel(q_ref, k_ref, v_ref, bias_ref, mix_ref, o_ref, *, rows):
    del mix_ref
    first_row = jnp.clip(pl.program_id(1) * NA_R - NA_KR // 2, 0, rows - NA_WIN)
    keys = pl.ds(pl.multiple_of(first_row * GRID_W, GRID_W), NA_WIN * GRID_W)
    scale = HEAD_DIM ** -0.5
    for h in range(NA_HEADS):
        s = lax.dot_general(q_ref[:, _head_slice(h)], k_ref[keys, _head_slice(h)], (((1,), (1,)), ((), ())),
                            preferred_element_type=F32) * scale
        s = s + bias_ref[h]
        p = jnp.exp(s - jnp.max(s, axis=-1, keepdims=True))
        p = p / jnp.sum(p, axis=-1, keepdims=True)
        o = jnp.dot(p.astype(BF16), v_ref[keys, _head_slice(h)], preferred_element_type=F32)
        o_ref[:, _head_slice(h)] = o.astype(o_ref.dtype)


def _na_group(qn, kn, vn, bias, row0, b, l, mix):
    rows = l // GRID_W
    nsteps = rows // NA_R
    tq = NA_R * GRID_W
    kind = lambda st: jnp.where(st == 0, 0, jnp.where(st == nsteps - 1, 2, 1))
    seq_spec = pl.BlockSpec((l, NA_WIDTH), lambda bi, st: (row0 // l + bi, 0))
    return pl.pallas_call(
        functools.partial(_na_kernel, rows=rows),
        out_shape=jax.ShapeDtypeStruct(mix.shape, mix.dtype), input_output_aliases={4: 0},
        grid=(b, nsteps),
        in_specs=[pl.BlockSpec((tq, NA_WIDTH), lambda bi, st: (row0 // tq + bi * nsteps + st, 0)),
                  seq_spec, seq_spec,
                  pl.BlockSpec((None, NA_HEADS, tq, NA_WIN * GRID_W), lambda bi, st: (kind(st), 0, 0, 0)),
                  _WHOLE],
        out_specs=pl.BlockSpec((None, tq, NA_WIDTH),
                               lambda bi, st: (MIX_NA, row0 // tq + bi * nsteps + st, 0)),
        compiler_params=_params(("parallel", "arbitrary")),
    )(qn, kn, vn, bias, mix)


def _na(qn, kn, vn, rpb, groups, mix):
    bias = _na_bias_table(rpb)
    for row0, b, l in groups:
        mix = _na_group(qn, kn, vn, bias, row0, b, l, mix)
    return mix


def _ssd_prep_kernel(flags_ref, xc_ref, xp_ref, xn_ref, dt_ref, w_ref, b_ref, dtb_ref,
                     xs_ref, bc_ref, dtf_ref, *, tr):
    fl = flags_ref[pl.program_id(0)]
    halo = 8
    xbc = slice(COL_XBC, COL_SSD_END)
    xp = jnp.where((fl & 1) > 0, 0.0, xp_ref[:, xbc])
    xn = jnp.where((fl & 2) > 0, 0.0, xn_ref[:, xbc])
    ext = jnp.concatenate([xp, xc_ref[:, xbc], xn], axis=0)
    n = tr + 2 * halo
    acc = None
    for k in range(SSD_CONV):
        sh = (SSD_CONV // 2 - k) % n
        xk = ext if sh == 0 else pltpu.roll(ext, sh, 0)
        term = xk[halo:halo + tr] * w_ref[k:k + 1, :]
        acc = term if acc is None else acc + term
    acc = acc + b_ref[...]
    y = acc * jax.nn.sigmoid(acc)
    xs_ref[...] = y[:, :SSD_INNER]
    bc_ref[...] = y[:, SSD_INNER:]
    t = dt_ref[...] + dtb_ref[...]
    dtf_ref[...] = jnp.maximum(t, 0.0) + jnp.log1p(jnp.exp(-jnp.abs(t)))


def _dt_pad(v):
    z = jnp.zeros((LANES - SSD_HEADS,), v.dtype)
    return jnp.concatenate([v[0], z, v[1], z])


def _ssd_prep(pm, dt_raw, segs, lw):
    n_tok = pm.shape[0]
    tr = 512
    _, flags = _tile_tables(segs, tr)
    nb8 = tr // 8
    last8 = n_tok // 8 - 1
    w8 = jnp.concatenate([lw['ssd_conv_w'], jnp.zeros((8 - SSD_CONV, SSD_CONV_DIM), F32)], axis=0)
    dtb = _dt_pad(lw['ssd_dt_bias'])[None, :]
    grid_spec = pltpu.PrefetchScalarGridSpec(
        num_scalar_prefetch=1, grid=(n_tok // tr,),
        in_specs=[pl.BlockSpec((tr, COL_SSD_END), lambda i, f: (i, 0)),
                  pl.BlockSpec((8, COL_SSD_END), lambda i, f: (jnp.maximum(i * nb8 - 1, 0), 0)),
                  pl.BlockSpec((8, COL_SSD_END), lambda i, f: (jnp.minimum((i + 1) * nb8, last8), 0)),
                  pl.BlockSpec((tr, 2 * LANES), lambda i, f: (i, 0)),
                  pl.BlockSpec((8, SSD_CONV_DIM), lambda i, f: (0, 0)),
                  pl.BlockSpec((1, SSD_CONV_DIM), lambda i, f: (0, 0)),
                  pl.BlockSpec((1, 2 * LANES), lambda i, f: (0, 0))],
        out_specs=[pl.BlockSpec((tr, SSD_INNER), lambda i, f: (i, 0)),
                   pl.BlockSpec((tr, 2 * SSD_GROUPS * SSD_STATE), lambda i, f: (i, 0)),
                   pl.BlockSpec((tr, 2 * LANES), lambda i, f: (i, 0))])
    return pl.pallas_call(
        functools.partial(_ssd_prep_kernel, tr=tr), grid_spec=grid_spec,
        out_shape=[jax.ShapeDtypeStruct((n_tok, SSD_INNER), F32),
                   jax.ShapeDtypeStruct((n_tok, 2 * SSD_GROUPS * SSD_STATE), F32),
                   jax.ShapeDtypeStruct((n_tok, 2 * LANES), F32)],
        compiler_params=_params(("parallel",)),
    )(jnp.asarray(flags), pm, pm, pm, dt_raw, w8, lw['ssd_conv_b'][None, :], dtb)


def _split3(x):
    hi = x.astype(BF16)
    r = x - hi.astype(F32)
    mid = r.astype(BF16)
    lo = (r - mid.astype(F32)).astype(BF16)
    return hi, mid, lo


def _dot_sel_l(sel, x):
    hi, mid, lo = _split3(x)
    d = lambda t: jnp.dot(sel, t, preferred_element_type=F32)
    return d(lo) + d(mid) + d(hi)


def _dot_sel_r(x, sel):
    hi, mid, lo = _split3(x)
    d = lambda t: jnp.dot(t, sel, preferred_element_type=F32)
    return d(lo) + d(mid) + d(hi)


def _ssd_scan_kernel(order_ref, reset_ref, xs_ref, bc_ref, dt_ref, alog_ref, e_ref, *rest, bwd):
    if bwd:
        yf_ref, z_ref, dsk_ref, g_ref, o_ref, state_ref = rest
    else:
        o_ref, state_ref = rest
    del order_ref
    q = SSD_CHUNK
    n_pairs = SSD_HEADS // 2
    gs = SSD_GROUPS * SSD_STATE

    @pl.when(reset_ref[pl.program_id(0)] > 0)
    def _():
        state_ref[...] = jnp.zeros_like(state_ref)

    dt = dt_ref[...]
    da = dt * (-jnp.exp(alog_ref[...]))
    r_i = lax.broadcasted_iota(jnp.int32, (q, q), 0)
    c_i = lax.broadcasted_iota(jnp.int32, (q, q), 1)
    incl = (r_i <= c_i) if bwd else (r_i >= c_i)
    tri = jnp.where(incl, 1.0, 0.0).astype(BF16)
    cs = _dot_sel_l(tri, da)
    cs_t = cs.T
    e = e_ref[...]
    dt_x = _dot_sel_r(dt, e)
    cs_x = _dot_sel_r(cs, e)
    total = cs_x[0:1, :] if bwd else cs_x[q - 1:q, :]
    xs = xs_ref[...]
    xdt = xs * dt_x
    w_state = (xdt * jnp.exp(total - cs_x)).astype(BF16)
    xdt_b = xdt.astype(BF16)
    ecs_x = jnp.exp(cs_x)
    etot = jnp.exp(total)
    bc = bc_ref[...]
    b_t = bc[:, :gs].T
    lane = lax.broadcasted_iota(jnp.int32, (q, LANES), 1)
    ys = []
    for g in range(SSD_GROUPS):
        b_g = bc[:, g * SSD_STATE:(g + 1) * SSD_STATE].astype(BF16)
        c_g = bc[:, gs + g * SSD_STATE:gs + (g + 1) * SSD_STATE].astype(BF16)
        bt_g = b_t[g * SSD_STATE:(g + 1) * SSD_STATE, :].astype(BF16)
        cb = lax.dot_general(c_g, b_g, (((1,), (1,)), ((), ())), preferred_element_type=F32)
        for j in range(n_pairs // SSD_GROUPS):
            pair = g * (n_pairs // SSD_GROUPS) + j
            lanes = slice(pair * LANES, (pair + 1) * LANES)
            halves = []
            for hh in range(2):
                h = 2 * pair + hh
                diff = jnp.broadcast_to(cs[:, h:h + 1], (q, q)) - jnp.broadcast_to(cs_t[h:h + 1, :], (q, q))
                decay = jnp.where(incl, jnp.exp(diff), 0.0)
                halves.append(jnp.dot((cb * decay).astype(BF16), xdt_b[:, lanes], preferred_element_type=F32))
            y_diag = jnp.where(lane < SSD_HEAD_DIM, halves[0], halves[1])
            s_prev = state_ref[pair]
            y_off = jnp.dot(c_g, s_prev.astype(BF16), preferred_element_type=F32) * ecs_x[:, lanes]
            contrib = jnp.dot(bt_g, w_state[:, lanes], preferred_element_type=F32)
            state_ref[pair] = s_prev * etot[:, lanes] + contrib
            ys.append(y_diag + y_off)
    y = jnp.concatenate(ys, axis=1)
    if not bwd:
        o_ref[...] = y
        return
    y = yf_ref[...] + y + dsk_ref[...] * xs
    z = z_ref[...]
    y = y * (z * jax.nn.sigmoid(z))
    y = y * lax.rsqrt(jnp.mean(y * y, axis=-1, keepdims=True) + EPS)
    o_ref[...] = (y * g_ref[...]).astype(o_ref.dtype)


def _ssd_scan(xs, bc, dtf, alog, e_mat, segs, bwd, extra=()):
    n_tok = xs.shape[0]
    q = SSD_CHUNK
    _, flags = _tile_tables(segs, q)
    n = n_tok // q
    order = np.arange(n, dtype=np.int32)[::-1].copy() if bwd else np.arange(n, dtype=np.int32)
    reset = ((flags[order] & (2 if bwd else 1)) > 0).astype(np.int32)
    d = 1 if bwd else 0
    row = lambda w, cb=0: pl.BlockSpec((q, w), lambda i, o, r: (o[i], cb))
    const = lambda shape: pl.BlockSpec(shape, lambda i, o, r: (0,) * len(shape))
    in_specs = [row(SSD_INNER), row(2 * SSD_GROUPS * SSD_STATE), row(LANES, d),
                pl.BlockSpec((None, 1, LANES), lambda i, o, r: (d, 0, 0)), const((LANES, SSD_INNER))]
    if bwd:
        in_specs += [row(SSD_INNER), row(SSD_INNER, COL_Z // SSD_INNER), const((1, SSD_INNER)),
                     const((1, SSD_INNER))]
    out_spec = (pl.BlockSpec((None, q, SSD_INNER), lambda i, o, r: (MIX_SSD, o[i], 0)) if bwd
                else row(SSD_INNER))
    out_shape = (jax.ShapeDtypeStruct((N_BRANCH, n_tok, SSD_INNER), BF16) if bwd
                 else jax.ShapeDtypeStruct((n_tok, SSD_INNER), F32))
    grid_spec = pltpu.PrefetchScalarGridSpec(
        num_scalar_prefetch=2, grid=(n,), in_specs=in_specs, out_specs=out_spec,
        scratch_shapes=[pltpu.VMEM((SSD_HEADS // 2, SSD_STATE, LANES), F32)])
    return pl.pallas_call(
        functools.partial(_ssd_scan_kernel, bwd=bwd), grid_spec=grid_spec,
        out_shape=out_shape,
        compiler_params=_params(("arbitrary",)),
    )(jnp.asarray(order), jnp.asarray(reset), xs, bc, dtf, alog, e_mat, *extra)


def _ssd(pm, dt_raw, segs, lw):
    xs, bc, dtf = _ssd_prep(pm, dt_raw, segs, lw)
    alog = _dt_pad(lw['ssd_a_log']).reshape(2, 1, LANES)
    e_np = np.zeros((LANES, SSD_INNER), np.float32)
    for h in range(SSD_HEADS):
        e_np[h, h * SSD_HEAD_DIM:(h + 1) * SSD_HEAD_DIM] = 1.0
    e_mat = jnp.asarray(e_np, BF16)
    y_fwd = _ssd_scan(xs, bc, dtf, alog, e_mat, segs, False)
    dsk = jnp.repeat(lw['ssd_d'], SSD_HEAD_DIM)[None, :]
    return _ssd_scan(xs, bc, dtf, alog, e_mat, segs, True,
                     extra=(y_fwd, pm, dsk, lw['ssd_norm_g'][None, :]))


S5_Q = 16
S5_SB = LANES // S5_GROUP
S5_NSB = S5_GROUPS // S5_SB
S5_SBW = S5_Q * LANES
S5_SW = S5_SB * S5_STATE
S5_NS = S5_GROUPS * S5_STATE
S5_CW = S5_Q * S5_GROUP
assert 4 * S5_STATE == S5_CW


def _s5_weights(lw):
    hp = lax.Precision.HIGHEST
    qn, g, p, c = S5_Q, S5_GROUPS, S5_STATE, S5_GROUP
    nsb, sb = S5_NSB, S5_SB
    cmul = lambda a, b: (a[0] * b[0] - a[1] * b[1], a[0] * b[1] + a[1] * b[0])
    a_re, a_im = lw['s5_a_re'], lw['s5_a_im']
    step = jnp.exp(lw['s5_log_step'])[..., None]
    mag = jnp.exp(a_re * step)
    lam_bar = (mag * jnp.cos(a_im * step), mag * jnp.sin(a_im * step))
    den = a_re * a_re + a_im * a_im
    coef = cmul((lam_bar[0] - 1.0, lam_bar[1]), (a_re / den, -a_im / den))
    b_bar = cmul((coef[0][..., None], coef[1][..., None]),
                 (lw['s5_b_re'][None], lw['s5_b_im'][None]))
    c_c = (lw['s5_c_re'], lw['s5_c_im'])
    pows = [(jnp.ones_like(mag), jnp.zeros_like(mag))]
    for _ in range(qn):
        pows.append(cmul(pows[-1], lam_bar))
    pw = (jnp.stack([t[0] for t in pows], axis=1), jnp.stack([t[1] for t in pows], axis=1))

    t1 = cmul((c_c[0][:, None], c_c[1][:, None]),
              (pw[0][:, :qn, :, None, :], pw[1][:, :qn, :, None, :]))
    lag = jnp.einsum('dmgiq,dgqj->dmgij', jnp.concatenate([t1[0], -t1[1]], axis=-1),
                     jnp.concatenate([b_bar[0], b_bar[1]], axis=2), precision=hp)

    s_idx = np.arange(qn)[:, None]
    t_idx = np.arange(qn)[None, :]
    df = t_idx - s_idx
    kf = lag[0][np.clip(df, 0, qn - 1)] * jnp.asarray(df >= 0, F32)[:, :, None, None, None]
    kb = lag[1][np.clip(-df, 0, qn - 1)] * jnp.asarray(df <= 0, F32)[:, :, None, None, None]
    tt = jnp.transpose(kf + kb, (2, 0, 4, 1, 3))
    by_block_row = lambda a: jnp.transpose(a.reshape(nsb, sb, qn, c, S5_CW), (0, 2, 1, 3, 4)).reshape(
        nsb, S5_SBW, S5_CW)
    tt = by_block_row(tt)

    def in_op(d, powers):
        w = cmul((pw[0][d, powers][:, :, :, None], pw[1][d, powers][:, :, :, None]),
                 (b_bar[0][d][None], b_bar[1][d][None]))
        return tuple(jnp.transpose(t, (1, 0, 3, 2)) for t in w)

    wf = in_op(0, qn - 1 - np.arange(qn))
    wb = in_op(1, np.arange(qn))
    wa = by_block_row(jnp.stack([wf[0], wf[1], wb[0], wb[1]], axis=3))

    def out_op(d, powers):
        return cmul((jnp.transpose(c_c[0][d], (0, 2, 1))[:, :, None, :],
                     jnp.transpose(c_c[1][d], (0, 2, 1))[:, :, None, :]),
                    (jnp.transpose(pw[0][d, powers], (1, 2, 0))[:, :, :, None],
                     jnp.transpose(pw[1][d, powers], (1, 2, 0))[:, :, :, None]))

    vf = out_op(0, 1 + np.arange(qn))
    vb = out_op(1, qn - np.arange(qn))
    v4 = jnp.stack([vf[0], -vf[1], vb[0], -vb[1]], axis=0)
    vc = jnp.transpose(v4.reshape(4, nsb, sb * p, S5_CW), (1, 0, 2, 3)).reshape(nsb, 4 * S5_SW, S5_CW)

    lam_q = (pw[0][:, qn].reshape(2, 1, S5_NS), pw[1][:, qn].reshape(2, 1, S5_NS))
    dsk = jnp.broadcast_to(lw['s5_d'].reshape(nsb, 1, 1, LANES), (nsb, 1, qn, LANES)).reshape(nsb, 1, S5_SBW)
    return dict(tt=tt.astype(BF16), wa=wa.astype(BF16), vc=vc.astype(BF16),
                lam_re=lam_q[0], lam_im=lam_q[1], dsk=dsk)


S5_TT = 4096


def _s5_blocks(u_ref, mt):
    return jnp.concatenate([u_ref[pl.ds(t, mt, stride=S5_Q), :] for t in range(S5_Q)], axis=1)


def _s5_spread_matrix(inner):
    src = np.arange(S5_CW)
    dst = np.arange(S5_SBW)
    same = (src[:, None] // inner == dst[None, :] // (S5_SB * inner)) & (src[:, None] % inner == dst[None, :] % inner)
    return jnp.asarray(same, BF16)


def _s5_expand(c_ref, spread_ref, out_ref, row_inner, col_inner):
    rows = S5_CW
    col_g = (lax.broadcasted_iota(jnp.int32, (rows, S5_SBW), 1) // col_inner) % S5_SB
    row_l = lax.broadcasted_iota(jnp.int32, (rows, S5_SBW), 0)
    for r0 in range(0, S5_SBW, rows):
        row_g = ((row_l + r0) // row_inner) % S5_SB
        full = jnp.dot(c_ref[r0:r0 + rows, :], spread_ref[...], preferred_element_type=F32)
        out_ref[r0:r0 + rows, :] = jnp.where(row_g == col_g, full, 0.0).astype(out_ref.dtype)


def _s5_in_kernel(u_ref, wc_ref, spread_ref, fre_ref, fim_ref, bre_ref, bim_ref, w_ref, *, mt):
    @pl.when(pl.program_id(1) == 0)
    def _():
        _s5_expand(wc_ref, spread_ref, w_ref, S5_GROUP, S5_STATE)

    r = jnp.dot(_s5_blocks(u_ref, mt).astype(BF16), w_ref[...], preferred_element_type=F32)
    fre_ref[...] = r[:, 0 * S5_SW:1 * S5_SW]
    fim_ref[...] = r[:, 1 * S5_SW:2 * S5_SW]
    bre_ref[...] = r[:, 2 * S5_SW:3 * S5_SW]
    bim_ref[...] = r[:, 3 * S5_SW:4 * S5_SW]


def _s5_in(pm, wa):
    n_tok = pm.shape[0]
    tt = min(S5_TT, n_tok)
    mt = tt // S5_Q
    ucb = COL_U_S5 // LANES
    st = pl.BlockSpec((mt, S5_SW), lambda k, i: (i, k))
    sds = jax.ShapeDtypeStruct((n_tok // S5_Q, S5_NS), F32)
    return pl.pallas_call(
        functools.partial(_s5_in_kernel, mt=mt), out_shape=[sds] * 4, grid=(S5_NSB, n_tok // tt),
        in_specs=[pl.BlockSpec((tt, LANES), lambda k, i: (i, ucb + k)),
                  pl.BlockSpec((None, S5_SBW, S5_CW), lambda k, i: (k, 0, 0)),
                  pl.BlockSpec((S5_CW, S5_SBW), lambda k, i: (0, 0))],
        out_specs=[st] * 4,
        scratch_shapes=[pltpu.VMEM((S5_SBW, 4 * S5_SW), BF16)],
        compiler_params=_params(("parallel", "arbitrary")),
    )(pm, wa, _s5_spread_matrix(S5_STATE))


def _s5_rec_kernel(flags_ref, fre_ref, fim_ref, bre_ref, bim_ref, lam_re_ref, lam_im_ref,
                   xre_ref, xim_ref, zre_ref, zim_ref, state_ref, *, tc):
    i = pl.program_id(0)
    n = pl.num_programs(0)

    @pl.when((flags_ref[i] & 1) > 0)
    def _():
        state_ref[0:2] = jnp.zeros((2, 1, S5_NS), F32)

    @pl.when((flags_ref[n - 1 - i] & 2) > 0)
    def _():
        state_ref[2:4] = jnp.zeros((2, 1, S5_NS), F32)

    flr, fli = lam_re_ref[0], lam_im_ref[0]
    blr, bli = lam_re_ref[1], lam_im_ref[1]

    def body(k, carry):
        xr, xi, zr, zi = carry
        cf = pl.ds(k, 1)
        cb = pl.ds(tc - 1 - k, 1)
        xre_ref[cf, :] = xr
        xim_ref[cf, :] = xi
        zre_ref[cb, :] = zr
        zim_ref[cb, :] = zi
        return (flr * xr - fli * xi + fre_ref[cf, :], flr * xi + fli * xr + fim_ref[cf, :],
                blr * zr - bli * zi + bre_ref[cb, :], blr * zi + bli * zr + bim_ref[cb, :])

    out = lax.fori_loop(0, tc, body, tuple(state_ref[j] for j in range(4)))
    for j in range(4):
        state_ref[j] = out[j]


def _s5_rec(fre, fim, bre, bim, lam_re, lam_im, segs):
    m = fre.shape[0]
    tc = min(128, min(l for _, l in segs) // S5_Q)
    _, flags = _tile_tables(segs, tc * S5_Q)
    n = m // tc
    fwd = pl.BlockSpec((tc, S5_NS), lambda i, f: (i, 0))
    bwd = pl.BlockSpec((tc, S5_NS), lambda i, f: (n - 1 - i, 0))
    lam = pl.BlockSpec((2, 1, S5_NS), lambda i, f: (0, 0, 0))
    grid_spec = pltpu.PrefetchScalarGridSpec(
        num_scalar_prefetch=1, grid=(n,), in_specs=[fwd, fwd, bwd, bwd, lam, lam],
        out_specs=[fwd, fwd, bwd, bwd],
        scratch_shapes=[pltpu.VMEM((4, 1, S5_NS), F32)])
    sds = jax.ShapeDtypeStruct((m, S5_NS), F32)
    return pl.pallas_call(
        functools.partial(_s5_rec_kernel, tc=tc), grid_spec=grid_spec, out_shape=[sds] * 4,
        compiler_params=_params(("arbitrary",)),
    )(jnp.asarray(flags), fre, fim, bre, bim, lam_re, lam_im)


def _s5_out_kernel(u_ref, ttc_ref, xre_ref, xim_ref, zre_ref, zim_ref, vcc_ref, dsk_ref, spread_ref, y_ref,
                   tt_ref, vc_ref, *, mt):
    @pl.when(pl.program_id(1) == 0)
    def _():
        _s5_expand(ttc_ref, spread_ref, tt_ref, S5_GROUP, S5_GROUP)
        _s5_expand(vcc_ref, spread_ref, vc_ref, S5_STATE, S5_GROUP)

    u = _s5_blocks(u_ref, mt)
    intra = jnp.dot(u.astype(BF16), tt_ref[...], preferred_element_type=F32)
    st = jnp.concatenate([xre_ref[...], xim_ref[...], zre_ref[...], zim_ref[...]], axis=1).astype(BF16)
    carry = jnp.dot(st, vc_ref[...], preferred_element_type=F32)
    y = intra + carry + dsk_ref[...] * u
    for t in range(S5_Q):
        y_ref[pl.ds(t, mt, stride=S5_Q), :] = y[:, t * LANES:(t + 1) * LANES]


def _s5_out(pm, w, xre, xim, zre, zim):
    n_tok = pm.shape[0]
    tt = min(S5_TT, n_tok)
    mt = tt // S5_Q
    ucb = COL_U_S5 // LANES
    st = pl.BlockSpec((mt, S5_SW), lambda k, i: (i, k))
    return pl.pallas_call(
        functools.partial(_s5_out_kernel, mt=mt),
        out_shape=jax.ShapeDtypeStruct((n_tok, S5_WIDTH), F32), grid=(S5_NSB, n_tok // tt),
        in_specs=[pl.BlockSpec((tt, LANES), lambda k, i: (i, ucb + k)),
                  pl.BlockSpec((None, S5_SBW, S5_CW), lambda k, i: (k, 0, 0)),
                  st, st, st, st,
                  pl.BlockSpec((None, 4 * S5_SW, S5_CW), lambda k, i: (k, 0, 0)),
                  pl.BlockSpec((None, 1, S5_SBW), lambda k, i: (k, 0, 0)),
                  pl.BlockSpec((S5_CW, S5_SBW), lambda k, i: (0, 0))],
        out_specs=pl.BlockSpec((tt, LANES), lambda k, i: (i, k)),
        scratch_shapes=[pltpu.VMEM((S5_SBW, S5_SBW), BF16), pltpu.VMEM((4 * S5_SW, S5_SBW), BF16)],
        compiler_params=_params(("parallel", "arbitrary")),
    )(pm, w['tt'], xre, xim, zre, zim, w['vc'], w['dsk'], _s5_spread_matrix(S5_GROUP))


def _s5_glu_kernel(y_ref, w_ref, b_ref, mix_ref, o_ref):
    del mix_ref
    g = jax.nn.gelu(y_ref[...])
    t = jnp.dot(g.astype(BF16), w_ref[...], preferred_element_type=F32) + b_ref[...]
    o_ref[...] = (g * jax.nn.sigmoid(t)).astype(o_ref.dtype)


def _s5_glu(y, glu_w, layer, glu_b, mix):
    n_tok = y.shape[0]
    tr = 512
    return pl.pallas_call(
        _s5_glu_kernel, out_shape=jax.ShapeDtypeStruct(mix.shape, mix.dtype), grid=(n_tok // tr,),
        input_output_aliases={3: 0},
        in_specs=[pl.BlockSpec((tr, S5_WIDTH), lambda i: (i, 0)),
                  pl.BlockSpec((None, S5_WIDTH, S5_WIDTH), lambda i: (layer, 0, 0)),
                  pl.BlockSpec((1, S5_WIDTH), lambda i: (0, 0)),
                  _WHOLE],
        out_specs=pl.BlockSpec((None, tr, S5_WIDTH), lambda i: (MIX_S5, i, 0)),
        compiler_params=_params(("parallel",)),
    )(y, glu_w, glu_b[None, :], mix)


def _s5(pm, segs, lw, glu_w, layer, mix):
    w = _s5_weights(lw)
    fre, fim, bre, bim = _s5_in(pm, w['wa'])
    xre, xim, zre, zim = _s5_rec(fre, fim, bre, bim, w['lam_re'], w['lam_im'], segs)
    y = _s5_out(pm, w, xre, xim, zre, zim)
    return _s5_glu(y, glu_w, layer, lw['s5_glu_b'], mix)


def _matmul_weights(w_in, w_branch_ssd, w_branch_swa, w_branch_s5, w_branch_na, w_out, ffn_w1, ffn_w3, ffn_w2,
                    s5_glu_w):
    o = IN_OFFS
    w_main = jnp.concatenate([w_in[:, :, :o[2]], w_in[:, :, o[3]:]], axis=2).astype(BF16)
    w_dt_raw = w_in[:, :, o[2]:o[3]]
    zpad = jnp.zeros((DEPTH, D_MODEL, LANES - SSD_HEADS), F32)
    w_dt = jnp.concatenate([w_dt_raw[:, :, :SSD_HEADS], zpad, w_dt_raw[:, :, SSD_HEADS:], zpad],
                           axis=2).astype(BF16)
    w_branch = jnp.stack([w_branch_ssd, w_branch_swa, w_branch_s5, w_branch_na], axis=1).astype(BF16)
    return dict(main=w_main, dt=w_dt, branch=w_branch, out=w_out.astype(BF16),
                w1=ffn_w1.astype(BF16), w3=ffn_w3.astype(BF16), w2=ffn_w2.astype(BF16),
                glu=s5_glu_w.astype(BF16))


def _layer(xs, c8, lw, mw, ada_w, ada_b, layer, split_output):
    mod = _ada(c8, ada_w, ada_b, layer)
    mod4 = mod.reshape(8, 6, 1, D_MODEL)

    h = _norm_mod(xs, lw['norm1_g'], mod4, 1, 0)
    pm = _mm(h, mw['main'], layer, N_MIX, 1024, F32)
    dt_raw = _mm(h, mw['dt'], layer, 2 * LANES, 2 * LANES, F32)

    mix = _ssd(pm, dt_raw, SEGS, lw)
    qs, ks, qn, kn, vn = _qk_prep(pm, SEGS, lw)
    mix = _swa(qs, ks, pm, lw['swa_sink'], SEGS, mix)
    mix = _s5(pm, SEGS, lw, mw['glu'], layer, mix)
    mix = _na(qn, kn, vn, lw['na_rpb'], GROUPS, mix)
    merged = _merge(h, mw['main'], mix, mw['branch'], layer)
    x = _mm_res(merged, mw['out'], layer, xs, mod4, 2, TM, 1024 // len(xs), D_MODEL)

    h2 = _norm_mod((x,), lw['norm2_g'], mod4, 4, 3)
    u = _ffn_up(h2, mw['w1'], mw['w3'], layer)
    tm = TM // 2
    down = functools.partial(_mm_res, u, mw['w2'], layer, (x,), mod4, 5, tm, 512, D_FF)
    if not split_output:
        return down()
    n_first = GROUPS[1][0] // tm
    return down(rows=(0, n_first)), down(rows=(n_first, N_TOK // tm - n_first))


_LAYER_KEYS = ('ada_w', 'ada_b', 'norm1_g', 'norm2_g', 'w_in',
               'ssd_conv_w', 'ssd_conv_b', 'ssd_dt_bias', 'ssd_a_log', 'ssd_d', 'ssd_norm_g',
               'swa_q_norm_g', 'swa_k_norm_g', 'swa_sink',
               's5_a_re', 's5_a_im', 's5_log_step', 's5_b_re', 's5_b_im', 's5_c_re', 's5_c_im',
               's5_d', 's5_glu_w', 's5_glu_b',
               'na_q_norm_g', 'na_k_norm_g', 'na_rpb',
               'w_branch_ssd', 'w_branch_swa', 'w_branch_s5', 'w_branch_na', 'w_out',
               'ffn_w1', 'ffn_w3', 'ffn_w2')


def kernel(x_prompt, x_sample, c_prompt, c_sample, ada_w, ada_b, norm1_g, norm2_g, w_in, ssd_conv_w, ssd_conv_b, ssd_dt_bias, ssd_a_log, ssd_d, ssd_norm_g, swa_q_norm_g, swa_k_norm_g, swa_sink, s5_a_re, s5_a_im, s5_log_step, s5_b_re, s5_b_im, s5_c_re, s5_c_im, s5_d, s5_glu_w, s5_glu_b, na_q_norm_g, na_k_norm_g, na_rpb, w_branch_ssd, w_branch_swa, w_branch_s5, w_branch_na, w_out, ffn_w1, ffn_w3, ffn_w2):
    stacked = dict(zip(_LAYER_KEYS, (ada_w, ada_b, norm1_g, norm2_g, w_in,
                                     ssd_conv_w, ssd_conv_b, ssd_dt_bias, ssd_a_log, ssd_d, ssd_norm_g,
                                     swa_q_norm_g, swa_k_norm_g, swa_sink,
                                     s5_a_re, s5_a_im, s5_log_step, s5_b_re, s5_b_im, s5_c_re, s5_c_im,
                                     s5_d, s5_glu_w, s5_glu_b,
                                     na_q_norm_g, na_k_norm_g, na_rpb,
                                     w_branch_ssd, w_branch_swa, w_branch_s5, w_branch_na, w_out,
                                     ffn_w1, ffn_w3, ffn_w2)))
    xs = (x_prompt.reshape(BATCH * SEQ, D_MODEL), x_sample.reshape(DEC_BATCH * DEC_SEQ, D_MODEL))
    c8 = jnp.concatenate([c_prompt, c_sample, jnp.zeros((8 - N_SEQS, D_MODEL), F32)], axis=0)
    mw = _matmul_weights(w_in, w_branch_ssd, w_branch_swa, w_branch_s5, w_branch_na, w_out, ffn_w1, ffn_w3, ffn_w2,
                         s5_glu_w)
    for i in range(DEPTH):
        lw = {k: v[i] for k, v in stacked.items() if v.size < D_MODEL * D_MODEL}
        last = i == DEPTH - 1
        out = _layer(xs, c8, lw, mw, ada_w, ada_b, i, split_output=last)
        xs = out if last else (out,)
    return (xs[0].reshape(BATCH, SEQ, D_MODEL), xs[1].reshape(DEC_BATCH, DEC_SEQ, D_MODEL))
```
